```python
import math
import jax, jax.numpy as jnp
from jax import lax
import numpy as np

D_MODEL = 1024
BATCH = 16
SEQ = 256
DEPTH = 2
DEC_BATCH = 8
DEC_SEQ = 1024
PAST_LEN = 512

GRID_W = 64
N_MIXERS = 2
N_HY_LAYERS = (DEPTH + 1) // 2
N_S5_LAYERS = DEPTH // 2
D_FF = -(-8 * D_MODEL // (3 * 256)) * 256
EPS = 1e-6
POS_BASE = 10000.0
HY_ORDER = 2
HY_EMB = 33
HY_BANDS = (HY_EMB - 1) // 2
HY_FO = 64
HY_SHORT = 3
HY_TARGET = 1e-2
HY_FAST = 0.3
HY_SLOW = 1.5
S5_GROUP = 16
S5_G = D_MODEL // S5_GROUP
S5_P = 64

kernel_name = 'hyena_s5_prefix_diffusion_step'


def _rmsnorm(x, g):
    xf = x.astype(jnp.float32)
    ms = jnp.mean(xf * xf, axis=-1, keepdims=True)
    return (xf * lax.rsqrt(ms + EPS) * g.astype(jnp.float32)).astype(x.dtype)


def _modulate(h, shift, scale):
    return h * (1 + scale[:, None]) + shift[:, None]


def _swiglu(h, w13, w2):
    a, b = jnp.split(h @ w13, 2, axis=-1)
    return (jax.nn.silu(a) * b) @ w2


def _grid_pos_embed(rows):
    quarter = D_MODEL // 4
    omega = 1.0 / (POS_BASE ** (jnp.arange(quarter, dtype=jnp.float32) / quarter))

    def axis_embed(n):
        ang = jnp.arange(n, dtype=jnp.float32)[:, None] * omega[None]
        return jnp.concatenate([jnp.sin(ang), jnp.cos(ang)], axis=-1)

    er = jnp.broadcast_to(axis_embed(rows)[:, None], (rows, GRID_W, D_MODEL // 2))
    ec = jnp.broadcast_to(axis_embed(GRID_W)[None], (rows, GRID_W, D_MODEL // 2))
    return jnp.concatenate([er, ec], axis=-1).reshape(rows * GRID_W, D_MODEL)


def _short_conv(x, w, b):
    L = x.shape[1]
    xp = jnp.pad(x, ((0, 0), (1, 1), (0, 0)))
    return xp[:, :L] * w[0] + xp[:, 1:L + 1] * w[1] + xp[:, 2:] * w[2] + b


def _hyena_filter_spectrum(L, w1, b1, w2, b2, w3, freq):
    f32 = jnp.float32
    t = jnp.linspace(0.0, 1.0, L, dtype=f32)[:, None]
    w = 2.0 * math.pi * jnp.arange(L, dtype=f32)[:, None] / L
    bands = jnp.linspace(1e-4, HY_BANDS - 1, HY_BANDS, dtype=f32)[None, :]
    z = jnp.concatenate([t, jnp.cos(bands * w), -jnp.sin(bands * w)], axis=-1)
    fr = freq.astype(f32)
    h = jnp.sin(fr * (z @ w1.astype(f32) + b1.astype(f32)))
    h = jnp.sin(fr * (h @ w2.astype(f32) + b2.astype(f32)))
    h = (h @ w3.astype(f32)).reshape(L, HY_ORDER, 2, D_MODEL)
    max_decay = math.log(HY_TARGET) / HY_FAST
    min_decay = math.log(HY_TARGET) / HY_SLOW
    deltas = jnp.abs(jnp.linspace(min_decay, max_decay, D_MODEL, dtype=f32))
    decay = jnp.exp(-t * deltas[None, :])
    h = h * decay[:, None, None, :]
    h = h / (jnp.sum(jnp.abs(h), axis=(0, 2), keepdims=True) + EPS)
    fwd = h[:, :, 0]
    bwd = h[:, :, 1]
    k = jnp.concatenate([fwd, jnp.zeros((1, HY_ORDER, D_MODEL), f32), bwd[:0:-1]], axis=0)
    return jnp.fft.rfft(k, axis=0)


def _fft_longconv(v, k_f, bias):
    L = v.shape[1]
    vf = v.astype(jnp.float32)
    v_f = jnp.fft.rfft(vf, n=2 * L, axis=1)
    y = jnp.fft.irfft(v_f * k_f[None], n=2 * L, axis=1)[:, :L]
    return y + vf * bias.astype(jnp.float32)


def _hyena_mixer(u, j, p):
    L = u.shape[1]
    z = u @ p['hy_in_w'][j] + p['hy_in_b'][j]
    z = _short_conv(z, p['hy_conv_w'][j], p['hy_conv_b'][j])
    v, x1, x2 = jnp.split(z, 3, axis=-1)
    k_f = _hyena_filter_spectrum(L, p['hy_pe_w1'][j], p['hy_pe_b1'][j], p['hy_pe_w2'][j],
                                 p['hy_pe_b2'][j], p['hy_pe_w3'][j], p['hy_freq'][j])
    fb = p['hy_fbias'][j]
    v = _fft_longconv(v, k_f[:, 0], fb[0]) * x1.astype(jnp.float32)
    v = _fft_longconv(v, k_f[:, 1], fb[1]) * x2.astype(jnp.float32)
    return (v @ p['hy_out_w'][j].astype(jnp.float32) + p['hy_out_b'][j].astype(jnp.float32)).astype(u.dtype)


def _s5_discretize(a_re, a_im, log_dt, b_re, b_im):
    f32 = jnp.float32
    a_re = a_re.astype(f32)
    a_im = a_im.astype(f32)
    b_re = b_re.astype(f32)
    b_im = b_im.astype(f32)
    dt = jnp.exp(log_dt.astype(f32))[:, None]
    mag = jnp.exp(dt * a_re)
    ph = dt * a_im
    ab_re = mag * jnp.cos(ph)
    ab_im = mag * jnp.sin(ph)
    nr = ab_re - 1.0
    ni = ab_im
    den = a_re * a_re + a_im * a_im
    co_re = ((nr * a_re + ni * a_im) / den)[..., None]
    co_im = ((ni * a_re - nr * a_im) / den)[..., None]
    bb_re = co_re * b_re - co_im * b_im
    bb_im = co_re * b_im + co_im * b_re
    return ab_re, ab_im, bb_re, bb_im


def _cmul_combine(e1, e2):
    a1r, a1i, b1r, b1i = e1
    a2r, a2i, b2r, b2i = e2
    return (a2r * a1r - a2i * a1i, a2r * a1i + a2i * a1r,
            a2r * b1r - a2i * b1i + b2r, a2r * b1i + a2i * b1r + b2i)


def _s5_scan(ug, ab_re, ab_im, bb_re, bb_im, init):
    L = ug.shape[1]
    bu_re = jnp.einsum('blgh,gph->blgp', ug, bb_re)
    bu_im = jnp.einsum('blgh,gph->blgp', ug, bb_im)
    a_re = jnp.broadcast_to(ab_re[None, None], (1, L) + ab_re.shape)
    a_im = jnp.broadcast_to(ab_im[None, None], (1, L) + ab_im.shape)
    pr, pi, hr, hi = lax.associative_scan(_cmul_combine, (a_re, a_im, bu_re, bu_im), axis=1)
    if init is not None:
        h0r = init[0][:, None]
        h0i = init[1][:, None]
        hr = hr + pr * h0r - pi * h0i
        hi = hi + pr * h0i + pi * h0r
    return hr, hi


def _s5_mixer(u, j, p, h0):
    f32 = jnp.float32
    bsz, L, _ = u.shape
    uf = u.astype(f32)
    ug = uf.reshape(bsz, L, S5_G, S5_GROUP)
    y = uf * p['s5_D'][j].astype(f32)
    finals = []
    for d in range(2):
        ab_re, ab_im, bb_re, bb_im = _s5_discretize(p['s5_A_re'][j, d], p['s5_A_im'][j, d], p['s5_log_dt'][j, d],
                                                    p['s5_B_re'][j, d], p['s5_B_im'][j, d])
        src = ug if d == 0 else jnp.flip(ug, axis=1)
        init = None if h0 is None else (h0[:, d, 0].astype(f32), h0[:, d, 1].astype(f32))
        hr, hi = _s5_scan(src, ab_re, ab_im, bb_re, bb_im, init)
        yd = (jnp.einsum('ghp,blgp->blgh', p['s5_C_re'][j, d].astype(f32), hr)
              - jnp.einsum('ghp,blgp->blgh', p['s5_C_im'][j, d].astype(f32), hi))
        if d == 1:
            yd = jnp.flip(yd, axis=1)
        y = y + yd.reshape(bsz, L, D_MODEL)
        if h0 is None:
            finals.append(jnp.stack([hr[:, -1], hi[:, -1]], axis=1))
    y = jax.nn.gelu(y)
    a, g = jnp.split(y @ p['s5_glu_w'][j].astype(f32) + p['s5_glu_b'][j].astype(f32), 2, axis=-1)
    out = (a * jax.nn.sigmoid(g)).astype(u.dtype)
    fin = jnp.stack(finals, axis=1) if h0 is None else None
    return out, fin


def _trunk(x, cond, s5_state, p):
    finals = []
    for i in range(DEPTH):
        j = i // N_MIXERS
        mod = jax.nn.silu(cond) @ p['ada_w'][i] + p['ada_b'][i]
        sh1, sc1, g1, sh2, sc2, g2 = jnp.split(mod.astype(x.dtype), 6, axis=-1)
        h = _modulate(_rmsnorm(x, p['norm1_g'][i]), sh1, sc1)
        if i % N_MIXERS == 0:
            m = _hyena_mixer(h, j, p)
        else:
            h0 = None if s5_state is None else s5_state[:, j]
            m, fin = _s5_mixer(h, j, p, h0)
            if s5_state is None:
                finals.append(fin)
        x = x + g1[:, None] * m
        h = _modulate(_rmsnorm(x, p['norm2_g'][i]), sh2, sc2)
        x = x + g2[:, None] * _swiglu(h, p['ffn_w13'][i], p['ffn_w2'][i])
    y = _rmsnorm(x, p['final_g'])
    new_state = jnp.stack(finals, axis=1) if s5_state is None else None
    return y, new_state


def setup_inputs(seed: int = 0) -> dict:
    key = jax.random.key(seed)
    ks = list(jax.random.split(key, 40))
    f32 = jnp.float32

    def nrm(k, shape, s):
        return jax.random.normal(k, shape, f32) * s

    D, F, NH, NS = D_MODEL, D_FF, N_HY_LAYERS, N_S5_LAYERS
    G, P, H = S5_G, S5_P, S5_GROUP
    n_idx = jnp.arange(P, dtype=f32)
    return {
        'x_prompt': nrm(ks[0], (BATCH, SEQ, D), 1.0),
        'x_sample': nrm(ks[1], (DEC_BATCH, DEC_SEQ, D), 1.0),
        'state_s5': nrm(ks[2], (DEC_BATCH, NS, 2, 2, G, P), 0.1),
        'c': nrm(ks[3], (DEC_BATCH, D), 1.0),
        'c_ctx': nrm(ks[4], (D,), 1.0),
        'norm1_g': 1.0 + nrm(ks[5], (DEPTH, D), 0.01),
        'norm2_g': 1.0 + nrm(ks[6], (DEPTH, D), 0.01),
        'final_g': 1.0 + nrm(ks[7], (D,), 0.01),
        'ada_w': nrm(ks[8], (DEPTH, D, 6 * D), 0.5 * D ** -0.5),
        'ada_b': nrm(ks[9], (DEPTH, 6 * D), 0.01),
        'ffn_w13': nrm(ks[10], (DEPTH, D, 2 * F), D ** -0.5),
        'ffn_w2': nrm(ks[11], (DEPTH, F, D), F ** -0.5),
        'hy_in_w': nrm(ks[12], (NH, D, 3 * D), D ** -0.5),
        'hy_in_b': nrm(ks[13], (NH, 3 * D), 0.01),
        'hy_conv_w': nrm(ks[14], (NH, HY_SHORT, 3 * D), HY_SHORT ** -0.5),
        'hy_conv_b': nrm(ks[15], (NH, 3 * D), 0.01),
        'hy_pe_w1': nrm(ks[16], (NH, HY_EMB, HY_FO), HY_EMB ** -0.5),
        'hy_pe_b1': nrm(ks[17], (NH, HY_FO), 0.1),
        'hy_pe_w2': nrm(ks[18], (NH, HY_FO, HY_FO), HY_FO ** -0.5),
        'hy_pe_b2': nrm(ks[19], (NH, HY_FO), 0.1),
        'hy_pe_w3': nrm(ks[20], (NH, HY_FO, HY_ORDER * 2 * D), HY_FO ** -0.5),
        'hy_freq': 1.0 + nrm(ks[21], (NH, HY_FO), 0.1),
        'hy_fbias': nrm(ks[22], (NH, HY_ORDER, D), 0.5),
        'hy_out_w': nrm(ks[23], (NH, D, D), D ** -0.5),
        'hy_out_b': nrm(ks[24], (NH, D), 0.01),
        's5_A_re': -0.5 + nrm(ks[25], (NS, 2, G, P), 0.01),
        's5_A_im': math.pi * n_idx + nrm(ks[26], (NS, 2, G, P), 0.01),
        's5_log_dt': jax.random.uniform(ks[27], (NS, 2, G), f32, math.log(1e-3), math.log(1e-1)),
        's5_B_re': nrm(ks[28], (NS, 2, G, P, H), (2 * H) ** -0.5),
        's5_B_im': nrm(ks[29], (NS, 2, G, P, H), (2 * H) ** -0.5),
        's5_C_re': nrm(ks[30], (NS, 2, G, H, P), P ** -0.5),
        's5_C_im': nrm(ks[31], (NS, 2, G, H, P), P ** -0.5),
        's5_D': nrm(ks[32], (NS, D), 1.0),
        's5_glu_w': nrm(ks[33], (NS, D, 2 * D), D ** -0.5),
        's5_glu_b': nrm(ks[34], (NS, 2 * D), 0.01),
    }


def reference(x_prompt, x_sample, state_s5, c, c_ctx, norm1_g, norm2_g, final_g, ada_w, ada_b,
              ffn_w13, ffn_w2, hy_in_w, hy_in_b, hy_conv_w, hy_conv_b, hy_pe_w1, hy_pe_b1,
              hy_pe_w2, hy_pe_b2, hy_pe_w3, hy_freq, hy_fbias, hy_out_w, hy_out_b,
              s5_A_re, s5_A_im, s5_log_dt, s5_B_re, s5_B_im, s5_C_re, s5_C_im, s5_D,
              s5_glu_w, s5_glu_b):
    p = dict(norm1_g=norm1_g, norm2_g=norm2_g, final_g=final_g, ada_w=ada_w, ada_b=ada_b,
             ffn_w13=ffn_w13, ffn_w2=ffn_w2, hy_in_w=hy_in_w, hy_in_b=hy_in_b,
             hy_conv_w=hy_conv_w, hy_conv_b=hy_conv_b, hy_pe_w1=hy_pe_w1, hy_pe_b1=hy_pe_b1,
             hy_pe_w2=hy_pe_w2, hy_pe_b2=hy_pe_b2, hy_pe_w3=hy_pe_w3, hy_freq=hy_freq,
             hy_fbias=hy_fbias, hy_out_w=hy_out_w, hy_out_b=hy_out_b, s5_A_re=s5_A_re,
             s5_A_im=s5_A_im, s5_log_dt=s5_log_dt, s5_B_re=s5_B_re, s5_B_im=s5_B_im,
             s5_C_re=s5_C_re, s5_C_im=s5_C_im, s5_D=s5_D, s5_glu_w=s5_glu_w, s5_glu_b=s5_glu_b)
    y_prompt, new_state_s5 = _trunk(x_prompt, c_ctx[None], None, p)
    rows = x_sample.shape[1] // GRID_W
    lat = x_sample + _grid_pos_embed(rows).astype(x_sample.dtype)[None]
    y_sample, _ = _trunk(lat, c, state_s5, p)
    return (y_prompt, y_sample, new_state_s5)
```

```python
import functools
import math

import numpy as np
import jax
import jax.numpy as jnp
from jax import lax
from jax.experimental import pallas as pl
from jax.experimental.pallas import tpu as pltpu

_F32 = jnp.float32
_BF16 = jnp.bfloat16

_EPS = 1e-6
_GRID_W = 64
_POS_BASE = 10000.0
_HY_ORDER = 2
_HY_EMB = 33
_HY_FO = 64
_HY_TARGET = 1e-2
_HY_FAST = 0.3
_HY_SLOW = 1.5
_S5_H = 16
_S5_P = 64
_S5_CHUNK = 16
_CHUNK_W = _S5_CHUNK * _S5_H

_VMEM_LIMIT = 56 * 1024 * 1024
_TOKEN_TILE = 256
_CONV_DBLK = 256
_FILT_DBLK = 256


def _params(sem):
    return pltpu.CompilerParams(dimension_semantics=sem, vmem_limit_bytes=_VMEM_LIMIT)


def _dot_bf16(a, b):
    return jnp.dot(a.astype(_BF16), b.astype(_BF16), preferred_element_type=_F32)


def _dot_f32(a, b):
    return jnp.dot(a, b, precision=lax.Precision.HIGHEST, preferred_element_type=_F32)


def _split_bf16(m):
    hi = m.astype(_BF16)
    return hi, (m - hi.astype(_F32)).astype(_BF16)


def _dot_split(m_hi, m_lo, x, passes):
    xh = x.astype(_BF16)
    r = jnp.dot(m_hi, xh, preferred_element_type=_F32)
    if passes == 3:
        xl = (x - xh.astype(_F32)).astype(_BF16)
        r = (r + jnp.dot(m_hi, xl, preferred_element_type=_F32)
             + jnp.dot(m_lo, xh, preferred_element_type=_F32))
    return r


def _dot_nt(a, b, precision=None):
    return lax.dot_general(a, b, (((1,), (1,)), ((), ())), precision=precision,
                           preferred_element_type=_F32)


def _rmsnorm(x, g):
    ms = jnp.mean(x * x, axis=-1, keepdims=True)
    return x * lax.rsqrt(ms + _EPS) * g


def _silu(x):
    return x * jax.nn.sigmoid(x)


def _gelu_tanh(x):
    c = math.sqrt(2.0 / math.pi)
    return 0.5 * x * (1.0 + jnp.tanh(c * (x + 0.044715 * (x * x * x))))


def _const_spec(shape):
    nd = len(shape)
    return pl.BlockSpec(shape, lambda *_: (0,) * nd, pipeline_mode=pl.Buffered(1))


def _layer_spec(shape, layer):
    nd = len(shape)
    return pl.BlockSpec((None,) + tuple(shape), lambda *_: (layer,) + (0,) * nd,
                        pipeline_mode=pl.Buffered(1))


def _mod_spec(d, layer, part, row_of_b):
    return pl.BlockSpec((None, None, None, 1, d), lambda b, t: (layer, part, row_of_b(b), 0, 0))


def _mod_kernel(cond_ref, w_ref, b_ref, o_ref):
    o_ref[...] = _dot_f32(_silu(cond_ref[...]), w_ref[...]) + b_ref[...]


def _modulation(cond, ada_w, ada_b):
    depth, d, d6 = ada_w.shape
    parts = d6 // d
    rows = cond.shape[0]
    out = pl.pallas_call(
        _mod_kernel,
        grid=(depth, parts),
        in_specs=[
            pl.BlockSpec((rows, d), lambda l, p: (0, 0)),
            pl.BlockSpec((None, d, d), lambda l, p: (l, 0, p)),
            pl.BlockSpec((None, None, 1, d), lambda l, p: (l, p, 0, 0)),
        ],
        out_specs=pl.BlockSpec((None, None, rows, d), lambda l, p: (l, p, 0, 0)),
        out_shape=jax.ShapeDtypeStruct((depth, parts, rows, d), _F32),
        compiler_params=_params(("arbitrary", "arbitrary")),
        name="mod",
    )(cond, ada_w, ada_b.reshape(depth, parts, 1, d))
    return out.reshape(depth, parts, rows, 1, d)


@functools.lru_cache(maxsize=None)
def _dft_consts(seq):
    f = np.arange(seq)[:, None]
    t = np.arange(seq)[None, :]
    ang = np.pi * ((f * t) % (2 * seq)) / seq
    cos = np.cos(ang)
    sin = np.sin(ang)
    alt = np.where(np.arange(seq) % 2 == 0, 1.0, -1.0)
    fwd_top = cos
    fwd_bot = -sin
    fwd_bot[0] = alt
    wgt = np.full((seq, 1), 2.0)
    wgt[0] = 1.0
    inv_top = (cos * wgt).T / (2 * seq)
    inv_bot = (-2.0 * sin).T / (2 * seq)
    inv_bot[:, 0] = alt / (2 * seq)
    fwd = np.concatenate([fwd_top, fwd_bot], axis=0).astype(np.float32)
    inv = np.concatenate([inv_top, inv_bot], axis=1).astype(np.float32)
    return fwd, inv, alt.astype(np.float32)[:, None]


@functools.lru_cache(maxsize=None)
def _filter_consts(seq, d):
    t = np.linspace(0.0, 1.0, seq)[:, None]
    w = 2.0 * np.pi * np.arange(seq)[:, None] / seq
    nb = (_HY_EMB - 1) // 2
    bands = np.linspace(1e-4, nb - 1, nb)[None, :]
    z = np.concatenate([t, np.cos(bands * w), -np.sin(bands * w)], axis=-1)
    zpad = np.zeros((seq, _HY_FO))
    zpad[:, :_HY_EMB] = z
    max_decay = math.log(_HY_TARGET) / _HY_FAST
    min_decay = math.log(_HY_TARGET) / _HY_SLOW
    deltas = np.abs(np.linspace(min_decay, max_decay, d))[None, :]
    return zpad.astype(np.float32), deltas.astype(np.float32)


@functools.lru_cache(maxsize=None)
def _pos_embed(rows, d):
    quarter = d // 4
    omega = 1.0 / (_POS_BASE ** (np.arange(quarter, dtype=np.float64) / quarter))

    def axis_embed(n):
        ang = np.arange(n, dtype=np.float64)[:, None] * omega[None]
        return np.concatenate([np.sin(ang), np.cos(ang)], axis=-1)

    er = np.broadcast_to(axis_embed(rows)[:, None], (rows, _GRID_W, d // 2))
    ec = np.broadcast_to(axis_embed(_GRID_W)[None], (rows, _GRID_W, d // 2))
    return np.concatenate([er, ec], axis=-1).reshape(rows * _GRID_W, d).astype(np.float32)


@functools.lru_cache(maxsize=None)
def _s5_consts():
    c, h = _S5_CHUNK, _S5_H
    step = np.repeat(np.arange(c), h)
    chan = np.tile(np.arange(h), c)
    cols = np.arange(2 * c)[None, :]
    off = c - 1

    def onehot(expo):
        return (cols == (expo + off)[:, None]).astype(np.float32)

    expand = np.concatenate([
        onehot(step),
        onehot(-step),
        onehot(c - 1 - step),
        onehot(step + 1),
        onehot(c - step),
    ], axis=0)
    tile = (chan[:, None] == np.arange(h)[None, :]).astype(np.float32)
    causal = (step[None, :] >= step[:, None]).astype(np.float32)
    anti = (step[:, None] >= step[None, :]).astype(np.float32)
    return expand, tile, causal, anti


def _filter_kernel(z_ref, w1_ref, b1_ref, w2_ref, b2_ref, fr_ref, w3f_ref, w3b_ref, dl_ref,
                   ath_ref, atl_ref, abh_ref, abl_ref, alt_ref, krt_ref, krb_ref, ki_ref):
    z = z_ref[...]
    fr = fr_ref[...]
    h = jnp.sin(fr * (_dot_f32(z, w1_ref[...]) + b1_ref[...]))
    h = jnp.sin(fr * (_dot_f32(h, w2_ref[...]) + b2_ref[...]))
    decay = jnp.exp(-z[:, 0:1] * dl_ref[...])
    hf = _dot_f32(h, w3f_ref[...]) * decay
    hb = _dot_f32(h, w3b_ref[...]) * decay
    norm = (jnp.sum(jnp.abs(hf), axis=0, keepdims=True)
            + jnp.sum(jnp.abs(hb), axis=0, keepdims=True) + _EPS)
    hf = hf / norm
    hb = hb / norm
    first = lax.broadcasted_iota(jnp.int32, hf.shape, 0) == 0
    hb = jnp.where(first, 0.0, hb)
    ksum = hf + hb
    kdiff = hf - hb
    kre = _dot_split(ath_ref[...], atl_ref[...], ksum, 3)
    kim = _dot_split(abh_ref[...], abl_ref[...], kdiff, 3)
    nyq = jnp.sum(alt_ref[...] * ksum, axis=0, keepdims=True)
    krt_ref[...] = kre
    krb_ref[...] = jnp.where(first, nyq, kre)
    ki_ref[...] = jnp.where(first, 0.0, kim)


def _hyena_filters(seq, d, w1p, b1, w2, b2, freq, w3):
    zpad, deltas = _filter_consts(seq, d)
    fwd, _, alt = _dft_consts(seq)
    fwd_hi, fwd_lo = _split_bf16(jnp.asarray(fwd))
    top = pl.BlockSpec((seq, seq), lambda o, j: (0, 0), pipeline_mode=pl.Buffered(1))
    bot = pl.BlockSpec((seq, seq), lambda o, j: (1, 0), pipeline_mode=pl.Buffered(1))
    nb = d // _FILT_DBLK
    fo = _HY_FO
    out_sds = jax.ShapeDtypeStruct((_HY_ORDER, seq, d), _F32)
    out_spec = pl.BlockSpec((None, seq, _FILT_DBLK), lambda o, j: (o, 0, j))
    return pl.pallas_call(
        _filter_kernel,
        grid=(_HY_ORDER, nb),
        in_specs=[
            _const_spec((seq, fo)), _const_spec((fo, fo)), _const_spec((1, fo)),
            _const_spec((fo, fo)), _const_spec((1, fo)), _const_spec((1, fo)),
            pl.BlockSpec((fo, _FILT_DBLK), lambda o, j: (0, (2 * o) * nb + j)),
            pl.BlockSpec((fo, _FILT_DBLK), lambda o, j: (0, (2 * o + 1) * nb + j)),
            pl.BlockSpec((1, _FILT_DBLK), lambda o, j: (0, j)),
            top, top, bot, bot,
            _const_spec((seq, 1)),
        ],
        out_specs=[out_spec, out_spec, out_spec],
        out_shape=[out_sds, out_sds, out_sds],
        compiler_params=_params(("arbitrary", "arbitrary")),
        name=f"filt{seq}",
    )(jnp.asarray(zpad), w1p, b1, w2, b2, freq, w3, w3, jnp.asarray(deltas),
      fwd_hi, fwd_lo, fwd_hi, fwd_lo, jnp.asarray(alt))


def _pre_kernel(has_pos, *refs):
    if has_pos:
        x_ref, pos_ref, sh_ref, sc_ref, g_ref, w_ref, b_ref, x0_ref, z_ref = refs
        x = x_ref[...] + pos_ref[...]
        x0_ref[...] = x
    else:
        x_ref, sh_ref, sc_ref, g_ref, w_ref, b_ref, z_ref = refs
        x = x_ref[...]
    h = _rmsnorm(x, g_ref[...]) * (1.0 + sc_ref[...]) + sh_ref[...]
    z_ref[...] = _dot_bf16(h, w_ref[...]) + b_ref[...]


def _pre_hyena(x, pos, mod5, row_of_b, layer, norm_g, w_in, b_in):
    bsz, seq, d = x.shape
    n = w_in.shape[-1]
    tm = _TOKEN_TILE
    tok = pl.BlockSpec((None, tm, d), lambda b, t: (b, t, 0))
    ins, specs = [x], [tok]
    if pos is not None:
        ins.append(pos)
        specs.append(pl.BlockSpec((tm, d), lambda b, t: (t, 0)))
    ins += [mod5, mod5, norm_g, w_in, b_in]
    specs += [_mod_spec(d, layer, 0, row_of_b), _mod_spec(d, layer, 1, row_of_b),
              _layer_spec((1, d), layer), _layer_spec((d, n), 0), _layer_spec((1, n), 0)]
    z_spec = pl.BlockSpec((None, tm, n), lambda b, t: (b, t, 0))
    z_sds = jax.ShapeDtypeStruct((bsz, seq, n), _F32)
    if pos is not None:
        out_specs, out_shape = [tok, z_spec], [jax.ShapeDtypeStruct(x.shape, _F32), z_sds]
    else:
        out_specs, out_shape = z_spec, z_sds
    out = pl.pallas_call(
        functools.partial(_pre_kernel, pos is not None),
        grid=(bsz, seq // tm),
        in_specs=specs, out_specs=out_specs, out_shape=out_shape,
        compiler_params=_params(("arbitrary", "arbitrary")),
        name=f"pre{seq}",
    )(*ins)
    return out if pos is not None else (x, out)


def _conv_kernel(zv_ref, z1_ref, z2_ref, wv_ref, w1_ref, w2_ref, bv_ref, b1_ref, b2_ref, fb_ref,
                 krt_ref, krb_ref, ki_ref, fth_ref, ftl_ref, fbh_ref, fbl_ref,
                 ith_ref, itl_ref, ibh_ref, ibl_ref, o_ref):
    seq, dblk = zv_ref.shape
    npass = _CONV_PASSES
    row = lax.broadcasted_iota(jnp.int32, (seq, dblk), 0)
    first = row == 0
    last = row == seq - 1

    def short_conv(z_ref, w_ref, b_ref):
        z = z_ref[...]
        prev = jnp.where(first, 0.0, pltpu.roll(z, 1, 0))
        nxt = jnp.where(last, 0.0, pltpu.roll(z, seq - 1, 0))
        w = w_ref[...]
        return prev * w[0:1] + z * w[1:2] + nxt * w[2:3] + b_ref[...]

    v = short_conv(zv_ref, wv_ref, bv_ref)
    gates = (short_conv(z1_ref, w1_ref, b1_ref), short_conv(z2_ref, w2_ref, b2_ref))
    for o in range(_HY_ORDER):
        p = _dot_split(fth_ref[...], ftl_ref[...], v, npass)
        q = _dot_split(fbh_ref[...], fbl_ref[...], v, npass)
        ki = ki_ref[o]
        ytop = p * krt_ref[o] - q * ki
        ybot = p * ki + q * krb_ref[o]
        y = (_dot_split(ith_ref[...], itl_ref[...], ytop, npass)
             + _dot_split(ibh_ref[...], ibl_ref[...], ybot, npass))
        v = (y + v * fb_ref[o:o + 1, :]) * gates[o]
    o_ref[...] = v.astype(o_ref.dtype)


def _hyena_conv(z, conv_w, conv_b, fbias, krt, krb, ki):
    bsz, seq, d3 = z.shape
    d = d3 // 3
    dblk = _CONV_DBLK
    nb = d // dblk
    fwd, inv, _ = _dft_consts(seq)
    fwd_hi, fwd_lo = _split_bf16(jnp.asarray(fwd))
    inv_hi, inv_lo = _split_bf16(jnp.asarray(inv))

    def mspec(r, c):
        return pl.BlockSpec((seq, seq), lambda j, b: (r, c), pipeline_mode=pl.Buffered(1))

    def zspec(k):
        return pl.BlockSpec((None, seq, dblk), lambda j, b: (b, 0, k * nb + j))

    def wspec(rows, k):
        return pl.BlockSpec((rows, dblk), lambda j, b: (0, k * nb + j))

    kspec = pl.BlockSpec((_HY_ORDER, seq, dblk), lambda j, b: (0, 0, j))
    return pl.pallas_call(
        _conv_kernel,
        grid=(nb, bsz),
        in_specs=[zspec(0), zspec(1), zspec(2),
                  wspec(3, 0), wspec(3, 1), wspec(3, 2),
                  wspec(1, 0), wspec(1, 1), wspec(1, 2),
                  wspec(_HY_ORDER, 0),
                  kspec, kspec, kspec,
                  mspec(0, 0), mspec(0, 0), mspec(1, 0), mspec(1, 0),
                  mspec(0, 0), mspec(0, 0), mspec(0, 1), mspec(0, 1)],
        out_specs=pl.BlockSpec((None, seq, dblk), lambda j, b: (b, 0, j)),
        out_shape=jax.ShapeDtypeStruct((bsz, seq, d), _BF16),
        compiler_params=_params(("arbitrary", "arbitrary")),
        name=f"conv{seq}",
    )(z, z, z, conv_w, conv_w, conv_w, conv_b, conv_b, conv_b, fbias, krt, krb, ki,
      fwd_hi, fwd_lo, fwd_hi, fwd_lo, inv_hi, inv_lo, inv_hi, inv_lo)


_CONV_PASSES = 3


def _post_kernel(kind, tail, names, *refs):
    r = dict(zip(names, refs))
    x = r["x"][...]
    if kind == "hyena":
        m = _dot_bf16(r["mix"][...], r["w_out"][...]) + r["b_out"][...]
    else:
        u = _rmsnorm(x, r["n1"][...]) * (1.0 + r["sc1"][...]) + r["sh1"][...]
        y = _gelu_tanh(r["mix"][...] + u * r["skip"][...])
        ag = _dot_bf16(y, r["w_out"][...]) + r["b_out"][...]
        half = ag.shape[-1] // 2
        m = ag[:, :half] * jax.nn.sigmoid(ag[:, half:])
    x = x + r["g1"][...] * m
    h = _rmsnorm(x, r["n2"][...]) * (1.0 + r["sc2"][...]) + r["sh2"][...]
    ab = _dot_bf16(h, r["w13"][...])
    ff = ab.shape[-1] // 2
    act = _silu(ab[:, :ff]) * ab[:, ff:]
    x = x + r["g2"][...] * _dot_bf16(act, r["w2"][...])
    if tail == "next":
        r["o_x"][...] = x
        un = _rmsnorm(x, r["nn"][...]) * (1.0 + r["scn"][...]) + r["shn"][...]
        r["o_u"][...] = un.astype(r["o_u"].dtype)
    else:
        r["o_x"][...] = _rmsnorm(x, r["nf"][...])


def _post_mixer(kind, tail, x, mix, mod5, row_of_b, layer, w_out, b_out, norm1_g, skip,
                norm2_g, w13, w2, final_g):
    bsz, seq, d = x.shape
    tm = _TOKEN_TILE
    tok = pl.BlockSpec((None, tm, d), lambda b, t: (b, t, 0))
    n_out = w_out.shape[-1]
    f2 = w13.shape[-1]

    def mod(part, lyr=layer):
        return (mod5, _mod_spec(d, lyr, part, row_of_b))

    items = [("x", x, tok), ("mix", mix, tok),
             ("w_out", w_out, _layer_spec((d, n_out), 0)), ("b_out", b_out, _layer_spec((1, n_out), 0)),
             ("g1",) + mod(2), ("sh2",) + mod(3), ("sc2",) + mod(4), ("g2",) + mod(5),
             ("n2", norm2_g, _layer_spec((1, d), layer)),
             ("w13", w13, _layer_spec((d, f2), layer)), ("w2", w2, _layer_spec((f2 // 2, d), layer))]
    if kind == "s5":
        items += [("n1", norm1_g, _layer_spec((1, d), layer)), ("sh1",) + mod(0), ("sc1",) + mod(1),
                  ("skip", skip, _layer_spec((1, d), 0))]
    out_specs, out_shape = [tok], [jax.ShapeDtypeStruct(x.shape, _F32)]
    names_out = ["o_x"]
    if tail == "next":
        items += [("nn", norm1_g, _layer_spec((1, d), layer + 1)),
                  ("shn",) + mod(0, layer + 1), ("scn",) + mod(1, layer + 1)]
        out_specs.append(tok)
        out_shape.append(jax.ShapeDtypeStruct(x.shape, _F32))
        names_out.append("o_u")
    else:
        items.append(("nf", final_g, _const_spec((1, d))))
    names = tuple(i[0] for i in items) + tuple(names_out)
    out = pl.pallas_call(
        functools.partial(_post_kernel, kind, tail, names),
        grid=(bsz, seq // tm),
        in_specs=[i[2] for i in items], out_specs=out_specs, out_shape=out_shape,
        compiler_params=_params(("arbitrary", "arbitrary")),
        name=f"post_{kind}{seq}",
    )(*[i[1] for i in items])
    return out


def _s5_kernel(nchunk, bsz, has_init, *refs):
    if has_init:
        (u_ref, prm_ref, btr_ref, bti_ref, cr_ref, ci_ref, exp_ref, tile_ref, causal_ref, anti_ref,
         h0fr_ref, h0fi_ref, h0br_ref, h0bi_ref, y_ref,
         s_fr, s_fi, s_br, s_bi, p_fr, p_fi, p_br, p_bi) = refs
    else:
        (u_ref, prm_ref, btr_ref, bti_ref, cr_ref, ci_ref, exp_ref, tile_ref, causal_ref, anti_ref,
         y_ref, ffr_ref, ffi_ref, fbr_ref, fbi_ref,
         s_fr, s_fi, s_br, s_bi, p_fr, p_fi, p_br, p_bi) = refs
    cw = _CHUNK_W
    nexp = _S5_CHUNK * 2

    def cmul(ar, ai, br, bi):
        return ar * br - ai * bi, ar * bi + ai * br

    tabs = []
    for d in range(2):
        a_re = prm_ref[d, 0:1, :]
        a_im = prm_ref[d, 1:2, :]
        dt = jnp.exp(prm_ref[d, 2:3, :])
        mag = jnp.exp(dt * a_re)
        ph = dt * a_im
        nr = mag * jnp.cos(ph) - 1.0
        ni = mag * jnp.sin(ph)
        den = a_re * a_re + a_im * a_im
        co_re = (nr * a_re + ni * a_im) / den
        co_im = (ni * a_re - nr * a_im) / den
        bt_re, bt_im = cmul(co_re, co_im, btr_ref[d], bti_ref[d])
        k = (lax.broadcasted_iota(jnp.int32, (nexp, 1), 0) - (_S5_CHUNK - 1)).astype(_F32)
        e = jnp.exp(k * (dt * a_re))
        pw_re = e * jnp.cos(k * ph)
        pw_im = e * jnp.sin(k * ph)
        x_re = _dot_f32(exp_ref[...], pw_re)
        x_im = _dot_f32(exp_ref[...], pw_im)
        tb_re = _dot_f32(tile_ref[...], bt_re)
        tb_im = _dot_f32(tile_ref[...], bt_im)
        tc_re = _dot_f32(tile_ref[...], cr_ref[d])
        tc_im = _dot_f32(tile_ref[...], ci_ref[d])

        xs = [(x_re[i * cw:(i + 1) * cw], x_im[i * cw:(i + 1) * cw]) for i in range(5)]
        tabs.append(dict(tb=(tb_re, tb_im), tc=(tc_re, tc_im), xs=xs,
                         a_chunk=(pw_re[nexp - 1:nexp], pw_im[nexp - 1:nexp])))

    f, b = tabs
    lf = cmul(*f["tb"], *f["xs"][1])
    rf = cmul(*f["tc"], *f["xs"][0])
    lb = cmul(*b["tb"], *b["xs"][0])
    rb = cmul(*b["tc"], *b["xs"][1])
    hi = lax.Precision.HIGHEST
    m = (causal_ref[...] * (_dot_nt(lf[0], rf[0], hi) - _dot_nt(lf[1], rf[1], hi))
         + anti_ref[...] * (_dot_nt(lb[0], rb[0], hi) - _dot_nt(lb[1], rb[1], hi)))
    wf = cmul(*f["tb"], *f["xs"][2])
    wb = lb
    vf = cmul(*f["tc"], *f["xs"][3])
    vb = cmul(*b["tc"], *b["xs"][4])

    u = u_ref[...]
    s_fr[...] = _dot_bf16(u, wf[0])
    s_fi[...] = _dot_bf16(u, wf[1])
    s_br[...] = _dot_bf16(u, wb[0])
    s_bi[...] = _dot_bf16(u, wb[1])

    afr, afi = f["a_chunk"]
    abr, abi = b["a_chunk"]
    if has_init:
        init = (h0fr_ref[...], h0fi_ref[...], h0br_ref[...], h0bi_ref[...])
    else:
        zero = jnp.zeros((bsz, _S5_P), _F32)
        init = (zero, zero, zero, zero)

    def step(kk, carry):
        hfr, hfi, hbr, hbi = carry
        rowf = pl.ds(pl.multiple_of(kk * bsz, bsz), bsz)
        rowb = pl.ds(pl.multiple_of((nchunk - 1 - kk) * bsz, bsz), bsz)
        p_fr[rowf, :] = hfr
        p_fi[rowf, :] = hfi
        p_br[rowb, :] = hbr
        p_bi[rowb, :] = hbi
        nfr = afr * hfr - afi * hfi + s_fr[rowf, :]
        nfi = afr * hfi + afi * hfr + s_fi[rowf, :]
        nbr = abr * hbr - abi * hbi + s_br[rowb, :]
        nbi = abr * hbi + abi * hbr + s_bi[rowb, :]
        return nfr, nfi, nbr, nbi

    fin = lax.fori_loop(0, nchunk, step, init)
    if not has_init:
        ffr_ref[...] = fin[0]
        ffi_ref[...] = fin[1]
        fbr_ref[...] = fin[2]
        fbi_ref[...] = fin[3]

    def nt(a, bm):
        return _dot_nt(a.astype(_BF16), bm.astype(_BF16))

    y_ref[...] = (_dot_bf16(u, m)
                  + nt(p_fr[...], vf[0]) - nt(p_fi[...], vf[1])
                  + nt(p_br[...], vb[0]) - nt(p_bi[...], vb[1]))


def _s5_scan(u, prm, bt_re, bt_im, c_re, c_im, init):
    bsz, seq, d = u.shape
    g = d // _S5_H
    nchunk = seq // _S5_CHUNK
    rows = nchunk * bsz
    cw = _CHUNK_W
    p = _S5_P
    ut = (u.astype(_BF16).reshape(bsz, nchunk, _S5_CHUNK, g, _S5_H)
          .transpose(3, 1, 0, 2, 4).reshape(g, rows, cw))
    expand, tile, causal, anti = (jnp.asarray(a) for a in _s5_consts())
    gspec4 = pl.BlockSpec((None, 2, _S5_H, p), lambda i: (i, 0, 0, 0))
    st_spec = pl.BlockSpec((None, bsz, p), lambda i: (i, 0, 0))
    st_sds = jax.ShapeDtypeStruct((g, bsz, p), _F32)
    ins = [ut, prm, bt_re, bt_im, c_re, c_im, expand, tile, causal, anti]
    specs = [pl.BlockSpec((None, rows, cw), lambda i: (i, 0, 0)),
             pl.BlockSpec((None, 2, 8, p), lambda i: (i, 0, 0, 0)),
             gspec4, gspec4, gspec4, gspec4,
             _const_spec(expand.shape), _const_spec(tile.shape),
             _const_spec(causal.shape), _const_spec(anti.shape)]
    y_spec = pl.BlockSpec((None, rows, cw), lambda i: (i, 0, 0))
    y_sds = jax.ShapeDtypeStruct((g, rows, cw), _F32)
    if init is not None:
        ins += list(init)
        specs += [st_spec] * 4
        out_specs, out_shape = y_spec, y_sds
    else:
        out_specs, out_shape = [y_spec] + [st_spec] * 4, [y_sds] + [st_sds] * 4
    out = pl.pallas_call(
        functools.partial(_s5_kernel, nchunk, bsz, init is not None),
        grid=(g,),
        in_specs=specs, out_specs=out_specs, out_shape=out_shape,
        scratch_shapes=[pltpu.VMEM((rows, p), _F32)] * 8,
        compiler_params=_params(("arbitrary",)),
        name=f"s5_{seq}",
    )(*ins)
    yt = out if init is not None else out[0]
    y = (yt.reshape(g, nchunk, bsz, _S5_CHUNK, _S5_H).transpose(2, 1, 3, 0, 4).reshape(bsz, seq, d))
    return y, (None if init is not None else out[1:])


def _trunk(x, pos, row_of_b, init_state, mod5, filt, wts):
    y0, z = _pre_hyena(x, pos, mod5, row_of_b, 0, wts["norm1_g"], wts["hy_in_w"], wts["hy_in_b"])
    v = _hyena_conv(z, wts["hy_conv_w"], wts["hy_conv_b"], wts["hy_fbias"], *filt)
    x1, u = _post_mixer("hyena", "next", y0, v, mod5, row_of_b, 0, wts["hy_out_w"], wts["hy_out_b"],
                        wts["norm1_g"], None, wts["norm2_g"], wts["ffn_w13"], wts["ffn_w2"], None)
    ys, fin = _s5_scan(u, wts["s5_prm"], wts["s5_bt_re"], wts["s5_bt_im"], wts["s5_c_re"], wts["s5_c_im"],
                       init_state)
    (out,) = _post_mixer("s5", "final", x1, ys, mod5, row_of_b, 1, wts["s5_glu_w"], wts["s5_glu_b"],
                         wts["norm1_g"], wts["s5_D"], wts["norm2_g"], wts["ffn_w13"], wts["ffn_w2"],
                         wts["final_g"])
    return out, fin


def kernel(x_prompt, x_sample, state_s5, c, c_ctx, norm1_g, norm2_g, final_g, ada_w, ada_b, ffn_w13, ffn_w2, hy_in_w, hy_in_b, hy_conv_w, hy_conv_b, hy_pe_w1, hy_pe_b1, hy_pe_w2, hy_pe_b2, hy_pe_w3, hy_freq, hy_fbias, hy_out_w, hy_out_b, s5_A_re, s5_A_im, s5_log_dt, s5_B_re, s5_B_im, s5_C_re, s5_C_im, s5_D, s5_glu_w, s5_glu_b):
    depth, d = norm1_g.shape
    assert depth == 2 and hy_in_w.shape[0] == 1 and s5_glu_w.shape[0] == 1
    dec_b, dec_seq, _ = x_sample.shape
    g = d // _S5_H

    nrow = -(-(1 + dec_b) // 8) * 8
    cond = jnp.concatenate([c_ctx[None], c, jnp.zeros((nrow - 1 - dec_b, d), _F32)], axis=0)
    mod5 = _modulation(cond, ada_w, ada_b)

    row3 = lambda a: a.reshape(a.shape[0], 1, a.shape[-1])
    wts = dict(
        norm1_g=row3(norm1_g), norm2_g=row3(norm2_g), final_g=final_g[None],
        hy_in_w=hy_in_w.astype(_BF16), hy_in_b=row3(hy_in_b),
        hy_conv_w=hy_conv_w[0], hy_conv_b=hy_conv_b, hy_fbias=hy_fbias[0],
        hy_out_w=hy_out_w.astype(_BF16), hy_out_b=row3(hy_out_b),
        ffn_w13=ffn_w13.astype(_BF16), ffn_w2=ffn_w2.astype(_BF16),
        s5_glu_w=s5_glu_w.astype(_BF16), s5_glu_b=row3(s5_glu_b), s5_D=row3(s5_D),
    )
    ldt = jnp.broadcast_to(s5_log_dt[0][:, :, None], (2, g, _S5_P))
    prm = jnp.stack([s5_A_re[0], s5_A_im[0], ldt] + [jnp.zeros((2, g, _S5_P), _F32)] * 5, axis=2)
    wts["s5_prm"] = prm.transpose(1, 0, 2, 3)
    wts["s5_bt_re"] = s5_B_re[0].transpose(1, 0, 3, 2)
    wts["s5_bt_im"] = s5_B_im[0].transpose(1, 0, 3, 2)
    wts["s5_c_re"] = s5_C_re[0].transpose(1, 0, 2, 3)
    wts["s5_c_im"] = s5_C_im[0].transpose(1, 0, 2, 3)

    w1p = jnp.pad(hy_pe_w1[0], ((0, _HY_FO - _HY_EMB), (0, 0)))
    filt_args = (w1p, hy_pe_b1, hy_pe_w2[0], hy_pe_b2, hy_freq, hy_pe_w3[0])
    filt_ctx = _hyena_filters(x_prompt.shape[1], d, *filt_args)
    filt_lat = _hyena_filters(dec_seq, d, *filt_args)

    y_prompt, fin = _trunk(x_prompt, None, lambda b: 0, None, mod5, filt_ctx, wts)
    new_state = (jnp.stack([jnp.stack([fin[0], fin[1]]), jnp.stack([fin[2], fin[3]])])
                 .transpose(3, 0, 1, 2, 4)[:, None])

    st = state_s5[:, 0].transpose(1, 2, 3, 0, 4)
    init = (st[0, 0], st[0, 1], st[1, 0], st[1, 1])
    pos = jnp.asarray(_pos_embed(dec_seq // _GRID_W, d))
    y_sample, _ = _trunk(x_sample, pos, lambda b: b + 1, init, mod5, filt_lat, wts)
    return (y_prompt, y_sample, new_state)
```

```python
import functools
import math

import numpy as np
import jax
import jax.numpy as jnp
from jax import lax
from jax.experimental import pallas as pl
from jax.experimental.pallas import tpu as pltpu

_F32 = jnp.float32
_BF16 = jnp.bfloat16

_EPS = 1e-6
_GRID_W = 64
_POS_BASE = 10000.0
_HY_ORDER = 2
_HY_EMB = 33
_HY_FO = 64
_HY_TARGET = 1e-2
_HY_FAST = 0.3
_HY_SLOW = 1.5
_S5_H = 16
_S5_P = 64
_S5_CHUNK = 16
_CHUNK_W = _S5_CHUNK * _S5_H

_VMEM_LIMIT = 56 * 1024 * 1024
_TOKEN_TILE = 256
_CONV_DBLK = 256
_FILT_DBLK = 256


def _params(sem):
    return pltpu.CompilerParams(dimension_semantics=sem, vmem_limit_bytes=_VMEM_LIMIT)


def _dot_bf16(a, b):
    return jnp.dot(a.astype(_BF16), b.astype(_BF16), preferred_element_type=_F32)


def _dot_f32(a, b):
    return jnp.dot(a, b, precision=lax.Precision.HIGHEST, preferred_element_type=_F32)


def _split_bf16(m):
    hi = m.astype(_BF16)
    return hi, (m - hi.astype(_F32)).astype(_BF16)


def _dot_split(m_hi, m_lo, x, passes):
    xh = x.astype(_BF16)
    r = jnp.dot(m_hi, xh, preferred_element_type=_F32)
    if passes == 3:
        xl = (x - xh.astype(_F32)).astype(_BF16)
        r = (r + jnp.dot(m_hi, xl, preferred_element_type=_F32)
             + jnp.dot(m_lo, xh, preferred_element_type=_F32))
    return r


def _dot_nt(a, b, precision=None):
    return lax.dot_general(a, b, (((1,), (1,)), ((), ())), precision=precision,
                           preferred_element_type=_F32)


def _rmsnorm(x, g):
    ms = jnp.mean(x * x, axis=-1, keepdims=True)
    return x * lax.rsqrt(ms + _EPS) * g


def _silu(x):
    return x * jax.nn.sigmoid(x)


def _gelu_tanh(x):
    c = math.sqrt(2.0 / math.pi)
    return 0.5 * x * (1.0 + jnp.tanh(c * (x + 0.044715 * (x * x * x))))


def _const_spec(shape):
    nd = len(shape)
    return pl.BlockSpec(shape, lambda *_: (0,) * nd, pipeline_mode=pl.Buffered(1))


def _layer_spec(shape, layer):
    nd = len(shape)
    return pl.BlockSpec((None,) + tuple(shape), lambda *_: (layer,) + (0,) * nd,
                        pipeline_mode=pl.Buffered(1))


def _mod_spec(d, layer, part, row_of_b):
    return pl.BlockSpec((None, None, None, 1, d), lambda b, t: (layer, part, row_of_b(b), 0, 0))


def _mod_kernel(cond_ref, w_ref, b_ref, o_ref):
    o_ref[...] = _dot_f32(_silu(cond_ref[...]), w_ref[...]) + b_ref[...]


def _modulation(cond, ada_w, ada_b):
    depth, d, d6 = ada_w.shape
    parts = d6 // d
    rows = cond.shape[0]
    out = pl.pallas_call(
        _mod_kernel,
        grid=(depth, parts),
        in_specs=[
            pl.BlockSpec((rows, d), lambda l, p: (0, 0)),
            pl.BlockSpec((None, d, d), lambda l, p: (l, 0, p)),
            pl.BlockSpec((None, None, 1, d), lambda l, p: (l, p, 0, 0)),
        ],
        out_specs=pl.BlockSpec((None, None, rows, d), lambda l, p: (l, p, 0, 0)),
        out_shape=jax.ShapeDtypeStruct((depth, parts, rows, d), _F32),
        compiler_params=_params(("arbitrary", "arbitrary")),
        name="mod",
    )(cond, ada_w, ada_b.reshape(depth, parts, 1, d))
    return out.reshape(depth, parts, rows, 1, d)


@functools.lru_cache(maxsize=None)
def _dft_consts(seq):
    f = np.arange(seq)[:, None]
    t = np.arange(seq)[None, :]
    ang = np.pi * ((f * t) % (2 * seq)) / seq
    cos = np.cos(ang)
    sin = np.sin(ang)
    alt = np.where(np.arange(seq) % 2 == 0, 1.0, -1.0)
    fwd_top = cos
    fwd_bot = -sin
    fwd_bot[0] = alt
    wgt = np.full((seq, 1), 2.0)
    wgt[0] = 1.0
    inv_top = (cos * wgt).T / (2 * seq)
    inv_bot = (-2.0 * sin).T / (2 * seq)
    inv_bot[:, 0] = alt / (2 * seq)
    fwd = np.concatenate([fwd_top, fwd_bot], axis=0).astype(np.float32)
    inv = np.concatenate([inv_top, inv_bot], axis=1).astype(np.float32)
    return fwd, inv, alt.astype(np.float32)[:, None]


@functools.lru_cache(maxsize=None)
def _filter_consts(seq, d):
    t = np.linspace(0.0, 1.0, seq)[:, None]
    w = 2.0 * np.pi * np.arange(seq)[:, None] / seq
    nb = (_HY_EMB - 1) // 2
    bands = np.linspace(1e-4, nb - 1, nb)[None, :]
    z = np.concatenate([t, np.cos(bands * w), -np.sin(bands * w)], axis=-1)
    zpad = np.zeros((seq, _HY_FO))
    zpad[:, :_HY_EMB] = z
    max_decay = math.log(_HY_TARGET) / _HY_FAST
    min_decay = math.log(_HY_TARGET) / _HY_SLOW
    deltas = np.abs(np.linspace(min_decay, max_decay, d))[None, :]
    return zpad.astype(np.float32), deltas.astype(np.float32)


@functools.lru_cache(maxsize=None)
def _pos_embed(rows, d):
    quarter = d // 4
    omega = 1.0 / (_POS_BASE ** (np.arange(quarter, dtype=np.float64) / quarter))

    def axis_embed(n):
        ang = np.arange(n, dtype=np.float64)[:, None] * omega[None]
        return np.concatenate([np.sin(ang), np.cos(ang)], axis=-1)

    er = np.broadcast_to(axis_embed(rows)[:, None], (rows, _GRID_W, d // 2))
    ec = np.broadcast_to(axis_embed(_GRID_W)[None], (rows, _GRID_W, d // 2))
    return np.concatenate([er, ec], axis=-1).reshape(rows * _GRID_W, d).astype(np.float32)


@functools.lru_cache(maxsize=None)
def _s5_consts():
    c, h = _S5_CHUNK, _S5_H
    step = np.repeat(np.arange(c), h)
    chan = np.tile(np.arange(h), c)
    cols = np.arange(2 * c)[None, :]
    off = c - 1

    def onehot(expo):
        return (cols == (expo + off)[:, None]).astype(np.float32)

    expand = np.concatenate([
        onehot(step),
        onehot(-step),
        onehot(c - 1 - step),
        onehot(step + 1),
        onehot(c - step),
    ], axis=0)
    tile = (chan[:, None] == np.arange(h)[None, :]).astype(np.float32)
    causal = (step[None, :] >= step[:, None]).astype(np.float32)
    anti = (step[:, None] >= step[None, :]).astype(np.float32)
    return expand, tile, causal, anti


def _filter_kernel(z_ref, w1_ref, b1_ref, w2_ref, b2_ref, fr_ref, w3f_ref, w3b_ref, dl_ref,
                   ath_ref, atl_ref, abh_ref, abl_ref, alt_ref, krt_ref, krb_ref, ki_ref):
    z = z_ref[...]
    fr = fr_ref[...]
    h = jnp.sin(fr * (_dot_f32(z, w1_ref[...]) + b1_ref[...]))
    h = jnp.sin(fr * (_dot_f32(h, w2_ref[...]) + b2_ref[...]))
    decay = jnp.exp(-z[:, 0:1] * dl_ref[...])
    hf = _dot_f32(h, w3f_ref[...]) * decay
    hb = _dot_f32(h, w3b_ref[...]) * decay
    norm = (jnp.sum(jnp.abs(hf), axis=0, keepdims=True)
            + jnp.sum(jnp.abs(hb), axis=0, keepdims=True) + _EPS)
    hf = hf / norm
    hb = hb / norm
    first = lax.broadcasted_iota(jnp.int32, hf.shape, 0) == 0
    hb = jnp.where(first, 0.0, hb)
    ksum = hf + hb
    kdiff = hf - hb
    kre = _dot_split(ath_ref[...], atl_ref[...], ksum, 3)
    kim = _dot_split(abh_ref[...], abl_ref[...], kdiff, 3)
    nyq = jnp.sum(alt_ref[...] * ksum, axis=0, keepdims=True)
    krt_ref[...] = kre
    krb_ref[...] = jnp.where(first, nyq, kre)
    ki_ref[...] = jnp.where(first, 0.0, kim)


def _hyena_filters(seq, d, w1p, b1, w2, b2, freq, w3):
    zpad, deltas = _filter_consts(seq, d)
    fwd, _, alt = _dft_consts(seq)
    fwd_hi, fwd_lo = _split_bf16(jnp.asarray(fwd))
    top = pl.BlockSpec((seq, seq), lambda o, j: (0, 0), pipeline_mode=pl.Buffered(1))
    bot = pl.BlockSpec((seq, seq), lambda o, j: (1, 0), pipeline_mode=pl.Buffered(1))
    nb = d // _FILT_DBLK
    fo = _HY_FO
    out_sds = jax.ShapeDtypeStruct((_HY_ORDER, seq, d), _F32)
    out_spec = pl.BlockSpec((None, seq, _FILT_DBLK), lambda o, j: (o, 0, j))
    return pl.pallas_call(
        _filter_kernel,
        grid=(_HY_ORDER, nb),
        in_specs=[
            _const_spec((seq, fo)), _const_spec((fo, fo)), _const_spec((1, fo)),
            _const_spec((fo, fo)), _const_spec((1, fo)), _const_spec((1, fo)),
            pl.BlockSpec((fo, _FILT_DBLK), lambda o, j: (0, (2 * o) * nb + j)),
            pl.BlockSpec((fo, _FILT_DBLK), lambda o, j: (0, (2 * o + 1) * nb + j)),
            pl.BlockSpec((1, _FILT_DBLK), lambda o, j: (0, j)),
            top, top, bot, bot,
            _const_spec((seq, 1)),
        ],
        out_specs=[out_spec, out_spec, out_spec],
        out_shape=[out_sds, out_sds, out_sds],
        compiler_params=_params(("arbitrary", "arbitrary")),
        name=f"filt{seq}",
    )(jnp.asarray(zpad), w1p, b1, w2, b2, freq, w3, w3, jnp.asarray(deltas),
      fwd_hi, fwd_lo, fwd_hi, fwd_lo, jnp.asarray(alt))


def _pre_kernel(has_pos, *refs):
    if has_pos:
        x_ref, pos_ref, sh_ref, sc_ref, g_ref, w_ref, b_ref, x0_ref, z_ref = refs
        x = x_ref[...] + pos_ref[...]
        x0_ref[...] = x
    else:
        x_ref, sh_ref, sc_ref, g_ref, w_ref, b_ref, z_ref = refs
        x = x_ref[...]
    h = _rmsnorm(x, g_ref[...]) * (1.0 + sc_ref[...]) + sh_ref[...]
    z_ref[...] = _dot_bf16(h, w_ref[...]) + b_ref[...]


def _pre_hyena(x, pos, mod5, row_of_b, layer, norm_g, w_in, b_in):
    bsz, seq, d = x.shape
    n = w_in.shape[-1]
    tm = _TOKEN_TILE
    tok = pl.BlockSpec((None, tm, d), lambda b, t: (b, t, 0))
    ins, specs = [x], [tok]
    if pos is not None:
        ins.append(pos)
        specs.append(pl.BlockSpec((tm, d), lambda b, t: (t, 0)))
    ins += [mod5, mod5, norm_g, w_in, b_in]
    specs += [_mod_spec(d, layer, 0, row_of_b), _mod_spec(d, layer, 1, row_of_b),
              _layer_spec((1, d), layer), _layer_spec((d, n), 0), _layer_spec((1, n), 0)]
    z_spec = pl.BlockSpec((None, tm, n), lambda b, t: (b, t, 0))
    z_sds = jax.ShapeDtypeStruct((bsz, seq, n), _F32)
    if pos is not None:
        out_specs, out_shape = [tok, z_spec], [jax.ShapeDtypeStruct(x.shape, _F32), z_sds]
    else:
        out_specs, out_shape = z_spec, z_sds
    out = pl.pallas_call(
        functools.partial(_pre_kernel, pos is not None),
        grid=(bsz, seq // tm),
        in_specs=specs, out_specs=out_specs, out_shape=out_shape,
        compiler_params=_params(("arbitrary", "arbitrary")),
        name=f"pre{seq}",
    )(*ins)
    return out if pos is not None else (x, out)


def _conv_kernel(zv_ref, z1_ref, z2_ref, wv_ref, w1_ref, w2_ref, bv_ref, b1_ref, b2_ref, fb_ref,
                 krt_ref, krb_ref, ki_ref, fth_ref, ftl_ref, fbh_ref, fbl_ref,
                 ith_ref, itl_ref, ibh_ref, ibl_ref, o_ref):
    seq, dblk = zv_ref.shape
    npass = _CONV_PASSES
    row = lax.broadcasted_iota(jnp.int32, (seq, dblk), 0)
    first = row == 0
    last = row == seq - 1

    def short_conv(z_ref, w_ref, b_ref):
        z = z_ref[...]
        prev = jnp.where(first, 0.0, pltpu.roll(z, 1, 0))
        nxt = jnp.where(last, 0.0, pltpu.roll(z, seq - 1, 0))
        w = w_ref[...]
        return prev * w[0:1] + z * w[1:2] + nxt * w[2:3] + b_ref[...]

    v = short_conv(zv_ref, wv_ref, bv_ref)
    gates = (short_conv(z1_ref, w1_ref, b1_ref), short_conv(z2_ref, w2_ref, b2_ref))
    for o in range(_HY_ORDER):
        p = _dot_split(fth_ref[...], ftl_ref[...], v, npass)
        q = _dot_split(fbh_ref[...], fbl_ref[...], v, npass)
        ki = ki_ref[o]
        ytop = p * krt_ref[o] - q * ki
        ybot = p * ki + q * krb_ref[o]
        y = (_dot_split(ith_ref[...], itl_ref[...], ytop, npass)
             + _dot_split(ibh_ref[...], ibl_ref[...], ybot, npass))
        v = (y + v * fb_ref[o:o + 1, :]) * gates[o]
    o_ref[...] = v.astype(o_ref.dtype)


def _hyena_conv(z, conv_w, conv_b, fbias, krt, krb, ki):
    bsz, seq, d3 = z.shape
    d = d3 // 3
    dblk = _CONV_DBLK
    nb = d // dblk
    fwd, inv, _ = _dft_consts(seq)
    fwd_hi, fwd_lo = _split_bf16(jnp.asarray(fwd))
    inv_hi, inv_lo = _split_bf16(jnp.asarray(inv))

    def mspec(r, c):
        return pl.BlockSpec((seq, seq), lambda j, b: (r, c), pipeline_mode=pl.Buffered(1))

    def zspec(k):
        return pl.BlockSpec((None, seq, dblk), lambda j, b: (b, 0, k * nb + j))

    def wspec(rows, k):
        return pl.BlockSpec((rows, dblk), lambda j, b: (0, k * nb + j))

    kspec = pl.BlockSpec((_HY_ORDER, seq, dblk), lambda j, b: (0, 0, j))
    return pl.pallas_call(
        _conv_kernel,
        grid=(nb, bsz),
        in_specs=[zspec(0), zspec(1), zspec(2),
                  wspec(3, 0), wspec(3, 1), wspec(3, 2),
                  wspec(1, 0), wspec(1, 1), wspec(1, 2),
                  wspec(_HY_ORDER, 0),
                  kspec, kspec, kspec,
                  mspec(0, 0), mspec(0, 0), mspec(1, 0), mspec(1, 0),
                  mspec(0, 0), mspec(0, 0), mspec(0, 1), mspec(0, 1)],
        out_specs=pl.BlockSpec((None, seq, dblk), lambda j, b: (b, 0, j)),
        out_shape=jax.ShapeDtypeStruct((bsz, seq, d), _BF16),
        compiler_params=_params(("arbitrary", "arbitrary")),
        name=f"conv{seq}",
    )(z, z, z, conv_w, conv_w, conv_w, conv_b, conv_b, conv_b, fbias, krt, krb, ki,
      fwd_hi, fwd_lo, fwd_hi, fwd_lo, inv_hi, inv_lo, inv_hi, inv_lo)


_CONV_PASSES = 1


def _post_kernel(kind, tail, names, *refs):
    r = dict(zip(names, refs))
    x = r["x"][...]
    if kind == "hyena":
        m = _dot_bf16(r["mix"][...], r["w_out"][...]) + r["b_out"][...]
    else:
        u = _rmsnorm(x, r["n1"][...]) * (1.0 + r["sc1"][...]) + r["sh1"][...]
        y = _gelu_tanh(r["mix"][...] + u * r["skip"][...])
        ag = _dot_bf16(y, r["w_out"][...]) + r["b_out"][...]
        half = ag.shape[-1] // 2
        m = ag[:, :half] * jax.nn.sigmoid(ag[:, half:])
    x = x + r["g1"][...] * m
    h = _rmsnorm(x, r["n2"][...]) * (1.0 + r["sc2"][...]) + r["sh2"][...]
    ab = _dot_bf16(h, r["w13"][...])
    ff = ab.shape[-1] // 2
    act = _silu(ab[:, :ff]) * ab[:, ff:]
    x = x + r["g2"][...] * _dot_bf16(act, r["w2"][...])
    if tail == "next":
        r["o_x"][...] = x
        un = _rmsnorm(x, r["nn"][...]) * (1.0 + r["scn"][...]) + r["shn"][...]
        r["o_u"][...] = un.astype(r["o_u"].dtype)
    else:
        r["o_x"][...] = _rmsnorm(x, r["nf"][...])


def _post_mixer(kind, tail, x, mix, mod5, row_of_b, layer, w_out, b_out, norm1_g, skip,
                norm2_g, w13, w2, final_g):
    bsz, seq, d = x.shape
    tm = _TOKEN_TILE
    tok = pl.BlockSpec((None, tm, d), lambda b, t: (b, t, 0))
    n_out = w_out.shape[-1]
    f2 = w13.shape[-1]

    def mod(part, lyr=layer):
        return (mod5, _mod_spec(d, lyr, part, row_of_b))

    items = [("x", x, tok), ("mix", mix, tok),
             ("w_out", w_out, _layer_spec((d, n_out), 0)), ("b_out", b_out, _layer_spec((1, n_out), 0)),
             ("g1",) + mod(2), ("sh2",) + mod(3), ("sc2",) + mod(4), ("g2",) + mod(5),
             ("n2", norm2_g, _layer_spec((1, d), layer)),
             ("w13", w13, _layer_spec((d, f2), layer)), ("w2", w2, _layer_spec((f2 // 2, d), layer))]
    if kind == "s5":
        items += [("n1", norm1_g, _layer_spec((1, d), layer)), ("sh1",) + mod(0), ("sc1",) + mod(1),
                  ("skip", skip, _layer_spec((1, d), 0))]
    out_specs, out_shape = [tok], [jax.ShapeDtypeStruct(x.shape, _F32)]
    names_out = ["o_x"]
    if tail == "next":
        items += [("nn", norm1_g, _layer_spec((1, d), layer + 1)),
                  ("shn",) + mod(0, layer + 1), ("scn",) + mod(1, layer + 1)]
        out_specs.append(tok)
        out_shape.append(jax.ShapeDtypeStruct(x.shape, _F32))
        names_out.append("o_u")
    else:
        items.append(("nf", final_g, _const_spec((1, d))))
    names = tuple(i[0] for i in items) + tuple(names_out)
    out = pl.pallas_call(
        functools.partial(_post_kernel, kind, tail, names),
        grid=(bsz, seq // tm),
        in_specs=[i[2] for i in items], out_specs=out_specs, out_shape=out_shape,
        compiler_params=_params(("arbitrary", "arbitrary")),
        name=f"post_{kind}{seq}",
    )(*[i[1] for i in items])
    return out


def _s5_kernel(nchunk, bsz, has_init, *refs):
    if has_init:
        (u_ref, prm_ref, btr_ref, bti_ref, cr_ref, ci_ref, exp_ref, tile_ref, causal_ref, anti_ref,
         h0fr_ref, h0fi_ref, h0br_ref, h0bi_ref, y_ref,
         s_fr, s_fi, s_br, s_bi, p_fr, p_fi, p_br, p_bi) = refs
    else:
        (u_ref, prm_ref, btr_ref, bti_ref, cr_ref, ci_ref, exp_ref, tile_ref, causal_ref, anti_ref,
         y_ref, ffr_ref, ffi_ref, fbr_ref, fbi_ref,
         s_fr, s_fi, s_br, s_bi, p_fr, p_fi, p_br, p_bi) = refs
    cw = _CHUNK_W
    nexp = _S5_CHUNK * 2

    def cmul(ar, ai, br, bi):
        return ar * br - ai * bi, ar * bi + ai * br

    tabs = []
    for d in range(2):
        a_re = prm_ref[d, 0:1, :]
        a_im = prm_ref[d, 1:2, :]
        dt = jnp.exp(prm_ref[d, 2:3, :])
        mag = jnp.exp(dt * a_re)
        ph = dt * a_im
        nr = mag * jnp.cos(ph) - 1.0
        ni = mag * jnp.sin(ph)
        den = a_re * a_re + a_im * a_im
        co_re = (nr * a_re + ni * a_im) / den
        co_im = (ni * a_re - nr * a_im) / den
        bt_re, bt_im = cmul(co_re, co_im, btr_ref[d], bti_ref[d])
        k = (lax.broadcasted_iota(jnp.int32, (nexp, 1), 0) - (_S5_CHUNK - 1)).astype(_F32)
        e = jnp.exp(k * (dt * a_re))
        pw_re = e * jnp.cos(k * ph)
        pw_im = e * jnp.sin(k * ph)
        x_re = _dot_f32(exp_ref[...], pw_re)
        x_im = _dot_f32(exp_ref[...], pw_im)
        tb_re = _dot_f32(tile_ref[...], bt_re)
        tb_im = _dot_f32(tile_ref[...], bt_im)
        tc_re = _dot_f32(tile_ref[...], cr_ref[d])
        tc_im = _dot_f32(tile_ref[...], ci_ref[d])

        xs = [(x_re[i * cw:(i + 1) * cw], x_im[i * cw:(i + 1) * cw]) for i in range(5)]
        tabs.append(dict(tb=(tb_re, tb_im), tc=(tc_re, tc_im), xs=xs,
                         a_chunk=(pw_re[nexp - 1:nexp], pw_im[nexp - 1:nexp])))

    f, b = tabs
    lf = cmul(*f["tb"], *f["xs"][1])
    rf = cmul(*f["tc"], *f["xs"][0])
    lb = cmul(*b["tb"], *b["xs"][0])
    rb = cmul(*b["tc"], *b["xs"][1])
    hi = lax.Precision.HIGHEST
    m = (causal_ref[...] * (_dot_nt(lf[0], rf[0], hi) - _dot_nt(lf[1], rf[1], hi))
         + anti_ref[...] * (_dot_nt(lb[0], rb[0], hi) - _dot_nt(lb[1], rb[1], hi)))
    wf = cmul(*f["tb"], *f["xs"][2])
    wb = lb
    vf = cmul(*f["tc"], *f["xs"][3])
    vb = cmul(*b["tc"], *b["xs"][4])

    u = u_ref[...]
    s_fr[...] = _dot_bf16(u, wf[0])
    s_fi[...] = _dot_bf16(u, wf[1])
    s_br[...] = _dot_bf16(u, wb[0])
    s_bi[...] = _dot_bf16(u, wb[1])

    afr, afi = f["a_chunk"]
    abr, abi = b["a_chunk"]
    if has_init:
        init = (h0fr_ref[...], h0fi_ref[...], h0br_ref[...], h0bi_ref[...])
    else:
        zero = jnp.zeros((bsz, _S5_P), _F32)
        init = (zero, zero, zero, zero)

    def step(kk, carry):
        hfr, hfi, hbr, hbi = carry
        rowf = pl.ds(pl.multiple_of(kk * bsz, bsz), bsz)
        rowb = pl.ds(pl.multiple_of((nchunk - 1 - kk) * bsz, bsz), bsz)
        p_fr[rowf, :] = hfr
        p_fi[rowf, :] = hfi
        p_br[rowb, :] = hbr
        p_bi[rowb, :] = hbi
        nfr = afr * hfr - afi * hfi + s_fr[rowf, :]
        nfi = afr * hfi + afi * hfr + s_fi[rowf, :]
        nbr = abr * hbr - abi * hbi + s_br[rowb, :]
        nbi = abr * hbi + abi * hbr + s_bi[rowb, :]
        return nfr, nfi, nbr, nbi

    fin = lax.fori_loop(0, nchunk, step, init)
    if not has_init:
        ffr_ref[...] = fin[0]
        ffi_ref[...] = fin[1]
        fbr_ref[...] = fin[2]
        fbi_ref[...] = fin[3]

    def nt(a, bm):
        return _dot_nt(a.astype(_BF16), bm.astype(_BF16))

    y_ref[...] = (_dot_bf16(u, m)
                  + nt(p_fr[...], vf[0]) - nt(p_fi[...], vf[1])
                  + nt(p_br[...], vb[0]) - nt(p_bi[...], vb[1]))


def _s5_scan(u, prm, bt_re, bt_im, c_re, c_im, init):
    bsz, seq, d = u.shape
    g = d // _S5_H
    nchunk = seq // _S5_CHUNK
    rows = nchunk * bsz
    cw = _CHUNK_W
    p = _S5_P
    ut = (u.astype(_BF16).reshape(bsz, nchunk, _S5_CHUNK, g, _S5_H)
          .transpose(3, 1, 0, 2, 4).reshape(g, rows, cw))
    expand, tile, causal, anti = (jnp.asarray(a) for a in _s5_consts())
    gspec4 = pl.BlockSpec((None, 2, _S5_H, p), lambda i: (i, 0, 0, 0))
    st_spec = pl.BlockSpec((None, bsz, p), lambda i: (i, 0, 0))
    st_sds = jax.ShapeDtypeStruct((g, bsz, p), _F32)
    ins = [ut, prm, bt_re, bt_im, c_re, c_im, expand, tile, causal, anti]
    specs = [pl.BlockSpec((None, rows, cw), lambda i: (i, 0, 0)),
             pl.BlockSpec((None, 2, 8, p), lambda i: (i, 0, 0, 0)),
             gspec4, gspec4, gspec4, gspec4,
             _const_spec(expand.shape), _const_spec(tile.shape),
             _const_spec(causal.shape), _const_spec(anti.shape)]
    y_spec = pl.BlockSpec((None, rows, cw), lambda i: (i, 0, 0))
    y_sds = jax.ShapeDtypeStruct((g, rows, cw), _F32)
    if init is not None:
        ins += list(init)
        specs += [st_spec] * 4
        out_specs, out_shape = y_spec, y_sds
    else:
        out_specs, out_shape = [y_spec] + [st_spec] * 4, [y_sds] + [st_sds] * 4
    out = pl.pallas_call(
        functools.partial(_s5_kernel, nchunk, bsz, init is not None),
        grid=(g,),
        in_specs=specs, out_specs=out_specs, out_shape=out_shape,
        scratch_shapes=[pltpu.VMEM((rows, p), _F32)] * 8,
        compiler_params=_params(("arbitrary",)),
        name=f"s5_{seq}",
    )(*ins)
    yt = out if init is not None else out[0]
    y = (yt.reshape(g, nchunk, bsz, _S5_CHUNK, _S5_H).transpose(2, 1, 3, 0, 4).reshape(bsz, seq, d))
    return y, (None if init is not None else out[1:])


def _trunk(x, pos, row_of_b, init_state, mod5, filt, wts):
    y0, z = _pre_hyena(x, pos, mod5, row_of_b, 0, wts["norm1_g"], wts["hy_in_w"], wts["hy_in_b"])
    v = _hyena_conv(z, wts["hy_conv_w"], wts["hy_conv_b"], wts["hy_fbias"], *filt)
    x1, u = _post_mixer("hyena", "next", y0, v, mod5, row_of_b, 0, wts["hy_out_w"], wts["hy_out_b"],
                        wts["norm1_g"], None, wts["norm2_g"], wts["ffn_w13"], wts["ffn_w2"], None)
    ys, fin = _s5_scan(u, wts["s5_prm"], wts["s5_bt_re"], wts["s5_bt_im"], wts["s5_c_re"], wts["s5_c_im"],
                       init_state)
    (out,) = _post_mixer("s5", "final", x1, ys, mod5, row_of_b, 1, wts["s5_glu_w"], wts["s5_glu_b"],
                         wts["norm1_g"], wts["s5_D"], wts["norm2_g"], wts["ffn_w13"], wts["ffn_w2"],
                         wts["final_g"])
    return out, fin


def kernel(x_prompt, x_sample, state_s5, c, c_ctx, norm1_g, norm2_g, final_g, ada_w, ada_b, ffn_w13, ffn_w2, hy_in_w, hy_in_b, hy_conv_w, hy_conv_b, hy_pe_w1, hy_pe_b1, hy_pe_w2, hy_pe_b2, hy_pe_w3, hy_freq, hy_fbias, hy_out_w, hy_out_b, s5_A_re, s5_A_im, s5_log_dt, s5_B_re, s5_B_im, s5_C_re, s5_C_im, s5_D, s5_glu_w, s5_glu_b):
    depth, d = norm1_g.shape
    assert depth == 2 and hy_in_w.shape[0] == 1 and s5_glu_w.shape[0] == 1
    dec_b, dec_seq, _ = x_sample.shape
    g = d // _S5_H

    nrow = -(-(1 + dec_b) // 8) * 8
    cond = jnp.concatenate([c_ctx[None], c, jnp.zeros((nrow - 1 - dec_b, d), _F32)], axis=0)
    mod5 = _modulation(cond, ada_w, ada_b)

    row3 = lambda a: a.reshape(a.shape[0], 1, a.shape[-1])
    wts = dict(
        norm1_g=row3(norm1_g), norm2_g=row3(norm2_g), final_g=final_g[None],
        hy_in_w=hy_in_w.astype(_BF16), hy_in_b=row3(hy_in_b),
        hy_conv_w=hy_conv_w[0], hy_conv_b=hy_conv_b, hy_fbias=hy_fbias[0],
        hy_out_w=hy_out_w.astype(_BF16), hy_out_b=row3(hy_out_b),
        ffn_w13=ffn_w13.astype(_BF16), ffn_w2=ffn_w2.astype(_BF16),
        s5_glu_w=s5_glu_w.astype(_BF16), s5_glu_b=row3(s5_glu_b), s5_D=row3(s5_D),
    )
    ldt = jnp.broadcast_to(s5_log_dt[0][:, :, None], (2, g, _S5_P))
    prm = jnp.stack([s5_A_re[0], s5_A_im[0], ldt] + [jnp.zeros((2, g, _S5_P), _F32)] * 5, axis=2)
    wts["s5_prm"] = prm.transpose(1, 0, 2, 3)
    wts["s5_bt_re"] = s5_B_re[0].transpose(1, 0, 3, 2)
    wts["s5_bt_im"] = s5_B_im[0].transpose(1, 0, 3, 2)
    wts["s5_c_re"] = s5_C_re[0].transpose(1, 0, 2, 3)
    wts["s5_c_im"] = s5_C_im[0].transpose(1, 0, 2, 3)

    w1p = jnp.pad(hy_pe_w1[0], ((0, _HY_FO - _HY_EMB), (0, 0)))
    filt_args = (w1p, hy_pe_b1, hy_pe_w2[0], hy_pe_b2, hy_freq, hy_pe_w3[0])
    filt_ctx = _hyena_filters(x_prompt.shape[1], d, *filt_args)
    filt_lat = _hyena_filters(dec_seq, d, *filt_args)

    y_prompt, fin = _trunk(x_prompt, None, lambda b: 0, None, mod5, filt_ctx, wts)
    new_state = (jnp.stack([jnp.stack([fin[0], fin[1]]), jnp.stack([fin[2], fin[3]])])
                 .transpose(3, 0, 1, 2, 4)[:, None])

    st = state_s5[:, 0].transpose(1, 2, 3, 0, 4)
    init = (st[0, 0], st[0, 1], st[1, 0], st[1, 1])
    pos = jnp.asarray(_pos_embed(dec_seq // _GRID_W, d))
    y_sample, _ = _trunk(x_sample, pos, lambda b: b + 1, init, mod5, filt_lat, wts)
    return (y_prompt, y_sample, new_state)
```

```python
import functools
import math

import numpy as np
import jax
import jax.numpy as jnp
from jax import lax
from jax.experimental import pallas as pl
from jax.experimental.pallas import tpu as pltpu

_F32 = jnp.float32
_BF16 = jnp.bfloat16

_EPS = 1e-6
_GRID_W = 64
_POS_BASE = 10000.0
_HY_ORDER = 2
_HY_EMB = 33
_HY_FO = 64
_HY_TARGET = 1e-2
_HY_FAST = 0.3
_HY_SLOW = 1.5
_S5_H = 16
_S5_P = 64
_S5_CHUNK = 16
_CHUNK_W = _S5_CHUNK * _S5_H

_VMEM_LIMIT = 56 * 1024 * 1024
_TOKEN_TILE = 256
_CONV_DBLK = 256
_FILT_DBLK = 256


def _params(sem):
    return pltpu.CompilerParams(dimension_semantics=sem, vmem_limit_bytes=_VMEM_LIMIT)


def _dot_bf16(a, b):
    return jnp.dot(a.astype(_BF16), b.astype(_BF16), preferred_element_type=_F32)


def _dot_f32(a, b):
    return jnp.dot(a, b, precision=lax.Precision.HIGHEST, preferred_element_type=_F32)


def _split_bf16(m):
    hi = m.astype(_BF16)
    return hi, (m - hi.astype(_F32)).astype(_BF16)


def _dot_split(m_hi, m_lo, x, passes):
    xh = x.astype(_BF16)
    r = jnp.dot(m_hi, xh, preferred_element_type=_F32)
    if passes == 3:
        xl = (x - xh.astype(_F32)).astype(_BF16)
        r = (r + jnp.dot(m_hi, xl, preferred_element_type=_F32)
             + jnp.dot(m_lo, xh, preferred_element_type=_F32))
    return r


def _dot_nt(a, b, precision=None):
    return lax.dot_general(a, b, (((1,), (1,)), ((), ())), precision=precision,
                           preferred_element_type=_F32)


def _rmsnorm(x, g):
    ms = jnp.mean(x * x, axis=-1, keepdims=True)
    return x * lax.rsqrt(ms + _EPS) * g


def _silu(x):
    return x * jax.nn.sigmoid(x)


def _gelu_tanh(x):
    c = math.sqrt(2.0 / math.pi)
    return 0.5 * x * (1.0 + jnp.tanh(c * (x + 0.044715 * (x * x * x))))


def _const_spec(shape):
    nd = len(shape)
    return pl.BlockSpec(shape, lambda *_: (0,) * nd, pipeline_mode=pl.Buffered(1))


def _layer_spec(shape, layer):
    nd = len(shape)
    return pl.BlockSpec((None,) + tuple(shape), lambda *_: (layer,) + (0,) * nd,
                        pipeline_mode=pl.Buffered(1))


def _mod_spec(d, layer, part, row_of_b):
    return pl.BlockSpec((None, None, None, 1, d), lambda b, t: (layer, part, row_of_b(b), 0, 0))


def _mod_kernel(cond_ref, w_ref, b_ref, o_ref):
    o_ref[...] = _dot_f32(_silu(cond_ref[...]), w_ref[...]) + b_ref[...]


def _modulation(cond, ada_w, ada_b):
    depth, d, d6 = ada_w.shape
    parts = d6 // d
    rows = cond.shape[0]
    out = pl.pallas_call(
        _mod_kernel,
        grid=(depth, parts),
        in_specs=[
            pl.BlockSpec((rows, d), lambda l, p: (0, 0)),
            pl.BlockSpec((None, d, d), lambda l, p: (l, 0, p)),
            pl.BlockSpec((None, None, 1, d), lambda l, p: (l, p, 0, 0)),
        ],
        out_specs=pl.BlockSpec((None, None, rows, d), lambda l, p: (l, p, 0, 0)),
        out_shape=jax.ShapeDtypeStruct((depth, parts, rows, d), _F32),
        compiler_params=_params(("arbitrary", "arbitrary")),
        name="mod",
    )(cond, ada_w, ada_b.reshape(depth, parts, 1, d))
    return out.reshape(depth, parts, rows, 1, d)


@functools.lru_cache(maxsize=None)
def _dft_consts(seq):
    f = np.arange(seq)[:, None]
    t = np.arange(seq)[None, :]
    ang = np.pi * ((f * t) % (2 * seq)) / seq
    cos = np.cos(ang)
    sin = np.sin(ang)
    alt = np.where(np.arange(seq) % 2 == 0, 1.0, -1.0)
    fwd_top = cos
    fwd_bot = -sin
    fwd_bot[0] = alt
    wgt = np.full((seq, 1), 2.0)
    wgt[0] = 1.0
    inv_top = (cos * wgt).T / (2 * seq)
    inv_bot = (-2.0 * sin).T / (2 * seq)
    inv_bot[:, 0] = alt / (2 * seq)
    fwd = np.concatenate([fwd_top, fwd_bot], axis=0).astype(np.float32)
    inv = np.concatenate([inv_top, inv_bot], axis=1).astype(np.float32)
    return fwd, inv, alt.astype(np.float32)[:, None]


@functools.lru_cache(maxsize=None)
def _filter_consts(seq, d):
    t = np.linspace(0.0, 1.0, seq)[:, None]
    w = 2.0 * np.pi * np.arange(seq)[:, None] / seq
    nb = (_HY_EMB - 1) // 2
    bands = np.linspace(1e-4, nb - 1, nb)[None, :]
    z = np.concatenate([t, np.cos(bands * w), -np.sin(bands * w)], axis=-1)
    zpad = np.zeros((seq, _HY_FO))
    zpad[:, :_HY_EMB] = z
    max_decay = math.log(_HY_TARGET) / _HY_FAST
    min_decay = math.log(_HY_TARGET) / _HY_SLOW
    deltas = np.abs(np.linspace(min_decay, max_decay, d))[None, :]
    return zpad.astype(np.float32), deltas.astype(np.float32)


@functools.lru_cache(maxsize=None)
def _pos_embed(rows, d):
    quarter = d // 4
    omega = 1.0 / (_POS_BASE ** (np.arange(quarter, dtype=np.float64) / quarter))

    def axis_embed(n):
        ang = np.arange(n, dtype=np.float64)[:, None] * omega[None]
        return np.concatenate([np.sin(ang), np.cos(ang)], axis=-1)

    er = np.broadcast_to(axis_embed(rows)[:, None], (rows, _GRID_W, d // 2))
    ec = np.broadcast_to(axis_embed(_GRID_W)[None], (rows, _GRID_W, d // 2))
    return np.concatenate([er, ec], axis=-1).reshape(rows * _GRID_W, d).astype(np.float32)


@functools.lru_cache(maxsize=None)
def _s5_masks():
    step = np.repeat(np.arange(_S5_CHUNK), _S5_H)
    causal = (step[None, :] >= step[:, None]).astype(np.float32)
    anti = (step[:, None] >= step[None, :]).astype(np.float32)
    return causal, anti


def _filter_kernel(z_ref, w1_ref, b1_ref, w2_ref, b2_ref, fr_ref, w3f_ref, w3b_ref, dl_ref,
                   ath_ref, atl_ref, abh_ref, abl_ref, alt_ref, krt_ref, krb_ref, ki_ref):
    z = z_ref[...]
    fr = fr_ref[...]
    h = jnp.sin(fr * (_dot_f32(z, w1_ref[...]) + b1_ref[...]))
    h = jnp.sin(fr * (_dot_f32(h, w2_ref[...]) + b2_ref[...]))
    decay = jnp.exp(-z[:, 0:1] * dl_ref[...])
    hf = _dot_f32(h, w3f_ref[...]) * decay
    hb = _dot_f32(h, w3b_ref[...]) * decay
    norm = (jnp.sum(jnp.abs(hf), axis=0, keepdims=True)
            + jnp.sum(jnp.abs(hb), axis=0, keepdims=True) + _EPS)
    hf = hf / norm
    hb = hb / norm
    first = lax.broadcasted_iota(jnp.int32, hf.shape, 0) == 0
    hb = jnp.where(first, 0.0, hb)
    ksum = hf + hb
    kdiff = hf - hb
    kre = _dot_split(ath_ref[...], atl_ref[...], ksum, 3)
    kim = _dot_split(abh_ref[...], abl_ref[...], kdiff, 3)
    nyq = jnp.sum(alt_ref[...] * ksum, axis=0, keepdims=True)
    krt_ref[...] = kre
    krb_ref[...] = jnp.where(first, nyq, kre)
    ki_ref[...] = jnp.where(first, 0.0, kim)


def _hyena_filters(seq, d, w1p, b1, w2, b2, freq, w3):
    zpad, deltas = _filter_consts(seq, d)
    fwd, _, alt = _dft_consts(seq)
    fwd_hi, fwd_lo = _split_bf16(jnp.asarray(fwd))
    top = pl.BlockSpec((seq, seq), lambda o, j: (0, 0), pipeline_mode=pl.Buffered(1))
    bot = pl.BlockSpec((seq, seq), lambda o, j: (1, 0), pipeline_mode=pl.Buffered(1))
    nb = d // _FILT_DBLK
    fo = _HY_FO
    out_sds = jax.ShapeDtypeStruct((_HY_ORDER, seq, d), _F32)
    out_spec = pl.BlockSpec((None, seq, _FILT_DBLK), lambda o, j: (o, 0, j))
    return pl.pallas_call(
        _filter_kernel,
        grid=(_HY_ORDER, nb),
        in_specs=[
            _const_spec((seq, fo)), _const_spec((fo, fo)), _const_spec((1, fo)),
            _const_spec((fo, fo)), _const_spec((1, fo)), _const_spec((1, fo)),
            pl.BlockSpec((fo, _FILT_DBLK), lambda o, j: (0, (2 * o) * nb + j)),
            pl.BlockSpec((fo, _FILT_DBLK), lambda o, j: (0, (2 * o + 1) * nb + j)),
            pl.BlockSpec((1, _FILT_DBLK), lambda o, j: (0, j)),
            top, top, bot, bot,
            _const_spec((seq, 1)),
        ],
        out_specs=[out_spec, out_spec, out_spec],
        out_shape=[out_sds, out_sds, out_sds],
        compiler_params=_params(("arbitrary", "arbitrary")),
        name=f"filt{seq}",
    )(jnp.asarray(zpad), w1p, b1, w2, b2, freq, w3, w3, jnp.asarray(deltas),
      fwd_hi, fwd_lo, fwd_hi, fwd_lo, jnp.asarray(alt))


def _pre_kernel(has_pos, *refs):
    if has_pos:
        x_ref, pos_ref, sh_ref, sc_ref, g_ref, w_ref, b_ref, x0_ref, z_ref = refs
        x = x_ref[...] + pos_ref[...]
        x0_ref[...] = x
    else:
        x_ref, sh_ref, sc_ref, g_ref, w_ref, b_ref, z_ref = refs
        x = x_ref[...]
    h = _rmsnorm(x, g_ref[...]) * (1.0 + sc_ref[...]) + sh_ref[...]
    z_ref[...] = _dot_bf16(h, w_ref[...]) + b_ref[...]


def _pre_hyena(x, pos, mod5, row_of_b, layer, norm_g, w_in, b_in):
    bsz, seq, d = x.shape
    n = w_in.shape[-1]
    tm = _TOKEN_TILE
    tok = pl.BlockSpec((None, tm, d), lambda b, t: (b, t, 0))
    ins, specs = [x], [tok]
    if pos is not None:
        ins.append(pos)
        specs.append(pl.BlockSpec((tm, d), lambda b, t: (t, 0)))
    ins += [mod5, mod5, norm_g, w_in, b_in]
    specs += [_mod_spec(d, layer, 0, row_of_b), _mod_spec(d, layer, 1, row_of_b),
              _layer_spec((1, d), layer), _layer_spec((d, n), 0), _layer_spec((1, n), 0)]
    z_spec = pl.BlockSpec((None, tm, n), lambda b, t: (b, t, 0))
    z_sds = jax.ShapeDtypeStruct((bsz, seq, n), _F32)
    if pos is not None:
        out_specs, out_shape = [tok, z_spec], [jax.ShapeDtypeStruct(x.shape, _F32), z_sds]
    else:
        out_specs, out_shape = z_spec, z_sds
    out = pl.pallas_call(
        functools.partial(_pre_kernel, pos is not None),
        grid=(bsz, seq // tm),
        in_specs=specs, out_specs=out_specs, out_shape=out_shape,
        compiler_params=_params(("arbitrary", "arbitrary")),
        name=f"pre{seq}",
    )(*ins)
    return out if pos is not None else (x, out)


def _conv_kernel(zv_ref, z1_ref, z2_ref, wv_ref, w1_ref, w2_ref, bv_ref, b1_ref, b2_ref, fb_ref,
                 krt_ref, krb_ref, ki_ref, fth_ref, ftl_ref, fbh_ref, fbl_ref,
                 ith_ref, itl_ref, ibh_ref, ibl_ref, o_ref):
    seq, dblk = zv_ref.shape
    npass = _CONV_PASSES
    row = lax.broadcasted_iota(jnp.int32, (seq, dblk), 0)
    first = row == 0
    last = row == seq - 1

    def short_conv(z_ref, w_ref, b_ref):
        z = z_ref[...]
        prev = jnp.where(first, 0.0, pltpu.roll(z, 1, 0))
        nxt = jnp.where(last, 0.0, pltpu.roll(z, seq - 1, 0))
        w = w_ref[...]
        return prev * w[0:1] + z * w[1:2] + nxt * w[2:3] + b_ref[...]

    v = short_conv(zv_ref, wv_ref, bv_ref)
    gates = (short_conv(z1_ref, w1_ref, b1_ref), short_conv(z2_ref, w2_ref, b2_ref))
    for o in range(_HY_ORDER):
        p = _dot_split(fth_ref[...], ftl_ref[...], v, npass)
        q = _dot_split(fbh_ref[...], fbl_ref[...], v, npass)
        ki = ki_ref[o]
        ytop = p * krt_ref[o] - q * ki
        ybot = p * ki + q * krb_ref[o]
        y = (_dot_split(ith_ref[...], itl_ref[...], ytop, npass)
             + _dot_split(ibh_ref[...], ibl_ref[...], ybot, npass))
        v = (y + v * fb_ref[o:o + 1, :]) * gates[o]
    o_ref[...] = v.astype(o_ref.dtype)


def _hyena_conv(z, conv_w, conv_b, fbias, krt, krb, ki):
    bsz, seq, d3 = z.shape
    d = d3 // 3
    dblk = _CONV_DBLK
    nb = d // dblk
    fwd, inv, _ = _dft_consts(seq)
    fwd_hi, fwd_lo = _split_bf16(jnp.asarray(fwd))
    inv_hi, inv_lo = _split_bf16(jnp.asarray(inv))

    def mspec(r, c):
        return pl.BlockSpec((seq, seq), lambda j, b: (r, c), pipeline_mode=pl.Buffered(1))

    def zspec(k):
        return pl.BlockSpec((None, seq, dblk), lambda j, b: (b, 0, k * nb + j))

    def wspec(rows, k):
        return pl.BlockSpec((rows, dblk), lambda j, b: (0, k * nb + j))

    kspec = pl.BlockSpec((_HY_ORDER, seq, dblk), lambda j, b: (0, 0, j))
    return pl.pallas_call(
        _conv_kernel,
        grid=(nb, bsz),
        in_specs=[zspec(0), zspec(1), zspec(2),
                  wspec(3, 0), wspec(3, 1), wspec(3, 2),
                  wspec(1, 0), wspec(1, 1), wspec(1, 2),
                  wspec(_HY_ORDER, 0),
                  kspec, kspec, kspec,
                  mspec(0, 0), mspec(0, 0), mspec(1, 0), mspec(1, 0),
                  mspec(0, 0), mspec(0, 0), mspec(0, 1), mspec(0, 1)],
        out_specs=pl.BlockSpec((None, seq, dblk), lambda j, b: (b, 0, j)),
        out_shape=jax.ShapeDtypeStruct((bsz, seq, d), _BF16),
        compiler_params=_params(("arbitrary", "arbitrary")),
        name=f"conv{seq}",
    )(z, z, z, conv_w, conv_w, conv_w, conv_b, conv_b, conv_b, fbias, krt, krb, ki,
      fwd_hi, fwd_lo, fwd_hi, fwd_lo, inv_hi, inv_lo, inv_hi, inv_lo)


_CONV_PASSES = 1


def _post_kernel(kind, tail, names, *refs):
    r = dict(zip(names, refs))
    x = r["x"][...]
    if kind == "hyena":
        m = _dot_bf16(r["mix"][...], r["w_out"][...]) + r["b_out"][...]
    else:
        u = _rmsnorm(x, r["n1"][...]) * (1.0 + r["sc1"][...]) + r["sh1"][...]
        y = _gelu_tanh(r["mix"][...] + u * r["skip"][...])
        ag = _dot_bf16(y, r["w_out"][...]) + r["b_out"][...]
        half = ag.shape[-1] // 2
        m = ag[:, :half] * jax.nn.sigmoid(ag[:, half:])
    x = x + r["g1"][...] * m
    h = _rmsnorm(x, r["n2"][...]) * (1.0 + r["sc2"][...]) + r["sh2"][...]
    ab = _dot_bf16(h, r["w13"][...])
    ff = ab.shape[-1] // 2
    act = _silu(ab[:, :ff]) * ab[:, ff:]
    x = x + r["g2"][...] * _dot_bf16(act, r["w2"][...])
    if tail == "next":
        r["o_x"][...] = x
        un = _rmsnorm(x, r["nn"][...]) * (1.0 + r["scn"][...]) + r["shn"][...]
        r["o_u"][...] = un.astype(r["o_u"].dtype)
    else:
        r["o_x"][...] = _rmsnorm(x, r["nf"][...])


def _post_mixer(kind, tail, x, mix, mod5, row_of_b, layer, w_out, b_out, norm1_g, skip,
                norm2_g, w13, w2, final_g):
    bsz, seq, d = x.shape
    tm = _TOKEN_TILE
    tok = pl.BlockSpec((None, tm, d), lambda b, t: (b, t, 0))
    n_out = w_out.shape[-1]
    f2 = w13.shape[-1]

    def mod(part, lyr=layer):
        return (mod5, _mod_spec(d, lyr, part, row_of_b))

    items = [("x", x, tok), ("mix", mix, tok),
             ("w_out", w_out, _layer_spec((d, n_out), 0)), ("b_out", b_out, _layer_spec((1, n_out), 0)),
             ("g1",) + mod(2), ("sh2",) + mod(3), ("sc2",) + mod(4), ("g2",) + mod(5),
             ("n2", norm2_g, _layer_spec((1, d), layer)),
             ("w13", w13, _layer_spec((d, f2), layer)), ("w2", w2, _layer_spec((f2 // 2, d), layer))]
    if kind == "s5":
        items += [("n1", norm1_g, _layer_spec((1, d), layer)), ("sh1",) + mod(0), ("sc1",) + mod(1),
                  ("skip", skip, _layer_spec((1, d), 0))]
    out_specs, out_shape = [tok], [jax.ShapeDtypeStruct(x.shape, _F32)]
    names_out = ["o_x"]
    if tail == "next":
        items += [("nn", norm1_g, _layer_spec((1, d), layer + 1)),
                  ("shn",) + mod(0, layer + 1), ("scn",) + mod(1, layer + 1)]
        out_specs.append(tok)
        out_shape.append(jax.ShapeDtypeStruct(x.shape, _F32))
        names_out.append("o_u")
    else:
        items.append(("nf", final_g, _const_spec((1, d))))
    names = tuple(i[0] for i in items) + tuple(names_out)
    out = pl.pallas_call(
        functools.partial(_post_kernel, kind, tail, names),
        grid=(bsz, seq // tm),
        in_specs=[i[2] for i in items], out_specs=out_specs, out_shape=out_shape,
        compiler_params=_params(("arbitrary", "arbitrary")),
        name=f"post_{kind}{seq}",
    )(*[i[1] for i in items])
    return out


_S5_GB = 8
_S5_LANES = 2 * _S5_P


def _cmul(ar, ai, br, bi):
    return ar * br - ai * bi, ar * bi + ai * br


def _s5_mats_kernel(prm_ref, btr_ref, bti_ref, cr_ref, ci_ref, causal_ref, anti_ref,
                    m_ref, w_ref, vf_ref, vb_ref, a_ref):
    c, h, lanes = _S5_CHUNK, _S5_H, _S5_LANES
    a_re = prm_ref[0:1, :]
    a_im = prm_ref[1:2, :]
    dt = jnp.exp(prm_ref[2:3, :])
    sr = dt * a_re
    ph = dt * a_im
    mag = jnp.exp(sr)
    nr = mag * jnp.cos(ph) - 1.0
    ni = mag * jnp.sin(ph)
    den = a_re * a_re + a_im * a_im
    co_re = (nr * a_re + ni * a_im) / den
    co_im = (ni * a_re - nr * a_im) / den
    bt = _cmul(co_re, co_im, btr_ref[...], bti_ref[...])
    ct = (cr_ref[...], ci_ref[...])

    def powers(k):
        kf = k.astype(_F32)
        e = jnp.exp(kf * sr)
        return e * jnp.cos(kf * ph), e * jnp.sin(kf * ph)

    def per_step(x):
        return jnp.broadcast_to(x[:, None, :], (c, h, lanes)).reshape(c * h, lanes)

    def per_chan(x):
        return jnp.broadcast_to(x[None, :, :], (c, h, lanes)).reshape(c * h, lanes)

    step = lax.broadcasted_iota(jnp.int32, (c, lanes), 0)
    fwd = lax.broadcasted_iota(jnp.int32, (c, lanes), 1) < _S5_P
    tb = tuple(per_chan(x) for x in bt)
    tc = tuple(per_chan(x) for x in ct)

    def table(base, k_fwd, k_bwd):
        return _cmul(*base, *(per_step(x) for x in powers(jnp.where(fwd, k_fwd, k_bwd))))

    lmat = table(tb, -step, step)
    rmat = table(tc, step, -step)
    wmat = table(tb, c - 1 - step, step)
    vmat = table(tc, step + 1, c - step)

    def split(x):
        hi = x.astype(_BF16)
        return hi, (x - hi.astype(_F32)).astype(_BF16)

    def nt3(x, y):
        xh, xl = split(x)
        yh, yl = split(y)
        return _dot_nt(xh, yh) + _dot_nt(xl, yh) + _dot_nt(xh, yl)

    fwd_rows = lax.broadcasted_iota(jnp.int32, (c * h, lanes), 1) < _S5_P

    def kernel_half(keep):
        return (nt3(jnp.where(keep, lmat[0], 0.0), rmat[0])
                - nt3(jnp.where(keep, lmat[1], 0.0), rmat[1]))

    m = causal_ref[...] * kernel_half(fwd_rows) + anti_ref[...] * kernel_half(jnp.logical_not(fwd_rows))
    m_ref[...] = m.astype(m_ref.dtype)
    w_ref[...] = jnp.concatenate([wmat[0], wmat[1]], axis=1).astype(w_ref.dtype)
    vcat = jnp.concatenate([vmat[0], -vmat[1]], axis=1)
    keep_f = (lax.broadcasted_iota(jnp.int32, vcat.shape, 1) % lanes) < _S5_P
    vf_ref[...] = jnp.where(keep_f, vcat, 0.0).T.astype(vf_ref.dtype)
    vb_ref[...] = jnp.where(keep_f, 0.0, vcat).T.astype(vb_ref.dtype)
    chunk_k = jnp.full((1, lanes), c, jnp.int32)
    a_chunk = powers(chunk_k)
    a_ref[0:1, :] = a_chunk[0]
    a_ref[1:2, :] = a_chunk[1]


def _s5_mats(prm, bt_re, bt_im, c_re, c_im):
    g = prm.shape[0]
    cw, lanes = _CHUNK_W, _S5_LANES
    causal, anti = (jnp.asarray(a) for a in _s5_masks())
    gspec = pl.BlockSpec((None, _S5_H, lanes), lambda i: (i, 0, 0))
    mat_spec = pl.BlockSpec((None, cw, cw), lambda i: (i, 0, 0))
    mat_sds = jax.ShapeDtypeStruct((g, cw, cw), _BF16)
    return pl.pallas_call(
        _s5_mats_kernel,
        grid=(g,),
        in_specs=[pl.BlockSpec((None, 3, lanes), lambda i: (i, 0, 0)), gspec, gspec, gspec, gspec,
                  _const_spec((cw, cw)), _const_spec((cw, cw))],
        out_specs=[mat_spec] * 4 + [pl.BlockSpec((None, 2, lanes), lambda i: (i, 0, 0))],
        out_shape=[mat_sds] * 4 + [jax.ShapeDtypeStruct((g, 2, lanes), _F32)],
        compiler_params=_params(("arbitrary",)),
        name="s5_mats",
    )(prm, bt_re, bt_im, c_re, c_im, causal, anti)


def _s5_scan_kernel(nchunk, bsz, has_init, *refs):
    if has_init:
        u_ref, m_ref, w_ref, vf_ref, vb_ref, a_ref, h0_ref, y_ref = refs[:8]
    else:
        u_ref, m_ref, w_ref, vf_ref, vb_ref, a_ref, y_ref, fin_ref = refs[:8]
    s_re_ref, s_im_ref, pf_re_ref, pf_im_ref, pb_re_ref, pb_im_ref, yall_ref = refs[8:]
    c, lanes, gb = _S5_CHUNK, _S5_LANES, _S5_GB
    rows = nchunk * bsz
    seg = lax.broadcasted_iota(jnp.int32, (rows, 128), 1) // _S5_H
    fwd = lax.broadcasted_iota(jnp.int32, (bsz, lanes), 1) < _S5_P

    for gl in range(gb):
        tiles = []
        for half in range(2):
            acc = None
            for s8 in range(8):
                piece = u_ref[pl.ds(half * 8 + s8, rows, stride=c), :]
                shift = ((s8 - gl) % 8) * _S5_H
                piece = pltpu.roll(piece, shift, 1) if shift else piece
                acc = piece if acc is None else jnp.where(seg == s8, piece, acc)
            tiles.append(acc)
        u = jnp.concatenate(tiles, axis=1).astype(_BF16)
        yall_ref[gl] = jnp.dot(u, m_ref[gl], preferred_element_type=_F32)
        s = jnp.dot(u, w_ref[gl], preferred_element_type=_F32)
        s_re_ref[...] = s[:, 0:lanes]
        s_im_ref[...] = s[:, lanes:2 * lanes]

        a_re = a_ref[gl, 0:1, :]
        a_im = a_ref[gl, 1:2, :]
        if has_init:
            init = (h0_ref[gl, :, 0:lanes], h0_ref[gl, :, lanes:2 * lanes])
        else:
            init = (jnp.zeros((bsz, lanes), _F32), jnp.zeros((bsz, lanes), _F32))

        def step(k, carry):
            h_re, h_im = carry
            rowf = pl.ds(k, bsz, stride=nchunk)
            rowb = pl.ds(nchunk - 1 - k, bsz, stride=nchunk)
            pf_re_ref[rowf, :] = h_re
            pf_im_ref[rowf, :] = h_im
            pb_re_ref[rowb, :] = h_re
            pb_im_ref[rowb, :] = h_im
            s_re = jnp.where(fwd, s_re_ref[rowf, :], s_re_ref[rowb, :])
            s_im = jnp.where(fwd, s_im_ref[rowf, :], s_im_ref[rowb, :])
            return (a_re * h_re - a_im * h_im + s_re, a_re * h_im + a_im * h_re + s_im)

        fin = lax.fori_loop(0, nchunk, step, init)
        if not has_init:
            fin_ref[gl, :, 0:lanes] = fin[0]
            fin_ref[gl, :, lanes:2 * lanes] = fin[1]
        pf = jnp.concatenate([pf_re_ref[...], pf_im_ref[...]], axis=1).astype(_BF16)
        pb = jnp.concatenate([pb_re_ref[...], pb_im_ref[...]], axis=1).astype(_BF16)
        yall_ref[gl] = (yall_ref[gl]
                        + jnp.dot(pf, vf_ref[gl], preferred_element_type=_F32)
                        + jnp.dot(pb, vb_ref[gl], preferred_element_type=_F32))

    for t in range(c):
        half, t8 = divmod(t, 8)
        acc = None
        for gl in range(gb):
            piece = yall_ref[gl, :, half * 128:(half + 1) * 128]
            shift = ((gl - t8) % 8) * _S5_H
            piece = pltpu.roll(piece, shift, 1) if shift else piece
            acc = piece if acc is None else jnp.where(seg == gl, piece, acc)
        y_ref[pl.ds(t, rows, stride=c), :] = acc


def _s5_scan(u, mats, init):
    bsz, seq, d = u.shape
    g = d // _S5_H
    nchunk = seq // _S5_CHUNK
    rows = nchunk * bsz
    cw, gb = _CHUNK_W, _S5_GB
    lanes = gb * _S5_H
    tok_spec = pl.BlockSpec((bsz * seq, lanes), lambda i: (0, i))
    mat_spec = pl.BlockSpec((gb, cw, cw), lambda i: (i, 0, 0))
    st_spec = pl.BlockSpec((gb, bsz, cw), lambda i: (i, 0, 0))
    ins = [u.reshape(bsz * seq, d)] + list(mats)
    specs = [tok_spec] + [mat_spec] * 4 + [pl.BlockSpec((gb, 2, _S5_LANES), lambda i: (i, 0, 0))]
    y_sds = jax.ShapeDtypeStruct((bsz * seq, d), _F32)
    if init is not None:
        ins.append(init)
        specs.append(st_spec)
        out_specs, out_shape = tok_spec, y_sds
    else:
        out_specs = [tok_spec, st_spec]
        out_shape = [y_sds, jax.ShapeDtypeStruct((g, bsz, cw), _F32)]
    out = pl.pallas_call(
        functools.partial(_s5_scan_kernel, nchunk, bsz, init is not None),
        grid=(g // gb,),
        in_specs=specs, out_specs=out_specs, out_shape=out_shape,
        scratch_shapes=[pltpu.VMEM((rows, _S5_LANES), _F32)] * 6 + [pltpu.VMEM((gb, rows, cw), _F32)],
        compiler_params=_params(("arbitrary",)),
        name=f"s5_{seq}",
    )(*ins)
    if init is not None:
        return out.reshape(bsz, seq, d), None
    return out[0].reshape(bsz, seq, d), out[1]


def _trunk(x, pos, row_of_b, init_state, mod5, filt, s5_mats, wts):
    y0, z = _pre_hyena(x, pos, mod5, row_of_b, 0, wts["norm1_g"], wts["hy_in_w"], wts["hy_in_b"])
    v = _hyena_conv(z, wts["hy_conv_w"], wts["hy_conv_b"], wts["hy_fbias"], *filt)
    x1, u = _post_mixer("hyena", "next", y0, v, mod5, row_of_b, 0, wts["hy_out_w"], wts["hy_out_b"],
                        wts["norm1_g"], None, wts["norm2_g"], wts["ffn_w13"], wts["ffn_w2"], None)
    ys, fin = _s5_scan(u, s5_mats, init_state)
    (out,) = _post_mixer("s5", "final", x1, ys, mod5, row_of_b, 1, wts["s5_glu_w"], wts["s5_glu_b"],
                         wts["norm1_g"], wts["s5_D"], wts["norm2_g"], wts["ffn_w13"], wts["ffn_w2"],
                         wts["final_g"])
    return out, fin


def kernel(x_prompt, x_sample, state_s5, c, c_ctx, norm1_g, norm2_g, final_g, ada_w, ada_b, ffn_w13, ffn_w2, hy_in_w, hy_in_b, hy_conv_w, hy_conv_b, hy_pe_w1, hy_pe_b1, hy_pe_w2, hy_pe_b2, hy_pe_w3, hy_freq, hy_fbias, hy_out_w, hy_out_b, s5_A_re, s5_A_im, s5_log_dt, s5_B_re, s5_B_im, s5_C_re, s5_C_im, s5_D, s5_glu_w, s5_glu_b):
    depth, d = norm1_g.shape
    assert depth == 2 and hy_in_w.shape[0] == 1 and s5_glu_w.shape[0] == 1
    dec_b, dec_seq, _ = x_sample.shape
    g = d // _S5_H
    p = _S5_P

    nrow = -(-(1 + dec_b) // 8) * 8
    cond = jnp.concatenate([c_ctx[None], c, jnp.zeros((nrow - 1 - dec_b, d), _F32)], axis=0)
    mod5 = _modulation(cond, ada_w, ada_b)

    row3 = lambda a: a.reshape(a.shape[0], 1, a.shape[-1])
    wts = dict(
        norm1_g=row3(norm1_g), norm2_g=row3(norm2_g), final_g=final_g[None],
        hy_in_w=hy_in_w.astype(_BF16), hy_in_b=row3(hy_in_b),
        hy_conv_w=hy_conv_w[0], hy_conv_b=hy_conv_b, hy_fbias=hy_fbias[0],
        hy_out_w=hy_out_w.astype(_BF16), hy_out_b=row3(hy_out_b),
        ffn_w13=ffn_w13.astype(_BF16), ffn_w2=ffn_w2.astype(_BF16),
        s5_glu_w=s5_glu_w.astype(_BF16), s5_glu_b=row3(s5_glu_b), s5_D=row3(s5_D),
    )
    dirs_on_lanes = lambda a: a.transpose(1, 0, 2).reshape(g, 2 * p)
    ldt = jnp.broadcast_to(s5_log_dt[0][:, :, None], (2, g, p))
    prm = jnp.stack([dirs_on_lanes(s5_A_re[0]), dirs_on_lanes(s5_A_im[0]), dirs_on_lanes(ldt)], axis=1)
    bt = lambda a: a.transpose(1, 3, 0, 2).reshape(g, _S5_H, 2 * p)
    ct = lambda a: a.transpose(1, 2, 0, 3).reshape(g, _S5_H, 2 * p)
    s5_mats = _s5_mats(prm, bt(s5_B_re[0]), bt(s5_B_im[0]), ct(s5_C_re[0]), ct(s5_C_im[0]))

    w1p = jnp.pad(hy_pe_w1[0], ((0, _HY_FO - _HY_EMB), (0, 0)))
    filt_args = (w1p, hy_pe_b1, hy_pe_w2[0], hy_pe_b2, hy_freq, hy_pe_w3[0])
    filt_ctx = _hyena_filters(x_prompt.shape[1], d, *filt_args)
    filt_lat = _hyena_filters(dec_seq, d, *filt_args)

    y_prompt, fin = _trunk(x_prompt, None, lambda b: 0, None, mod5, filt_ctx, s5_mats, wts)
    new_state = fin.reshape(g, -1, 2, 2, p).transpose(1, 3, 2, 0, 4)[:, None]

    init = state_s5[:, 0].transpose(3, 0, 2, 1, 4).reshape(g, dec_b, 4 * p)
    pos = jnp.asarray(_pos_embed(dec_seq // _GRID_W, d))
    y_sample, _ = _trunk(x_sample, pos, lambda b: b + 1, init, mod5, filt_lat, s5_mats, wts)
    return (y_prompt, y_sample, new_state)
```

```python
import functools
import math

import numpy as np
import jax
import jax.numpy as jnp
from jax import lax
from jax.experimental import pallas as pl
from jax.experimental.pallas import tpu as pltpu

_F32 = jnp.float32
_BF16 = jnp.bfloat16

_EPS = 1e-6
_GRID_W = 64
_POS_BASE = 10000.0
_HY_ORDER = 2
_HY_EMB = 33
_HY_FO = 64
_HY_TARGET = 1e-2
_HY_FAST = 0.3
_HY_SLOW = 1.5
_S5_H = 16
_S5_P = 64
_S5_CHUNK = 16
_CHUNK_W = _S5_CHUNK * _S5_H

_VMEM_LIMIT = 56 * 1024 * 1024
_TOKEN_TILE = 256
_POST_TILE = 512
_CONV_DBLK = 512
_CONV_SUB = 512
_FILT_DBLK = 512


def _params(sem):
    return pltpu.CompilerParams(dimension_semantics=sem, vmem_limit_bytes=_VMEM_LIMIT)


def _dot_bf16(a, b):
    return jnp.dot(a.astype(_BF16), b.astype(_BF16), preferred_element_type=_F32)


def _dot_f32(a, b):
    return jnp.dot(a, b, precision=lax.Precision.HIGHEST, preferred_element_type=_F32)


def _dot_nt(a, b, precision=None):
    return lax.dot_general(a, b, (((1,), (1,)), ((), ())), precision=precision,
                           preferred_element_type=_F32)


def _rmsnorm(x, g):
    ms = jnp.mean(x * x, axis=-1, keepdims=True)
    return x * lax.rsqrt(ms + _EPS) * g


def _silu(x):
    return x * jax.nn.sigmoid(x)


def _gelu_tanh(x):
    c = math.sqrt(2.0 / math.pi)
    return 0.5 * x * (1.0 + jnp.tanh(c * (x + 0.044715 * (x * x * x))))


def _const_spec(shape):
    nd = len(shape)
    return pl.BlockSpec(shape, lambda *_: (0,) * nd, pipeline_mode=pl.Buffered(1))


def _layer_spec(shape, layer):
    nd = len(shape)
    return pl.BlockSpec((None,) + tuple(shape), lambda *_: (layer,) + (0,) * nd,
                        pipeline_mode=pl.Buffered(1))


def _mod_spec(d, layer, part, row_of_b):
    return pl.BlockSpec((None, None, None, 1, d), lambda b, t: (layer, part, row_of_b(b), 0, 0))


def _mod_kernel(cond_ref, w_ref, b_ref, o_ref):
    o_ref[...] = _dot_f32(_silu(cond_ref[...]), w_ref[...]) + b_ref[...]


def _modulation(cond, ada_w, ada_b):
    depth, d, d6 = ada_w.shape
    parts = d6 // d
    rows = cond.shape[0]
    out = pl.pallas_call(
        _mod_kernel,
        grid=(depth, parts),
        in_specs=[
            pl.BlockSpec((rows, d), lambda l, p: (0, 0)),
            pl.BlockSpec((None, d, d), lambda l, p: (l, 0, p)),
            pl.BlockSpec((None, None, 1, d), lambda l, p: (l, p, 0, 0)),
        ],
        out_specs=pl.BlockSpec((None, None, rows, d), lambda l, p: (l, p, 0, 0)),
        out_shape=jax.ShapeDtypeStruct((depth, parts, rows, d), _F32),
        compiler_params=_params(("arbitrary", "arbitrary")),
        name="mod",
    )(cond, ada_w, ada_b.reshape(depth, parts, 1, d))
    return out.reshape(depth, parts, rows, 1, d)


@functools.lru_cache(maxsize=None)
def _dft_consts(seq):
    f = np.arange(seq)[:, None]
    t = np.arange(seq)[None, :]
    ang = np.pi * ((f * t) % (2 * seq)) / seq
    cos = np.cos(ang)
    sin = np.sin(ang)
    alt = np.where(np.arange(seq) % 2 == 0, 1.0, -1.0)
    fwd_top = cos
    fwd_bot = -sin
    fwd_bot[0] = alt
    wgt = np.full((seq, 1), 2.0)
    wgt[0] = 1.0
    inv_top = (cos * wgt).T / (2 * seq)
    inv_bot = (-2.0 * sin).T / (2 * seq)
    inv_bot[:, 0] = alt / (2 * seq)
    fwd = np.concatenate([fwd_top, fwd_bot], axis=0).astype(np.float32)
    inv = np.concatenate([inv_top, inv_bot], axis=1).astype(np.float32)
    return fwd, inv, alt.astype(np.float32)[:, None]


@functools.lru_cache(maxsize=None)
def _filter_consts(seq, d):
    t = np.linspace(0.0, 1.0, seq)[:, None]
    w = 2.0 * np.pi * np.arange(seq)[:, None] / seq
    nb = (_HY_EMB - 1) // 2
    bands = np.linspace(1e-4, nb - 1, nb)[None, :]
    z = np.concatenate([t, np.cos(bands * w), -np.sin(bands * w)], axis=-1)
    zpad = np.zeros((seq, _HY_FO))
    zpad[:, :_HY_EMB] = z
    max_decay = math.log(_HY_TARGET) / _HY_FAST
    min_decay = math.log(_HY_TARGET) / _HY_SLOW
    deltas = np.abs(np.linspace(min_decay, max_decay, d))[None, :]
    return zpad.astype(np.float32), deltas.astype(np.float32)


@functools.lru_cache(maxsize=None)
def _pos_embed(rows, d):
    quarter = d // 4
    omega = 1.0 / (_POS_BASE ** (np.arange(quarter, dtype=np.float64) / quarter))

    def axis_embed(n):
        ang = np.arange(n, dtype=np.float64)[:, None] * omega[None]
        return np.concatenate([np.sin(ang), np.cos(ang)], axis=-1)

    er = np.broadcast_to(axis_embed(rows)[:, None], (rows, _GRID_W, d // 2))
    ec = np.broadcast_to(axis_embed(_GRID_W)[None], (rows, _GRID_W, d // 2))
    return np.concatenate([er, ec], axis=-1).reshape(rows * _GRID_W, d).astype(np.float32)


@functools.lru_cache(maxsize=None)
def _s5_masks():
    step = np.repeat(np.arange(_S5_CHUNK), _S5_H)
    causal = (step[None, :] >= step[:, None]).astype(np.float32)
    anti = (step[:, None] >= step[None, :]).astype(np.float32)
    return causal, anti


def _filter_kernel(z_ref, w1_ref, b1_ref, w2_ref, b2_ref, fr_ref, w3f_ref, w3b_ref, dl_ref,
                   at_ref, ab_ref, alt_ref, krt_ref, krb_ref, ki_ref):
    z = z_ref[...]
    fr = fr_ref[...]
    h = jnp.sin(fr * (_dot_f32(z, w1_ref[...]) + b1_ref[...]))
    h = jnp.sin(fr * (_dot_f32(h, w2_ref[...]) + b2_ref[...]))
    decay = jnp.exp(-z[:, 0:1] * dl_ref[...])
    hf = _dot_f32(h, w3f_ref[...]) * decay
    hb = _dot_f32(h, w3b_ref[...]) * decay
    norm = (jnp.sum(jnp.abs(hf), axis=0, keepdims=True)
            + jnp.sum(jnp.abs(hb), axis=0, keepdims=True) + _EPS)
    hf = hf / norm
    hb = hb / norm
    first = lax.broadcasted_iota(jnp.int32, hf.shape, 0) == 0
    hb = jnp.where(first, 0.0, hb)
    ksum = hf + hb
    kdiff = hf - hb
    kre = jnp.dot(at_ref[...], ksum.astype(_BF16), preferred_element_type=_F32)
    kim = jnp.dot(ab_ref[...], kdiff.astype(_BF16), preferred_element_type=_F32)
    nyq = jnp.sum(alt_ref[...] * ksum, axis=0, keepdims=True)
    krt_ref[...] = kre
    krb_ref[...] = jnp.where(first, nyq, kre)
    ki_ref[...] = jnp.where(first, 0.0, kim)


def _hyena_filters(seq, d, w1p, b1, w2, b2, freq, w3):
    zpad, deltas = _filter_consts(seq, d)
    fwd, _, alt = _dft_consts(seq)
    fwd = jnp.asarray(fwd).astype(_BF16)
    top = pl.BlockSpec((seq, seq), lambda o, j: (0, 0), pipeline_mode=pl.Buffered(1))
    bot = pl.BlockSpec((seq, seq), lambda o, j: (1, 0), pipeline_mode=pl.Buffered(1))
    nb = d // _FILT_DBLK
    fo = _HY_FO
    out_sds = jax.ShapeDtypeStruct((_HY_ORDER, seq, d), _F32)
    out_spec = pl.BlockSpec((None, seq, _FILT_DBLK), lambda o, j: (o, 0, j))
    return pl.pallas_call(
        _filter_kernel,
        grid=(_HY_ORDER, nb),
        in_specs=[
            _const_spec((seq, fo)), _const_spec((fo, fo)), _const_spec((1, fo)),
            _const_spec((fo, fo)), _const_spec((1, fo)), _const_spec((1, fo)),
            pl.BlockSpec((fo, _FILT_DBLK), lambda o, j: (0, (2 * o) * nb + j)),
            pl.BlockSpec((fo, _FILT_DBLK), lambda o, j: (0, (2 * o + 1) * nb + j)),
            pl.BlockSpec((1, _FILT_DBLK), lambda o, j: (0, j)),
            top, bot,
            _const_spec((seq, 1)),
        ],
        out_specs=[out_spec, out_spec, out_spec],
        out_shape=[out_sds, out_sds, out_sds],
        compiler_params=_params(("arbitrary", "arbitrary")),
        name=f"filt{seq}",
    )(jnp.asarray(zpad), w1p, b1, w2, b2, freq, w3, w3, jnp.asarray(deltas),
      fwd, fwd, jnp.asarray(alt))


def _pre_kernel(has_pos, *refs):
    if has_pos:
        x_ref, pos_ref, sh_ref, sc_ref, g_ref, w_ref, b_ref, x0_ref, z_ref = refs
        x = x_ref[...] + pos_ref[...]
        x0_ref[...] = x
    else:
        x_ref, sh_ref, sc_ref, g_ref, w_ref, b_ref, z_ref = refs
        x = x_ref[...]
    h = _rmsnorm(x, g_ref[...]) * (1.0 + sc_ref[...]) + sh_ref[...]
    z_ref[...] = _dot_bf16(h, w_ref[...]) + b_ref[...]


def _pre_hyena(x, pos, mod5, row_of_b, layer, norm_g, w_in, b_in):
    bsz, seq, d = x.shape
    n = w_in.shape[-1]
    tm = _TOKEN_TILE
    tok = pl.BlockSpec((None, tm, d), lambda b, t: (b, t, 0))
    ins, specs = [x], [tok]
    if pos is not None:
        ins.append(pos)
        specs.append(pl.BlockSpec((tm, d), lambda b, t: (t, 0)))
    ins += [mod5, mod5, norm_g, w_in, b_in]
    specs += [_mod_spec(d, layer, 0, row_of_b), _mod_spec(d, layer, 1, row_of_b),
              _layer_spec((1, d), layer), _layer_spec((d, n), 0), _layer_spec((1, n), 0)]
    z_spec = pl.BlockSpec((None, tm, n), lambda b, t: (b, t, 0))
    z_sds = jax.ShapeDtypeStruct((bsz, seq, n), _F32)
    if pos is not None:
        out_specs, out_shape = [tok, z_spec], [jax.ShapeDtypeStruct(x.shape, _F32), z_sds]
    else:
        out_specs, out_shape = z_spec, z_sds
    out = pl.pallas_call(
        functools.partial(_pre_kernel, pos is not None),
        grid=(bsz, seq // tm),
        in_specs=specs, out_specs=out_specs, out_shape=out_shape,
        compiler_params=_params(("arbitrary", "arbitrary")),
        name=f"pre{seq}",
    )(*ins)
    return out if pos is not None else (x, out)


def _conv_kernel(zv_ref, z1_ref, z2_ref, wv_ref, w1_ref, w2_ref, bv_ref, b1_ref, b2_ref, fb_ref,
                 krt_ref, krb_ref, ki_ref, ft_ref, fb2_ref, it_ref, ib_ref, o_ref):
    seq, dblk = zv_ref.shape
    sub = _CONV_SUB
    row = lax.broadcasted_iota(jnp.int32, (seq, sub), 0)
    first = row == 0
    last = row == seq - 1

    for c0 in range(0, dblk, sub):
        cols = slice(c0, c0 + sub)

        def short_conv(z_ref, w_ref, b_ref):
            z = z_ref[:, cols]
            prev = jnp.where(first, 0.0, pltpu.roll(z, 1, 0))
            nxt = jnp.where(last, 0.0, pltpu.roll(z, seq - 1, 0))
            w = w_ref[:, cols]
            return prev * w[0:1] + z * w[1:2] + nxt * w[2:3] + b_ref[:, cols]

        v = short_conv(zv_ref, wv_ref, bv_ref)
        gates = (short_conv(z1_ref, w1_ref, b1_ref), short_conv(z2_ref, w2_ref, b2_ref))
        for o in range(_HY_ORDER):
            vb = v.astype(_BF16)
            p = jnp.dot(ft_ref[...], vb, preferred_element_type=_F32)
            q = jnp.dot(fb2_ref[...], vb, preferred_element_type=_F32)
            ki = ki_ref[o, :, cols]
            ytop = p * krt_ref[o, :, cols] - q * ki
            ybot = p * ki + q * krb_ref[o, :, cols]
            y = (jnp.dot(it_ref[...], ytop.astype(_BF16), preferred_element_type=_F32)
                 + jnp.dot(ib_ref[...], ybot.astype(_BF16), preferred_element_type=_F32))
            v = (y + v * fb_ref[o:o + 1, cols]) * gates[o]
        o_ref[:, cols] = v.astype(o_ref.dtype)


def _hyena_conv(z, conv_w, conv_b, fbias, krt, krb, ki):
    bsz, seq, d3 = z.shape
    d = d3 // 3
    dblk = _CONV_DBLK
    nb = d // dblk
    fwd, inv, _ = _dft_consts(seq)
    fwd = jnp.asarray(fwd).astype(_BF16)
    inv = jnp.asarray(inv).astype(_BF16)

    def mspec(r, c):
        return pl.BlockSpec((seq, seq), lambda j, b: (r, c), pipeline_mode=pl.Buffered(1))

    def zspec(k):
        return pl.BlockSpec((None, seq, dblk), lambda j, b: (b, 0, k * nb + j))

    def wspec(rows, k):
        return pl.BlockSpec((rows, dblk), lambda j, b: (0, k * nb + j))

    kspec = pl.BlockSpec((_HY_ORDER, seq, dblk), lambda j, b: (0, 0, j), pipeline_mode=pl.Buffered(1))
    return pl.pallas_call(
        _conv_kernel,
        grid=(nb, bsz),
        in_specs=[zspec(0), zspec(1), zspec(2),
                  wspec(3, 0), wspec(3, 1), wspec(3, 2),
                  wspec(1, 0), wspec(1, 1), wspec(1, 2),
                  wspec(_HY_ORDER, 0),
                  kspec, kspec, kspec,
                  mspec(0, 0), mspec(1, 0), mspec(0, 0), mspec(0, 1)],
        out_specs=pl.BlockSpec((None, seq, dblk), lambda j, b: (b, 0, j)),
        out_shape=jax.ShapeDtypeStruct((bsz, seq, d), _BF16),
        compiler_params=_params(("arbitrary", "arbitrary")),
        name=f"conv{seq}",
    )(z, z, z, conv_w, conv_w, conv_w, conv_b, conv_b, conv_b, fbias, krt, krb, ki,
      fwd, fwd, inv, inv)


def _post_kernel(kind, tail, names, *refs):
    r = dict(zip(names, refs))
    x = r["x"][...]
    if kind == "hyena":
        m = _dot_bf16(r["mix"][...], r["w_out"][...]) + r["b_out"][...]
    else:
        u = _rmsnorm(x, r["n1"][...]) * (1.0 + r["sc1"][...]) + r["sh1"][...]
        y = _gelu_tanh(r["mix"][...] + u * r["skip"][...])
        ag = _dot_bf16(y, r["w_out"][...]) + r["b_out"][...]
        half = ag.shape[-1] // 2
        m = ag[:, :half] * jax.nn.sigmoid(ag[:, half:])
    x = x + r["g1"][...] * m
    h = _rmsnorm(x, r["n2"][...]) * (1.0 + r["sc2"][...]) + r["sh2"][...]
    ab = _dot_bf16(h, r["w13"][...])
    ff = ab.shape[-1] // 2
    act = _silu(ab[:, :ff]) * ab[:, ff:]
    x = x + r["g2"][...] * _dot_bf16(act, r["w2"][...])
    if tail == "next":
        r["o_x"][...] = x
        un = _rmsnorm(x, r["nn"][...]) * (1.0 + r["scn"][...]) + r["shn"][...]
        r["o_u"][...] = un.astype(r["o_u"].dtype)
    else:
        r["o_x"][...] = _rmsnorm(x, r["nf"][...])


def _post_mixer(kind, tail, x, mix, mod5, row_of_b, layer, w_out, b_out, norm1_g, skip,
                norm2_g, w13, w2, final_g):
    bsz, seq, d = x.shape
    tm = min(_POST_TILE, seq)
    tok = pl.BlockSpec((None, tm, d), lambda b, t: (b, t, 0))
    n_out = w_out.shape[-1]
    f2 = w13.shape[-1]

    def mod(part, lyr=layer):
        return (mod5, _mod_spec(d, lyr, part, row_of_b))

    items = [("x", x, tok), ("mix", mix, tok),
             ("w_out", w_out, _layer_spec((d, n_out), 0)), ("b_out", b_out, _layer_spec((1, n_out), 0)),
             ("g1",) + mod(2), ("sh2",) + mod(3), ("sc2",) + mod(4), ("g2",) + mod(5),
             ("n2", norm2_g, _layer_spec((1, d), layer)),
             ("w13", w13, _layer_spec((d, f2), layer)), ("w2", w2, _layer_spec((f2 // 2, d), layer))]
    if kind == "s5":
        items += [("n1", norm1_g, _layer_spec((1, d), layer)), ("sh1",) + mod(0), ("sc1",) + mod(1),
                  ("skip", skip, _layer_spec((1, d), 0))]
    out_specs, out_shape = [tok], [jax.ShapeDtypeStruct(x.shape, _F32)]
    names_out = ["o_x"]
    if tail == "next":
        items += [("nn", norm1_g, _layer_spec((1, d), layer + 1)),
                  ("shn",) + mod(0, layer + 1), ("scn",) + mod(1, layer + 1)]
        out_specs.append(tok)
        out_shape.append(jax.ShapeDtypeStruct(x.shape, _F32))
        names_out.append("o_u")
    else:
        items.append(("nf", final_g, _const_spec((1, d))))
    names = tuple(i[0] for i in items) + tuple(names_out)
    out = pl.pallas_call(
        functools.partial(_post_kernel, kind, tail, names),
        grid=(bsz, seq // tm),
        in_specs=[i[2] for i in items], out_specs=out_specs, out_shape=out_shape,
        compiler_params=_params(("arbitrary", "arbitrary")),
        name=f"post_{kind}{seq}",
    )(*[i[1] for i in items])
    return out


_S5_GB = 8
_S5_LANES = 2 * _S5_P


def _cmul(ar, ai, br, bi):
    return ar * br - ai * bi, ar * bi + ai * br


def _s5_mats_kernel(prm_ref, btr_ref, bti_ref, cr_ref, ci_ref, causal_ref, anti_ref,
                    m_ref, w_ref, vf_ref, vb_ref, a_ref):
    c, h, lanes = _S5_CHUNK, _S5_H, _S5_LANES
    a_re = prm_ref[0:1, :]
    a_im = prm_ref[1:2, :]
    dt = jnp.exp(prm_ref[2:3, :])
    sr = dt * a_re
    ph = dt * a_im
    mag = jnp.exp(sr)
    nr = mag * jnp.cos(ph) - 1.0
    ni = mag * jnp.sin(ph)
    den = a_re * a_re + a_im * a_im
    co_re = (nr * a_re + ni * a_im) / den
    co_im = (ni * a_re - nr * a_im) / den
    bt = _cmul(co_re, co_im, btr_ref[...], bti_ref[...])
    ct = (cr_ref[...], ci_ref[...])

    def powers(k):
        kf = k.astype(_F32)
        e = jnp.exp(kf * sr)
        return e * jnp.cos(kf * ph), e * jnp.sin(kf * ph)

    def per_step(x):
        return jnp.broadcast_to(x[:, None, :], (c, h, lanes)).reshape(c * h, lanes)

    def per_chan(x):
        return jnp.broadcast_to(x[None, :, :], (c, h, lanes)).reshape(c * h, lanes)

    step = lax.broadcasted_iota(jnp.int32, (c, lanes), 0)
    fwd = lax.broadcasted_iota(jnp.int32, (c, lanes), 1) < _S5_P
    tb = tuple(per_chan(x) for x in bt)
    tc = tuple(per_chan(x) for x in ct)

    def table(base, k_fwd, k_bwd):
        return _cmul(*base, *(per_step(x) for x in powers(jnp.where(fwd, k_fwd, k_bwd))))

    lmat = table(tb, -step, step)
    rmat = table(tc, step, -step)
    wmat = table(tb, c - 1 - step, step)
    vmat = table(tc, step + 1, c - step)

    def split(x):
        hi = x.astype(_BF16)
        return hi, (x - hi.astype(_F32)).astype(_BF16)

    def nt3(x, y):
        xh, xl = split(x)
        yh, yl = split(y)
        return _dot_nt(xh, yh) + _dot_nt(xl, yh) + _dot_nt(xh, yl)

    fwd_rows = lax.broadcasted_iota(jnp.int32, (c * h, lanes), 1) < _S5_P

    def kernel_half(keep):
        return (nt3(jnp.where(keep, lmat[0], 0.0), rmat[0])
                - nt3(jnp.where(keep, lmat[1], 0.0), rmat[1]))

    m = causal_ref[...] * kernel_half(fwd_rows) + anti_ref[...] * kernel_half(jnp.logical_not(fwd_rows))
    m_ref[...] = m.astype(m_ref.dtype)
    w_ref[...] = jnp.concatenate([wmat[0], wmat[1]], axis=1).astype(w_ref.dtype)
    vcat = jnp.concatenate([vmat[0], -vmat[1]], axis=1)
    keep_f = (lax.broadcasted_iota(jnp.int32, vcat.shape, 1) % lanes) < _S5_P
    vf_ref[...] = jnp.where(keep_f, vcat, 0.0).T.astype(vf_ref.dtype)
    vb_ref[...] = jnp.where(keep_f, 0.0, vcat).T.astype(vb_ref.dtype)
    chunk_k = jnp.full((1, lanes), c, jnp.int32)
    a_chunk = powers(chunk_k)
    a_ref[0:1, :] = a_chunk[0]
    a_ref[1:2, :] = a_chunk[1]


def _s5_mats(prm, bt_re, bt_im, c_re, c_im):
    g = prm.shape[0]
    cw, lanes = _CHUNK_W, _S5_LANES
    causal, anti = (jnp.asarray(a) for a in _s5_masks())
    gspec = pl.BlockSpec((None, _S5_H, lanes), lambda i: (i, 0, 0))
    mat_spec = pl.BlockSpec((None, cw, cw), lambda i: (i, 0, 0))
    mat_sds = jax.ShapeDtypeStruct((g, cw, cw), _BF16)
    return pl.pallas_call(
        _s5_mats_kernel,
        grid=(g,),
        in_specs=[pl.BlockSpec((None, 3, lanes), lambda i: (i, 0, 0)), gspec, gspec, gspec, gspec,
                  _const_spec((cw, cw)), _const_spec((cw, cw))],
        out_specs=[mat_spec] * 4 + [pl.BlockSpec((None, 2, lanes), lambda i: (i, 0, 0))],
        out_shape=[mat_sds] * 4 + [jax.ShapeDtypeStruct((g, 2, lanes), _F32)],
        compiler_params=_params(("arbitrary",)),
        name="s5_mats",
    )(prm, bt_re, bt_im, c_re, c_im, causal, anti)


def _s5_scan_kernel(nchunk, bsz, has_init, *refs):
    if has_init:
        u_ref, m_ref, w_ref, vf_ref, vb_ref, a_ref, h0_ref, y_ref = refs[:8]
    else:
        u_ref, m_ref, w_ref, vf_ref, vb_ref, a_ref, y_ref, fin_ref = refs[:8]
    s_re_ref, s_im_ref, pf_re_ref, pf_im_ref, pb_re_ref, pb_im_ref, yall_ref = refs[8:]
    c, lanes, gb = _S5_CHUNK, _S5_LANES, _S5_GB
    rows = nchunk * bsz
    seg = lax.broadcasted_iota(jnp.int32, (rows, 128), 1) // _S5_H
    fwd = lax.broadcasted_iota(jnp.int32, (bsz, lanes), 1) < _S5_P

    for gl in range(gb):
        tiles = []
        for half in range(2):
            acc = None
            for s8 in range(8):
                piece = u_ref[pl.ds(half * 8 + s8, rows, stride=c), :]
                shift = ((s8 - gl) % 8) * _S5_H
                piece = pltpu.roll(piece, shift, 1) if shift else piece
                acc = piece if acc is None else jnp.where(seg == s8, piece, acc)
            tiles.append(acc)
        u = jnp.concatenate(tiles, axis=1).astype(_BF16)
        yall_ref[gl] = jnp.dot(u, m_ref[gl], preferred_element_type=_F32)
        s = jnp.dot(u, w_ref[gl], preferred_element_type=_F32)
        s_re_ref[...] = s[:, 0:lanes]
        s_im_ref[...] = s[:, lanes:2 * lanes]

        a_re = a_ref[gl, 0:1, :]
        a_im = a_ref[gl, 1:2, :]
        if has_init:
            init = (h0_ref[gl, :, 0:lanes], h0_ref[gl, :, lanes:2 * lanes])
        else:
            init = (jnp.zeros((bsz, lanes), _F32), jnp.zeros((bsz, lanes), _F32))

        def step(k, carry):
            h_re, h_im = carry
            rowf = pl.ds(k, bsz, stride=nchunk)
            rowb = pl.ds(nchunk - 1 - k, bsz, stride=nchunk)
            pf_re_ref[rowf, :] = h_re
            pf_im_ref[rowf, :] = h_im
            pb_re_ref[rowb, :] = h_re
            pb_im_ref[rowb, :] = h_im
            s_re = jnp.where(fwd, s_re_ref[rowf, :], s_re_ref[rowb, :])
            s_im = jnp.where(fwd, s_im_ref[rowf, :], s_im_ref[rowb, :])
            return (a_re * h_re - a_im * h_im + s_re, a_re * h_im + a_im * h_re + s_im)

        fin = lax.fori_loop(0, nchunk, step, init)
        if not has_init:
            fin_ref[gl, :, 0:lanes] = fin[0]
            fin_ref[gl, :, lanes:2 * lanes] = fin[1]
        pf = jnp.concatenate([pf_re_ref[...], pf_im_ref[...]], axis=1).astype(_BF16)
        pb = jnp.concatenate([pb_re_ref[...], pb_im_ref[...]], axis=1).astype(_BF16)
        yall_ref[gl] = (yall_ref[gl]
                        + jnp.dot(pf, vf_ref[gl], preferred_element_type=_F32)
                        + jnp.dot(pb, vb_ref[gl], preferred_element_type=_F32))

    for t in range(c):
        half, t8 = divmod(t, 8)
        acc = None
        for gl in range(gb):
            piece = yall_ref[gl, :, half * 128:(half + 1) * 128]
            shift = ((gl - t8) % 8) * _S5_H
            piece = pltpu.roll(piece, shift, 1) if shift else piece
            acc = piece if acc is None else jnp.where(seg == gl, piece, acc)
        y_ref[pl.ds(t, rows, stride=c), :] = acc


def _s5_scan(u, mats, init):
    bsz, seq, d = u.shape
    g = d // _S5_H
    nchunk = seq // _S5_CHUNK
    rows = nchunk * bsz
    cw, gb = _CHUNK_W, _S5_GB
    lanes = gb * _S5_H
    tok_spec = pl.BlockSpec((bsz * seq, lanes), lambda i: (0, i))
    mat_spec = pl.BlockSpec((gb, cw, cw), lambda i: (i, 0, 0))
    st_spec = pl.BlockSpec((gb, bsz, cw), lambda i: (i, 0, 0))
    ins = [u.reshape(bsz * seq, d)] + list(mats)
    specs = [tok_spec] + [mat_spec] * 4 + [pl.BlockSpec((gb, 2, _S5_LANES), lambda i: (i, 0, 0))]
    y_sds = jax.ShapeDtypeStruct((bsz * seq, d), _F32)
    if init is not None:
        ins.append(init)
        specs.append(st_spec)
        out_specs, out_shape = tok_spec, y_sds
    else:
        out_specs = [tok_spec, st_spec]
        out_shape = [y_sds, jax.ShapeDtypeStruct((g, bsz, cw), _F32)]
    out = pl.pallas_call(
        functools.partial(_s5_scan_kernel, nchunk, bsz, init is not None),
        grid=(g // gb,),
        in_specs=specs, out_specs=out_specs, out_shape=out_shape,
        scratch_shapes=[pltpu.VMEM((rows, _S5_LANES), _F32)] * 6 + [pltpu.VMEM((gb, rows, cw), _F32)],
        compiler_params=_params(("arbitrary",)),
        name=f"s5_{seq}",
    )(*ins)
    if init is not None:
        return out.reshape(bsz, seq, d), None
    return out[0].reshape(bsz, seq, d), out[1]


def _trunk(x, pos, row_of_b, init_state, mod5, filt, s5_mats, wts):
    y0, z = _pre_hyena(x, pos, mod5, row_of_b, 0, wts["norm1_g"], wts["hy_in_w"], wts["hy_in_b"])
    v = _hyena_conv(z, wts["hy_conv_w"], wts["hy_conv_b"], wts["hy_fbias"], *filt)
    x1, u = _post_mixer("hyena", "next", y0, v, mod5, row_of_b, 0, wts["hy_out_w"], wts["hy_out_b"],
                        wts["norm1_g"], None, wts["norm2_g"], wts["ffn_w13"], wts["ffn_w2"], None)
    ys, fin = _s5_scan(u, s5_mats, init_state)
    (out,) = _post_mixer("s5", "final", x1, ys, mod5, row_of_b, 1, wts["s5_glu_w"], wts["s5_glu_b"],
                         wts["norm1_g"], wts["s5_D"], wts["norm2_g"], wts["ffn_w13"], wts["ffn_w2"],
                         wts["final_g"])
    return out, fin


def kernel(x_prompt, x_sample, state_s5, c, c_ctx, norm1_g, norm2_g, final_g, ada_w, ada_b, ffn_w13, ffn_w2, hy_in_w, hy_in_b, hy_conv_w, hy_conv_b, hy_pe_w1, hy_pe_b1, hy_pe_w2, hy_pe_b2, hy_pe_w3, hy_freq, hy_fbias, hy_out_w, hy_out_b, s5_A_re, s5_A_im, s5_log_dt, s5_B_re, s5_B_im, s5_C_re, s5_C_im, s5_D, s5_glu_w, s5_glu_b):
    depth, d = norm1_g.shape
    assert depth == 2 and hy_in_w.shape[0] == 1 and s5_glu_w.shape[0] == 1
    dec_b, dec_seq, _ = x_sample.shape
    g = d // _S5_H
    p = _S5_P

    nrow = -(-(1 + dec_b) // 8) * 8
    cond = jnp.concatenate([c_ctx[None], c, jnp.zeros((nrow - 1 - dec_b, d), _F32)], axis=0)
    mod5 = _modulation(cond, ada_w, ada_b)

    row3 = lambda a: a.reshape(a.shape[0], 1, a.shape[-1])
    wts = dict(
        norm1_g=row3(norm1_g), norm2_g=row3(norm2_g), final_g=final_g[None],
        hy_in_w=hy_in_w.astype(_BF16), hy_in_b=row3(hy_in_b),
        hy_conv_w=hy_conv_w[0], hy_conv_b=hy_conv_b, hy_fbias=hy_fbias[0],
        hy_out_w=hy_out_w.astype(_BF16), hy_out_b=row3(hy_out_b),
        ffn_w13=ffn_w13.astype(_BF16), ffn_w2=ffn_w2.astype(_BF16),
        s5_glu_w=s5_glu_w.astype(_BF16), s5_glu_b=row3(s5_glu_b), s5_D=row3(s5_D),
    )
    dirs_on_lanes = lambda a: a.transpose(1, 0, 2).reshape(g, 2 * p)
    ldt = jnp.broadcast_to(s5_log_dt[0][:, :, None], (2, g, p))
    prm = jnp.stack([dirs_on_lanes(s5_A_re[0]), dirs_on_lanes(s5_A_im[0]), dirs_on_lanes(ldt)], axis=1)
    bt = lambda a: a.transpose(1, 3, 0, 2).reshape(g, _S5_H, 2 * p)
    ct = lambda a: a.transpose(1, 2, 0, 3).reshape(g, _S5_H, 2 * p)
    s5_mats = _s5_mats(prm, bt(s5_B_re[0]), bt(s5_B_im[0]), ct(s5_C_re[0]), ct(s5_C_im[0]))

    w1p = jnp.pad(hy_pe_w1[0], ((0, _HY_FO - _HY_EMB), (0, 0)))
    filt_args = (w1p, hy_pe_b1, hy_pe_w2[0], hy_pe_b2, hy_freq, hy_pe_w3[0])
    filt_ctx = _hyena_filters(x_prompt.shape[1], d, *filt_args)
    filt_lat = _hyena_filters(dec_seq, d, *filt_args)

    y_prompt, fin = _trunk(x_prompt, None, lambda b: 0, None, mod5, filt_ctx, s5_mats, wts)
    new_state = fin.reshape(g, -1, 2, 2, p).transpose(1, 3, 2, 0, 4)[:, None]

    init = state_s5[:, 0].transpose(3, 0, 2, 1, 4).reshape(g, dec_b, 4 * p)
    pos = jnp.asarray(_pos_embed(dec_seq // _GRID_W, d))
    y_sample, _ = _trunk(x_sample, pos, lambda b: b + 1, init, mod5, filt_lat, s5_mats, wts)
    return (y_prompt, y_sample, new_state)
```

```python
import functools
import math

import numpy as np
import jax
import jax.numpy as jnp
from jax import lax
from jax.experimental import pallas as pl
from jax.experimental.pallas import tpu as pltpu

_F32 = jnp.float32
_BF16 = jnp.bfloat16

_EPS = 1e-6
_GRID_W = 64
_POS_BASE = 10000.0
_HY_ORDER = 2
_HY_EMB = 33
_HY_FO = 64
_HY_TARGET = 1e-2
_HY_FAST = 0.3
_HY_SLOW = 1.5
_S5_H = 16
_S5_P = 64
_S5_CHUNK = 16
_CHUNK_W = _S5_CHUNK * _S5_H

_VMEM_LIMIT = 56 * 1024 * 1024
_TOKEN_TILE = 256
_POST_TILE = 512
_CONV_DBLK = 512
_CONV_SUB = 512
_FILT_DBLK = 512


def _params(sem):
    return pltpu.CompilerParams(dimension_semantics=sem, vmem_limit_bytes=_VMEM_LIMIT)


def _dot_bf16(a, b):
    return jnp.dot(a.astype(_BF16), b.astype(_BF16), preferred_element_type=_F32)


def _dot_f32(a, b):
    return jnp.dot(a, b, precision=lax.Precision.HIGHEST, preferred_element_type=_F32)


def _dot_nt(a, b, precision=None):
    return lax.dot_general(a, b, (((1,), (1,)), ((), ())), precision=precision,
                           preferred_element_type=_F32)


def _rmsnorm(x, g):
    ms = jnp.mean(x * x, axis=-1, keepdims=True)
    return x * lax.rsqrt(ms + _EPS) * g


def _silu(x):
    return x * jax.nn.sigmoid(x)


def _gelu_tanh(x):
    c = math.sqrt(2.0 / math.pi)
    return 0.5 * x * (1.0 + jnp.tanh(c * (x + 0.044715 * (x * x * x))))


def _const_spec(shape):
    nd = len(shape)
    return pl.BlockSpec(shape, lambda *_: (0,) * nd, pipeline_mode=pl.Buffered(1))


def _layer_spec(shape, layer):
    nd = len(shape)
    return pl.BlockSpec((None,) + tuple(shape), lambda *_: (layer,) + (0,) * nd,
                        pipeline_mode=pl.Buffered(1))


def _mod_spec(d, layer, part, row_of_b):
    return pl.BlockSpec((None, None, None, 1, d), lambda b, t: (layer, part, row_of_b(b), 0, 0))


def _mod_kernel(cond_ref, w_ref, b_ref, o_ref):
    o_ref[...] = _dot_f32(_silu(cond_ref[...]), w_ref[...]) + b_ref[...]


def _modulation(cond, ada_w, ada_b):
    depth, d, d6 = ada_w.shape
    parts = d6 // d
    rows = cond.shape[0]
    out = pl.pallas_call(
        _mod_kernel,
        grid=(depth, parts),
        in_specs=[
            pl.BlockSpec((rows, d), lambda l, p: (0, 0)),
            pl.BlockSpec((None, d, d), lambda l, p: (l, 0, p)),
            pl.BlockSpec((None, None, 1, d), lambda l, p: (l, p, 0, 0)),
        ],
        out_specs=pl.BlockSpec((None, None, rows, d), lambda l, p: (l, p, 0, 0)),
        out_shape=jax.ShapeDtypeStruct((depth, parts, rows, d), _F32),
        compiler_params=_params(("arbitrary", "arbitrary")),
        name="mod",
    )(cond, ada_w, ada_b.reshape(depth, parts, 1, d))
    return out.reshape(depth, parts, rows, 1, d)


@functools.lru_cache(maxsize=None)
def _dft_consts(seq):
    f = np.arange(seq)[:, None]
    t = np.arange(seq)[None, :]
    ang = np.pi * ((f * t) % (2 * seq)) / seq
    cos = np.cos(ang)
    sin = np.sin(ang)
    alt = np.where(np.arange(seq) % 2 == 0, 1.0, -1.0)
    fwd_top = cos
    fwd_bot = -sin
    fwd_bot[0] = alt
    wgt = np.full((seq, 1), 2.0)
    wgt[0] = 1.0
    inv_top = (cos * wgt).T / (2 * seq)
    inv_bot = (-2.0 * sin).T / (2 * seq)
    inv_bot[:, 0] = alt / (2 * seq)
    fwd = np.concatenate([fwd_top, fwd_bot], axis=0).astype(np.float32)
    inv = np.concatenate([inv_top, inv_bot], axis=1).astype(np.float32)
    return fwd, inv, alt.astype(np.float32)[:, None]


@functools.lru_cache(maxsize=None)
def _filter_consts(seq, d):
    t = np.linspace(0.0, 1.0, seq)[:, None]
    w = 2.0 * np.pi * np.arange(seq)[:, None] / seq
    nb = (_HY_EMB - 1) // 2
    bands = np.linspace(1e-4, nb - 1, nb)[None, :]
    z = np.concatenate([t, np.cos(bands * w), -np.sin(bands * w)], axis=-1)
    zpad = np.zeros((seq, _HY_FO))
    zpad[:, :_HY_EMB] = z
    max_decay = math.log(_HY_TARGET) / _HY_FAST
    min_decay = math.log(_HY_TARGET) / _HY_SLOW
    deltas = np.abs(np.linspace(min_decay, max_decay, d))[None, :]
    return zpad.astype(np.float32), deltas.astype(np.float32)


@functools.lru_cache(maxsize=None)
def _pos_embed(rows, d):
    quarter = d // 4
    omega = 1.0 / (_POS_BASE ** (np.arange(quarter, dtype=np.float64) / quarter))

    def axis_embed(n):
        ang = np.arange(n, dtype=np.float64)[:, None] * omega[None]
        return np.concatenate([np.sin(ang), np.cos(ang)], axis=-1)

    er = np.broadcast_to(axis_embed(rows)[:, None], (rows, _GRID_W, d // 2))
    ec = np.broadcast_to(axis_embed(_GRID_W)[None], (rows, _GRID_W, d // 2))
    return np.concatenate([er, ec], axis=-1).reshape(rows * _GRID_W, d).astype(np.float32)


@functools.lru_cache(maxsize=None)
def _s5_masks():
    step = np.repeat(np.arange(_S5_CHUNK), _S5_H)
    causal = (step[None, :] >= step[:, None]).astype(np.float32)
    anti = (step[:, None] >= step[None, :]).astype(np.float32)
    return causal, anti


def _filter_kernel(z_ref, w1_ref, b1_ref, w2_ref, b2_ref, fr_ref, w3f_ref, w3b_ref, dl_ref,
                   at_ref, ab_ref, alt_ref, krt_ref, krb_ref, ki_ref):
    z = z_ref[...]
    fr = fr_ref[...]
    h = jnp.sin(fr * (_dot_f32(z, w1_ref[...]) + b1_ref[...]))
    h = jnp.sin(fr * (_dot_f32(h, w2_ref[...]) + b2_ref[...]))
    decay = jnp.exp(-z[:, 0:1] * dl_ref[...])
    hf = _dot_f32(h, w3f_ref[...]) * decay
    hb = _dot_f32(h, w3b_ref[...]) * decay
    norm = (jnp.sum(jnp.abs(hf), axis=0, keepdims=True)
            + jnp.sum(jnp.abs(hb), axis=0, keepdims=True) + _EPS)
    hf = hf / norm
    hb = hb / norm
    first = lax.broadcasted_iota(jnp.int32, hf.shape, 0) == 0
    hb = jnp.where(first, 0.0, hb)
    ksum = hf + hb
    kdiff = hf - hb
    kre = jnp.dot(at_ref[...], ksum.astype(_BF16), preferred_element_type=_F32)
    kim = jnp.dot(ab_ref[...], kdiff.astype(_BF16), preferred_element_type=_F32)
    nyq = jnp.sum(alt_ref[...] * ksum, axis=0, keepdims=True)
    krt_ref[...] = kre
    krb_ref[...] = jnp.where(first, nyq, kre)
    ki_ref[...] = jnp.where(first, 0.0, kim)


def _hyena_filters(seq, d, w1p, b1, w2, b2, freq, w3):
    zpad, deltas = _filter_consts(seq, d)
    fwd, _, alt = _dft_consts(seq)
    fwd = jnp.asarray(fwd).astype(_BF16)
    top = pl.BlockSpec((seq, seq), lambda o, j: (0, 0), pipeline_mode=pl.Buffered(1))
    bot = pl.BlockSpec((seq, seq), lambda o, j: (1, 0), pipeline_mode=pl.Buffered(1))
    nb = d // _FILT_DBLK
    fo = _HY_FO
    out_sds = jax.ShapeDtypeStruct((_HY_ORDER, seq, d), _F32)
    out_spec = pl.BlockSpec((None, seq, _FILT_DBLK), lambda o, j: (o, 0, j))
    return pl.pallas_call(
        _filter_kernel,
        grid=(_HY_ORDER, nb),
        in_specs=[
            _const_spec((seq, fo)), _const_spec((fo, fo)), _const_spec((1, fo)),
            _const_spec((fo, fo)), _const_spec((1, fo)), _const_spec((1, fo)),
            pl.BlockSpec((fo, _FILT_DBLK), lambda o, j: (0, (2 * o) * nb + j)),
            pl.BlockSpec((fo, _FILT_DBLK), lambda o, j: (0, (2 * o + 1) * nb + j)),
            pl.BlockSpec((1, _FILT_DBLK), lambda o, j: (0, j)),
            top, bot,
            _const_spec((seq, 1)),
        ],
        out_specs=[out_spec, out_spec, out_spec],
        out_shape=[out_sds, out_sds, out_sds],
        compiler_params=_params(("arbitrary", "arbitrary")),
        name=f"filt{seq}",
    )(jnp.asarray(zpad), w1p, b1, w2, b2, freq, w3, w3, jnp.asarray(deltas),
      fwd, fwd, jnp.asarray(alt))


def _pre_kernel(has_pos, *refs):
    if has_pos:
        x_ref, pos_ref, sh_ref, sc_ref, g_ref, w_ref, b_ref, x0_ref, z_ref = refs
        x = x_ref[...] + pos_ref[...]
        x0_ref[...] = x
    else:
        x_ref, sh_ref, sc_ref, g_ref, w_ref, b_ref, z_ref = refs
        x = x_ref[...]
    h = _rmsnorm(x, g_ref[...]) * (1.0 + sc_ref[...]) + sh_ref[...]
    z_ref[...] = _dot_bf16(h, w_ref[...]) + b_ref[...]


def _pre_hyena(x, pos, mod5, row_of_b, layer, norm_g, w_in, b_in):
    bsz, seq, d = x.shape
    n = w_in.shape[-1]
    tm = _TOKEN_TILE
    tok = pl.BlockSpec((None, tm, d), lambda b, t: (b, t, 0))
    ins, specs = [x], [tok]
    if pos is not None:
        ins.append(pos)
        specs.append(pl.BlockSpec((tm, d), lambda b, t: (t, 0)))
    ins += [mod5, mod5, norm_g, w_in, b_in]
    specs += [_mod_spec(d, layer, 0, row_of_b), _mod_spec(d, layer, 1, row_of_b),
              _layer_spec((1, d), layer), _layer_spec((d, n), 0), _layer_spec((1, n), 0)]
    z_spec = pl.BlockSpec((None, tm, n), lambda b, t: (b, t, 0))
    z_sds = jax.ShapeDtypeStruct((bsz, seq, n), _F32)
    if pos is not None:
        out_specs, out_shape = [tok, z_spec], [jax.ShapeDtypeStruct(x.shape, _F32), z_sds]
    else:
        out_specs, out_shape = z_spec, z_sds
    out = pl.pallas_call(
        functools.partial(_pre_kernel, pos is not None),
        grid=(bsz, seq // tm),
        in_specs=specs, out_specs=out_specs, out_shape=out_shape,
        compiler_params=_params(("arbitrary", "arbitrary")),
        name=f"pre{seq}",
    )(*ins)
    return out if pos is not None else (x, out)


def _conv_kernel(zv_ref, z1_ref, z2_ref, wv_ref, w1_ref, w2_ref, bv_ref, b1_ref, b2_ref, fb_ref,
                 krt_ref, krb_ref, ki_ref, ft_ref, fb2_ref, it_ref, ib_ref, o_ref):
    seq, dblk = zv_ref.shape
    sub = _CONV_SUB
    row = lax.broadcasted_iota(jnp.int32, (seq, sub), 0)
    first = row == 0
    last = row == seq - 1

    for c0 in range(0, dblk, sub):
        cols = slice(c0, c0 + sub)

        def short_conv(z_ref, w_ref, b_ref):
            z = z_ref[:, cols]
            prev = jnp.where(first, 0.0, pltpu.roll(z, 1, 0))
            nxt = jnp.where(last, 0.0, pltpu.roll(z, seq - 1, 0))
            w = w_ref[:, cols]
            return prev * w[0:1] + z * w[1:2] + nxt * w[2:3] + b_ref[:, cols]

        v = short_conv(zv_ref, wv_ref, bv_ref)
        gates = (short_conv(z1_ref, w1_ref, b1_ref), short_conv(z2_ref, w2_ref, b2_ref))
        for o in range(_HY_ORDER):
            vb = v.astype(_BF16)
            p = jnp.dot(ft_ref[...], vb, preferred_element_type=_F32)
            q = jnp.dot(fb2_ref[...], vb, preferred_element_type=_F32)
            ki = ki_ref[o, :, cols]
            ytop = p * krt_ref[o, :, cols] - q * ki
            ybot = p * ki + q * krb_ref[o, :, cols]
            y = (jnp.dot(it_ref[...], ytop.astype(_BF16), preferred_element_type=_F32)
                 + jnp.dot(ib_ref[...], ybot.astype(_BF16), preferred_element_type=_F32))
            v = (y + v * fb_ref[o:o + 1, cols]) * gates[o]
        o_ref[:, cols] = v.astype(o_ref.dtype)


def _hyena_conv(z, conv_w, conv_b, fbias, krt, krb, ki):
    bsz, seq, d3 = z.shape
    d = d3 // 3
    dblk = _CONV_DBLK
    nb = d // dblk
    fwd, inv, _ = _dft_consts(seq)
    fwd = jnp.asarray(fwd).astype(_BF16)
    inv = jnp.asarray(inv).astype(_BF16)

    def mspec(r, c):
        return pl.BlockSpec((seq, seq), lambda j, b: (r, c), pipeline_mode=pl.Buffered(1))

    def zspec(k):
        return pl.BlockSpec((None, seq, dblk), lambda j, b: (b, 0, k * nb + j))

    def wspec(rows, k):
        return pl.BlockSpec((rows, dblk), lambda j, b: (0, k * nb + j))

    kspec = pl.BlockSpec((_HY_ORDER, seq, dblk), lambda j, b: (0, 0, j), pipeline_mode=pl.Buffered(1))
    return pl.pallas_call(
        _conv_kernel,
        grid=(nb, bsz),
        in_specs=[zspec(0), zspec(1), zspec(2),
                  wspec(3, 0), wspec(3, 1), wspec(3, 2),
                  wspec(1, 0), wspec(1, 1), wspec(1, 2),
                  wspec(_HY_ORDER, 0),
                  kspec, kspec, kspec,
                  mspec(0, 0), mspec(1, 0), mspec(0, 0), mspec(0, 1)],
        out_specs=pl.BlockSpec((None, seq, dblk), lambda j, b: (b, 0, j)),
        out_shape=jax.ShapeDtypeStruct((bsz, seq, d), _BF16),
        compiler_params=_params(("arbitrary", "arbitrary")),
        name=f"conv{seq}",
    )(z, z, z, conv_w, conv_w, conv_w, conv_b, conv_b, conv_b, fbias, krt, krb, ki,
      fwd, fwd, inv, inv)


def _post_kernel(kind, tail, names, *refs):
    r = dict(zip(names, refs))
    x = r["x"][...]
    if kind == "hyena":
        m = _dot_bf16(r["mix"][...], r["w_out"][...]) + r["b_out"][...]
    else:
        u = _rmsnorm(x, r["n1"][...]) * (1.0 + r["sc1"][...]) + r["sh1"][...]
        mix = pltpu.einshape("scd->csd", r["mix"][...]).reshape(x.shape)
        y = _gelu_tanh(mix + u * r["skip"][...])
        ag = _dot_bf16(y, r["w_out"][...]) + r["b_out"][...]
        half = ag.shape[-1] // 2
        m = ag[:, :half] * jax.nn.sigmoid(ag[:, half:])
    x = x + r["g1"][...] * m
    h = _rmsnorm(x, r["n2"][...]) * (1.0 + r["sc2"][...]) + r["sh2"][...]
    ab = _dot_bf16(h, r["w13"][...])
    ff = ab.shape[-1] // 2
    act = _silu(ab[:, :ff]) * ab[:, ff:]
    x = x + r["g2"][...] * _dot_bf16(act, r["w2"][...])
    if tail == "next":
        r["o_x"][...] = x
        un = _rmsnorm(x, r["nn"][...]) * (1.0 + r["scn"][...]) + r["shn"][...]
        steps = r["o_u"].shape[0]
        r["o_u"][...] = pltpu.einshape("csd->scd", un.reshape(-1, steps, un.shape[-1]))
    else:
        r["o_x"][...] = _rmsnorm(x, r["nf"][...])


def _post_mixer(kind, tail, x, mix, mod5, row_of_b, layer, w_out, b_out, norm1_g, skip,
                norm2_g, w13, w2, final_g):
    bsz, seq, d = x.shape
    tm = min(_POST_TILE, seq)
    tok = pl.BlockSpec((None, tm, d), lambda b, t: (b, t, 0))
    nt = seq // tm
    steps = _S5_CHUNK
    stepmajor = pl.BlockSpec((steps, tm // steps, d), lambda b, t: (0, b * nt + t, 0))
    stepmajor_sds = jax.ShapeDtypeStruct((steps, bsz * seq // steps, d), _F32)
    n_out = w_out.shape[-1]
    f2 = w13.shape[-1]

    def mod(part, lyr=layer):
        return (mod5, _mod_spec(d, lyr, part, row_of_b))

    items = [("x", x, tok), ("mix", mix, stepmajor if kind == "s5" else tok),
             ("w_out", w_out, _layer_spec((d, n_out), 0)), ("b_out", b_out, _layer_spec((1, n_out), 0)),
             ("g1",) + mod(2), ("sh2",) + mod(3), ("sc2",) + mod(4), ("g2",) + mod(5),
             ("n2", norm2_g, _layer_spec((1, d), layer)),
             ("w13", w13, _layer_spec((d, f2), layer)), ("w2", w2, _layer_spec((f2 // 2, d), layer))]
    if kind == "s5":
        items += [("n1", norm1_g, _layer_spec((1, d), layer)), ("sh1",) + mod(0), ("sc1",) + mod(1),
                  ("skip", skip, _layer_spec((1, d), 0))]
    out_specs, out_shape = [tok], [jax.ShapeDtypeStruct(x.shape, _F32)]
    names_out = ["o_x"]
    if tail == "next":
        items += [("nn", norm1_g, _layer_spec((1, d), layer + 1)),
                  ("shn",) + mod(0, layer + 1), ("scn",) + mod(1, layer + 1)]
        out_specs.append(stepmajor)
        out_shape.append(stepmajor_sds)
        names_out.append("o_u")
    else:
        items.append(("nf", final_g, _const_spec((1, d))))
    names = tuple(i[0] for i in items) + tuple(names_out)
    out = pl.pallas_call(
        functools.partial(_post_kernel, kind, tail, names),
        grid=(bsz, seq // tm),
        in_specs=[i[2] for i in items], out_specs=out_specs, out_shape=out_shape,
        compiler_params=_params(("arbitrary", "arbitrary")),
        name=f"post_{kind}{seq}",
    )(*[i[1] for i in items])
    return out


_S5_GB = 8
_S5_LANES = 2 * _S5_P


def _cmul(ar, ai, br, bi):
    return ar * br - ai * bi, ar * bi + ai * br


def _s5_mats_kernel(prm_ref, btr_ref, bti_ref, cr_ref, ci_ref, causal_ref, anti_ref,
                    m_ref, w_ref, vf_ref, vb_ref, a_ref):
    c, h, lanes = _S5_CHUNK, _S5_H, _S5_LANES
    a_re = prm_ref[0:1, :]
    a_im = prm_ref[1:2, :]
    dt = jnp.exp(prm_ref[2:3, :])
    sr = dt * a_re
    ph = dt * a_im
    mag = jnp.exp(sr)
    nr = mag * jnp.cos(ph) - 1.0
    ni = mag * jnp.sin(ph)
    den = a_re * a_re + a_im * a_im
    co_re = (nr * a_re + ni * a_im) / den
    co_im = (ni * a_re - nr * a_im) / den
    bt = _cmul(co_re, co_im, btr_ref[...], bti_ref[...])
    ct = (cr_ref[...], ci_ref[...])

    def powers(k):
        kf = k.astype(_F32)
        e = jnp.exp(kf * sr)
        return e * jnp.cos(kf * ph), e * jnp.sin(kf * ph)

    def per_step(x):
        return jnp.broadcast_to(x[:, None, :], (c, h, lanes)).reshape(c * h, lanes)

    def per_chan(x):
        return jnp.broadcast_to(x[None, :, :], (c, h, lanes)).reshape(c * h, lanes)

    step = lax.broadcasted_iota(jnp.int32, (c, lanes), 0)
    fwd = lax.broadcasted_iota(jnp.int32, (c, lanes), 1) < _S5_P
    tb = tuple(per_chan(x) for x in bt)
    tc = tuple(per_chan(x) for x in ct)

    def table(base, k_fwd, k_bwd):
        return _cmul(*base, *(per_step(x) for x in powers(jnp.where(fwd, k_fwd, k_bwd))))

    lmat = table(tb, -step, step)
    rmat = table(tc, step, -step)
    wmat = table(tb, c - 1 - step, step)
    vmat = table(tc, step + 1, c - step)

    def split(x):
        hi = x.astype(_BF16)
        return hi, (x - hi.astype(_F32)).astype(_BF16)

    def nt3(x, y):
        xh, xl = split(x)
        yh, yl = split(y)
        return _dot_nt(xh, yh) + _dot_nt(xl, yh) + _dot_nt(xh, yl)

    fwd_rows = lax.broadcasted_iota(jnp.int32, (c * h, lanes), 1) < _S5_P

    def kernel_half(keep):
        return (nt3(jnp.where(keep, lmat[0], 0.0), rmat[0])
                - nt3(jnp.where(keep, lmat[1], 0.0), rmat[1]))

    m = causal_ref[...] * kernel_half(fwd_rows) + anti_ref[...] * kernel_half(jnp.logical_not(fwd_rows))
    m_ref[...] = m.astype(m_ref.dtype)
    w_ref[...] = jnp.concatenate([wmat[0], wmat[1]], axis=1).astype(w_ref.dtype)
    vcat = jnp.concatenate([vmat[0], -vmat[1]], axis=1)
    keep_f = (lax.broadcasted_iota(jnp.int32, vcat.shape, 1) % lanes) < _S5_P
    vf_ref[...] = jnp.where(keep_f, vcat, 0.0).T.astype(vf_ref.dtype)
    vb_ref[...] = jnp.where(keep_f, 0.0, vcat).T.astype(vb_ref.dtype)
    chunk_k = jnp.full((1, lanes), c, jnp.int32)
    a_chunk = powers(chunk_k)
    a_ref[0:1, :] = a_chunk[0]
    a_ref[1:2, :] = a_chunk[1]


def _s5_mats(prm, bt_re, bt_im, c_re, c_im):
    g = prm.shape[0]
    cw, lanes = _CHUNK_W, _S5_LANES
    causal, anti = (jnp.asarray(a) for a in _s5_masks())
    gspec = pl.BlockSpec((None, _S5_H, lanes), lambda i: (i, 0, 0))
    mat_spec = pl.BlockSpec((None, cw, cw), lambda i: (i, 0, 0))
    mat_sds = jax.ShapeDtypeStruct((g, cw, cw), _BF16)
    return pl.pallas_call(
        _s5_mats_kernel,
        grid=(g,),
        in_specs=[pl.BlockSpec((None, 3, lanes), lambda i: (i, 0, 0)), gspec, gspec, gspec, gspec,
                  _const_spec((cw, cw)), _const_spec((cw, cw))],
        out_specs=[mat_spec] * 4 + [pl.BlockSpec((None, 2, lanes), lambda i: (i, 0, 0))],
        out_shape=[mat_sds] * 4 + [jax.ShapeDtypeStruct((g, 2, lanes), _F32)],
        compiler_params=_params(("arbitrary",)),
        name="s5_mats",
    )(prm, bt_re, bt_im, c_re, c_im, causal, anti)


def _segment_transpose(xs):
    n = len(xs)
    seg = lax.broadcasted_iota(jnp.int32, xs[0].shape, 1) // _S5_H
    diags = []
    for k in range(n):
        z = xs[-k % n]
        for m in range(1, n):
            z = jnp.where(seg == m, xs[(m - k) % n], z)
        diags.append(pltpu.roll(z, ((n - k) % n) * _S5_H, 1) if k else z)
    ys = []
    for b in range(n):
        y = diags[b]
        for a in range(1, n):
            y = jnp.where(seg == a, diags[(b - a) % n], y)
        ys.append(y)
    return ys


def _s5_scan_kernel(nchunk, bsz, has_init, *refs):
    if has_init:
        u_ref, m_ref, w_ref, vf_ref, vb_ref, a_ref, h0_ref, y_ref = refs[:8]
    else:
        u_ref, m_ref, w_ref, vf_ref, vb_ref, a_ref, y_ref, fin_ref = refs[:8]
    s_ref, pf_ref, pb_ref, uall_ref, yall_ref = refs[8:]
    lanes, gb = _S5_LANES, _S5_GB
    rows = nchunk * bsz
    fwd = lax.broadcasted_iota(jnp.int32, (bsz, lanes), 1) < _S5_P

    for half in range(2):
        per_group = _segment_transpose([u_ref[half * 8 + s8] for s8 in range(8)])
        for gl in range(gb):
            uall_ref[gl, :, half * 128:(half + 1) * 128] = per_group[gl].astype(_BF16)

    for gl in range(gb):
        u = uall_ref[gl]
        yall_ref[gl] = jnp.dot(u, m_ref[gl], preferred_element_type=_F32)
        s = jnp.dot(u, w_ref[gl], preferred_element_type=_F32)
        s_ref[...] = pltpu.einshape("bcl->cbl", s.reshape(bsz, nchunk, 2 * lanes)).reshape(rows, 2 * lanes)

        a_re = a_ref[gl, 0:1, :]
        a_im = a_ref[gl, 1:2, :]
        if has_init:
            init = (h0_ref[gl, :, 0:lanes], h0_ref[gl, :, lanes:2 * lanes])
        else:
            init = (jnp.zeros((bsz, lanes), _F32), jnp.zeros((bsz, lanes), _F32))

        def step(k, carry):
            h_re, h_im = carry
            rowf = pl.ds(pl.multiple_of(k * bsz, bsz), bsz)
            rowb = pl.ds(pl.multiple_of((nchunk - 1 - k) * bsz, bsz), bsz)
            pf_ref[rowf, 0:lanes] = h_re
            pf_ref[rowf, lanes:2 * lanes] = h_im
            pb_ref[rowb, 0:lanes] = h_re
            pb_ref[rowb, lanes:2 * lanes] = h_im
            s_re = jnp.where(fwd, s_ref[rowf, 0:lanes], s_ref[rowb, 0:lanes])
            s_im = jnp.where(fwd, s_ref[rowf, lanes:2 * lanes], s_ref[rowb, lanes:2 * lanes])
            return (a_re * h_re - a_im * h_im + s_re, a_re * h_im + a_im * h_re + s_im)

        fin = lax.fori_loop(0, nchunk, step, init)
        if not has_init:
            fin_ref[gl, :, 0:lanes] = fin[0]
            fin_ref[gl, :, lanes:2 * lanes] = fin[1]
        carried = (jnp.dot(pf_ref[...].astype(_BF16), vf_ref[gl], preferred_element_type=_F32)
                   + jnp.dot(pb_ref[...].astype(_BF16), vb_ref[gl], preferred_element_type=_F32))
        carried = pltpu.einshape("cbl->bcl", carried.reshape(nchunk, bsz, 2 * lanes)).reshape(rows, 2 * lanes)
        yall_ref[gl] = yall_ref[gl] + carried

    for half in range(2):
        per_step = _segment_transpose([yall_ref[gl, :, half * 128:(half + 1) * 128] for gl in range(gb)])
        for t8 in range(8):
            y_ref[half * 8 + t8] = per_step[t8]


def _s5_scan(u, bsz, mats, init):
    steps, rows, d = u.shape
    g = d // _S5_H
    nchunk = rows // bsz
    cw, gb = _CHUNK_W, _S5_GB
    lanes = gb * _S5_H
    tok_spec = pl.BlockSpec((steps, rows, lanes), lambda i: (0, 0, i))
    mat_spec = pl.BlockSpec((gb, cw, cw), lambda i: (i, 0, 0))
    st_spec = pl.BlockSpec((gb, bsz, cw), lambda i: (i, 0, 0))
    ins = [u] + list(mats)
    specs = [tok_spec] + [mat_spec] * 4 + [pl.BlockSpec((gb, 2, _S5_LANES), lambda i: (i, 0, 0))]
    y_sds = jax.ShapeDtypeStruct(u.shape, _F32)
    if init is not None:
        ins.append(init)
        specs.append(st_spec)
        out_specs, out_shape = tok_spec, y_sds
    else:
        out_specs = [tok_spec, st_spec]
        out_shape = [y_sds, jax.ShapeDtypeStruct((g, bsz, cw), _F32)]
    out = pl.pallas_call(
        functools.partial(_s5_scan_kernel, nchunk, bsz, init is not None),
        grid=(g // gb,),
        in_specs=specs, out_specs=out_specs, out_shape=out_shape,
        scratch_shapes=[pltpu.VMEM((rows, cw), _F32)] * 3
        + [pltpu.VMEM((gb, rows, cw), _BF16), pltpu.VMEM((gb, rows, cw), _F32)],
        compiler_params=_params(("arbitrary",)),
        name=f"s5_{nchunk * steps}",
    )(*ins)
    if init is not None:
        return out, None
    return out[0], out[1]


def _trunk(x, pos, row_of_b, init_state, mod5, filt, s5_mats, wts):
    y0, z = _pre_hyena(x, pos, mod5, row_of_b, 0, wts["norm1_g"], wts["hy_in_w"], wts["hy_in_b"])
    v = _hyena_conv(z, wts["hy_conv_w"], wts["hy_conv_b"], wts["hy_fbias"], *filt)
    x1, u = _post_mixer("hyena", "next", y0, v, mod5, row_of_b, 0, wts["hy_out_w"], wts["hy_out_b"],
                        wts["norm1_g"], None, wts["norm2_g"], wts["ffn_w13"], wts["ffn_w2"], None)
    ys, fin = _s5_scan(u, x.shape[0], s5_mats, init_state)
    (out,) = _post_mixer("s5", "final", x1, ys, mod5, row_of_b, 1, wts["s5_glu_w"], wts["s5_glu_b"],
                         wts["norm1_g"], wts["s5_D"], wts["norm2_g"], wts["ffn_w13"], wts["ffn_w2"],
                         wts["final_g"])
    return out, fin


def kernel(x_prompt, x_sample, state_s5, c, c_ctx, norm1_g, norm2_g, final_g, ada_w, ada_b, ffn_w13, ffn_w2, hy_in_w, hy_in_b, hy_conv_w, hy_conv_b, hy_pe_w1, hy_pe_b1, hy_pe_w2, hy_pe_b2, hy_pe_w3, hy_freq, hy_fbias, hy_out_w, hy_out_b, s5_A_re, s5_A_im, s5_log_dt, s5_B_re, s5_B_im, s5_C_re, s5_C_im, s5_D, s5_glu_w, s5_glu_b):
    depth, d = norm1_g.shape
    assert depth == 2 and hy_in_w.shape[0] == 1 and s5_glu_w.shape[0] == 1
    dec_b, dec_seq, _ = x_sample.shape
    g = d // _S5_H
    p = _S5_P

    nrow = -(-(1 + dec_b) // 8) * 8
    cond = jnp.concatenate([c_ctx[None], c, jnp.zeros((nrow - 1 - dec_b, d), _F32)], axis=0)
    mod5 = _modulation(cond, ada_w, ada_b)

    row3 = lambda a: a.reshape(a.shape[0], 1, a.shape[-1])
    wts = dict(
        norm1_g=row3(norm1_g), norm2_g=row3(norm2_g), final_g=final_g[None],
        hy_in_w=hy_in_w.astype(_BF16), hy_in_b=row3(hy_in_b),
        hy_conv_w=hy_conv_w[0], hy_conv_b=hy_conv_b, hy_fbias=hy_fbias[0],
        hy_out_w=hy_out_w.astype(_BF16), hy_out_b=row3(hy_out_b),
        ffn_w13=ffn_w13.astype(_BF16), ffn_w2=ffn_w2.astype(_BF16),
        s5_glu_w=s5_glu_w.astype(_BF16), s5_glu_b=row3(s5_glu_b), s5_D=row3(s5_D),
    )
    dirs_on_lanes = lambda a: a.transpose(1, 0, 2).reshape(g, 2 * p)
    ldt = jnp.broadcast_to(s5_log_dt[0][:, :, None], (2, g, p))
    prm = jnp.stack([dirs_on_lanes(s5_A_re[0]), dirs_on_lanes(s5_A_im[0]), dirs_on_lanes(ldt)], axis=1)
    bt = lambda a: a.transpose(1, 3, 0, 2).reshape(g, _S5_H, 2 * p)
    ct = lambda a: a.transpose(1, 2, 0, 3).reshape(g, _S5_H, 2 * p)
    s5_mats = _s5_mats(prm, bt(s5_B_re[0]), bt(s5_B_im[0]), ct(s5_C_re[0]), ct(s5_C_im[0]))

    w1p = jnp.pad(hy_pe_w1[0], ((0, _HY_FO - _HY_EMB), (0, 0)))
    filt_args = (w1p, hy_pe_b1, hy_pe_w2[0], hy_pe_b2, hy_freq, hy_pe_w3[0])
    filt_ctx = _hyena_filters(x_prompt.shape[1], d, *filt_args)
    filt_lat = _hyena_filters(dec_seq, d, *filt_args)

    y_prompt, fin = _trunk(x_prompt, None, lambda b: 0, None, mod5, filt_ctx, s5_mats, wts)
    new_state = fin.reshape(g, -1, 2, 2, p).transpose(1, 3, 2, 0, 4)[:, None]

    init = state_s5[:, 0].transpose(3, 0, 2, 1, 4).reshape(g, dec_b, 4 * p)
    pos = jnp.asarray(_pos_embed(dec_seq // _GRID_W, d))
    y_sample, _ = _trunk(x_sample, pos, lambda b: b + 1, init, mod5, filt_lat, s5_mats, wts)
    return (y_prompt, y_sample, new_state)
```

```python
import functools
import math

import numpy as np
import jax
import jax.numpy as jnp
from jax import lax
from jax.experimental import pallas as pl
from jax.experimental.pallas import tpu as pltpu

_F32 = jnp.float32
_BF16 = jnp.bfloat16

_EPS = 1e-6
_GRID_W = 64
_POS_BASE = 10000.0
_HY_ORDER = 2
_HY_EMB = 33
_HY_FO = 64
_HY_TARGET = 1e-2
_HY_FAST = 0.3
_HY_SLOW = 1.5
_S5_H = 16
_S5_P = 64
_S5_CHUNK = 16
_CHUNK_W = _S5_CHUNK * _S5_H

_VMEM_LIMIT = 56 * 1024 * 1024
_TOKEN_TILE = 256
_POST_TILE = 512
_CONV_DBLK = 512
_CONV_SUB = 512
_FILT_DBLK = 512


def _params(sem):
    return pltpu.CompilerParams(dimension_semantics=sem, vmem_limit_bytes=_VMEM_LIMIT)


def _dot_bf16(a, b):
    return jnp.dot(a.astype(_BF16), b.astype(_BF16), preferred_element_type=_F32)


def _split_bf16(x):
    hi = x.astype(_BF16)
    return hi, (x - hi.astype(_F32)).astype(_BF16)


def _dot_3pass(a, b):
    ah, al = _split_bf16(a)
    bh, bl = _split_bf16(b)
    return (jnp.dot(ah, bh, preferred_element_type=_F32) + jnp.dot(al, bh, preferred_element_type=_F32)
            + jnp.dot(ah, bl, preferred_element_type=_F32))


def _dot_nt(a, b, precision=None):
    return lax.dot_general(a, b, (((1,), (1,)), ((), ())), precision=precision,
                           preferred_element_type=_F32)


def _rmsnorm(x, g):
    ms = jnp.mean(x * x, axis=-1, keepdims=True)
    return x * lax.rsqrt(ms + _EPS) * g


def _silu(x):
    return x * jax.nn.sigmoid(x)


def _gelu_tanh(x):
    c = math.sqrt(2.0 / math.pi)
    return 0.5 * x * (1.0 + jnp.tanh(c * (x + 0.044715 * (x * x * x))))


def _const_spec(shape):
    nd = len(shape)
    return pl.BlockSpec(shape, lambda *_: (0,) * nd, pipeline_mode=pl.Buffered(1))


def _layer_spec(shape, layer):
    nd = len(shape)
    return pl.BlockSpec((None,) + tuple(shape), lambda *_: (layer,) + (0,) * nd,
                        pipeline_mode=pl.Buffered(1))


def _mod_spec(d, layer, part, row_of_b):
    return pl.BlockSpec((None, None, None, 1, d), lambda b, t: (layer, part, row_of_b(b), 0, 0))


def _mod_kernel(cond_ref, w_ref, b_ref, o_ref):
    o_ref[...] = _dot_3pass(_silu(cond_ref[...]), w_ref[...]) + b_ref[...]


def _modulation(cond, ada_w, ada_b):
    depth, d, d6 = ada_w.shape
    parts = d6 // d
    rows = cond.shape[0]
    out = pl.pallas_call(
        _mod_kernel,
        grid=(depth, parts),
        in_specs=[
            pl.BlockSpec((rows, d), lambda l, p: (0, 0)),
            pl.BlockSpec((None, d, d), lambda l, p: (l, 0, p)),
            pl.BlockSpec((None, None, 1, d), lambda l, p: (l, p, 0, 0)),
        ],
        out_specs=pl.BlockSpec((None, None, rows, d), lambda l, p: (l, p, 0, 0)),
        out_shape=jax.ShapeDtypeStruct((depth, parts, rows, d), _F32),
        compiler_params=_params(("arbitrary", "arbitrary")),
        name="mod",
    )(cond, ada_w, ada_b.reshape(depth, parts, 1, d))
    return out.reshape(depth, parts, rows, 1, d)


@functools.lru_cache(maxsize=None)
def _dft_consts(seq):
    f = np.arange(seq)[:, None]
    t = np.arange(seq)[None, :]
    ang = np.pi * ((f * t) % (2 * seq)) / seq
    cos = np.cos(ang)
    sin = np.sin(ang)
    alt = np.where(np.arange(seq) % 2 == 0, 1.0, -1.0)
    fwd_top = cos
    fwd_bot = -sin
    fwd_bot[0] = alt
    wgt = np.full((seq, 1), 2.0)
    wgt[0] = 1.0
    inv_top = (cos * wgt).T / (2 * seq)
    inv_bot = (-2.0 * sin).T / (2 * seq)
    inv_bot[:, 0] = alt / (2 * seq)
    fwd = np.concatenate([fwd_top, fwd_bot], axis=0).astype(np.float32)
    inv = np.concatenate([inv_top, inv_bot], axis=1).astype(np.float32)
    return fwd, inv, alt.astype(np.float32)[:, None]


@functools.lru_cache(maxsize=None)
def _filter_consts(seq, d):
    t = np.linspace(0.0, 1.0, seq)[:, None]
    w = 2.0 * np.pi * np.arange(seq)[:, None] / seq
    nb = (_HY_EMB - 1) // 2
    bands = np.linspace(1e-4, nb - 1, nb)[None, :]
    z = np.concatenate([t, np.cos(bands * w), -np.sin(bands * w)], axis=-1)
    zpad = np.zeros((seq, _HY_FO))
    zpad[:, :_HY_EMB] = z
    max_decay = math.log(_HY_TARGET) / _HY_FAST
    min_decay = math.log(_HY_TARGET) / _HY_SLOW
    deltas = np.abs(np.linspace(min_decay, max_decay, d))[None, :]
    return zpad.astype(np.float32), deltas.astype(np.float32)


@functools.lru_cache(maxsize=None)
def _pos_embed(rows, d):
    quarter = d // 4
    omega = 1.0 / (_POS_BASE ** (np.arange(quarter, dtype=np.float64) / quarter))

    def axis_embed(n):
        ang = np.arange(n, dtype=np.float64)[:, None] * omega[None]
        return np.concatenate([np.sin(ang), np.cos(ang)], axis=-1)

    er = np.broadcast_to(axis_embed(rows)[:, None], (rows, _GRID_W, d // 2))
    ec = np.broadcast_to(axis_embed(_GRID_W)[None], (rows, _GRID_W, d // 2))
    return np.concatenate([er, ec], axis=-1).reshape(rows * _GRID_W, d).astype(np.float32)


@functools.lru_cache(maxsize=None)
def _s5_masks():
    step = np.repeat(np.arange(_S5_CHUNK), _S5_H)
    causal = (step[None, :] >= step[:, None]).astype(np.float32)
    anti = (step[:, None] >= step[None, :]).astype(np.float32)
    return causal, anti


def _filter_kernel(z_ref, w1_ref, b1_ref, w2_ref, b2_ref, fr_ref, w3f_ref, w3b_ref, dl_ref,
                   at_ref, ab_ref, alt_ref, krt_ref, krb_ref, ki_ref):
    z = z_ref[...]
    fr = fr_ref[...]
    h = jnp.sin(fr * (_dot_3pass(z, w1_ref[...]) + b1_ref[...]))
    h = jnp.sin(fr * (_dot_3pass(h, w2_ref[...]) + b2_ref[...]))
    decay = jnp.exp(-z[:, 0:1] * dl_ref[...])
    hf = _dot_3pass(h, w3f_ref[...]) * decay
    hb = _dot_3pass(h, w3b_ref[...]) * decay
    norm = (jnp.sum(jnp.abs(hf), axis=0, keepdims=True)
            + jnp.sum(jnp.abs(hb), axis=0, keepdims=True) + _EPS)
    hf = hf / norm
    hb = hb / norm
    first = lax.broadcasted_iota(jnp.int32, hf.shape, 0) == 0
    hb = jnp.where(first, 0.0, hb)
    ksum = hf + hb
    kdiff = hf - hb
    kre = jnp.dot(at_ref[...], ksum.astype(_BF16), preferred_element_type=_F32)
    kim = jnp.dot(ab_ref[...], kdiff.astype(_BF16), preferred_element_type=_F32)
    nyq = jnp.sum(alt_ref[...] * ksum, axis=0, keepdims=True)
    krt_ref[...] = kre
    krb_ref[...] = jnp.where(first, nyq, kre)
    ki_ref[...] = jnp.where(first, 0.0, kim)


def _hyena_filters(seq, d, w1p, b1, w2, b2, freq, w3):
    zpad, deltas = _filter_consts(seq, d)
    fwd, _, alt = _dft_consts(seq)
    fwd = jnp.asarray(fwd).astype(_BF16)
    top = pl.BlockSpec((seq, seq), lambda o, j: (0, 0), pipeline_mode=pl.Buffered(1))
    bot = pl.BlockSpec((seq, seq), lambda o, j: (1, 0), pipeline_mode=pl.Buffered(1))
    nb = d // _FILT_DBLK
    fo = _HY_FO
    out_sds = jax.ShapeDtypeStruct((_HY_ORDER, seq, d), _F32)
    out_spec = pl.BlockSpec((None, seq, _FILT_DBLK), lambda o, j: (o, 0, j))
    return pl.pallas_call(
        _filter_kernel,
        grid=(_HY_ORDER, nb),
        in_specs=[
            _const_spec((seq, fo)), _const_spec((fo, fo)), _const_spec((1, fo)),
            _const_spec((fo, fo)), _const_spec((1, fo)), _const_spec((1, fo)),
            pl.BlockSpec((fo, _FILT_DBLK), lambda o, j: (0, (2 * o) * nb + j)),
            pl.BlockSpec((fo, _FILT_DBLK), lambda o, j: (0, (2 * o + 1) * nb + j)),
            pl.BlockSpec((1, _FILT_DBLK), lambda o, j: (0, j)),
            top, bot,
            _const_spec((seq, 1)),
        ],
        out_specs=[out_spec, out_spec, out_spec],
        out_shape=[out_sds, out_sds, out_sds],
        compiler_params=_params(("arbitrary", "arbitrary")),
        name=f"filt{seq}",
    )(jnp.asarray(zpad), w1p, b1, w2, b2, freq, w3, w3, jnp.asarray(deltas),
      fwd, fwd, jnp.asarray(alt))


def _pre_kernel(has_pos, *refs):
    if has_pos:
        x_ref, pos_ref, sh_ref, sc_ref, g_ref, w_ref, b_ref, x0_ref, z_ref = refs
        x = x_ref[...] + pos_ref[...]
        x0_ref[...] = x
    else:
        x_ref, sh_ref, sc_ref, g_ref, w_ref, b_ref, z_ref = refs
        x = x_ref[...]
    h = _rmsnorm(x, g_ref[...]) * (1.0 + sc_ref[...]) + sh_ref[...]
    z_ref[...] = (_dot_bf16(h, w_ref[...]) + b_ref[...]).astype(z_ref.dtype)


def _pre_hyena(x, pos, mod5, row_of_b, layer, norm_g, w_in, b_in):
    bsz, seq, d = x.shape
    n = w_in.shape[-1]
    tm = _TOKEN_TILE
    tok = pl.BlockSpec((None, tm, d), lambda b, t: (b, t, 0))
    ins, specs = [x], [tok]
    if pos is not None:
        ins.append(pos)
        specs.append(pl.BlockSpec((tm, d), lambda b, t: (t, 0)))
    ins += [mod5, mod5, norm_g, w_in, b_in]
    specs += [_mod_spec(d, layer, 0, row_of_b), _mod_spec(d, layer, 1, row_of_b),
              _layer_spec((1, d), layer), _layer_spec((d, n), 0), _layer_spec((1, n), 0)]
    z_spec = pl.BlockSpec((None, tm, n), lambda b, t: (b, t, 0))
    z_sds = jax.ShapeDtypeStruct((bsz, seq, n), _BF16)
    if pos is not None:
        out_specs, out_shape = [tok, z_spec], [jax.ShapeDtypeStruct(x.shape, _F32), z_sds]
    else:
        out_specs, out_shape = z_spec, z_sds
    out = pl.pallas_call(
        functools.partial(_pre_kernel, pos is not None),
        grid=(bsz, seq // tm),
        in_specs=specs, out_specs=out_specs, out_shape=out_shape,
        compiler_params=_params(("arbitrary", "arbitrary")),
        name=f"pre{seq}",
    )(*ins)
    return out if pos is not None else (x, out)


def _conv_kernel(zv_ref, z1_ref, z2_ref, wv_ref, w1_ref, w2_ref, bv_ref, b1_ref, b2_ref, fb_ref,
                 krt_ref, krb_ref, ki_ref, ft_ref, fb2_ref, it_ref, ib_ref, o_ref):
    seq, dblk = zv_ref.shape
    sub = _CONV_SUB
    row = lax.broadcasted_iota(jnp.int32, (seq, sub), 0)
    first = row == 0
    last = row == seq - 1

    for c0 in range(0, dblk, sub):
        cols = slice(c0, c0 + sub)

        def short_conv(z_ref, w_ref, b_ref):
            z = z_ref[:, cols].astype(_F32)
            prev = jnp.where(first, 0.0, pltpu.roll(z, 1, 0))
            nxt = jnp.where(last, 0.0, pltpu.roll(z, seq - 1, 0))
            w = w_ref[:, cols]
            return prev * w[0:1] + z * w[1:2] + nxt * w[2:3] + b_ref[:, cols]

        v = short_conv(zv_ref, wv_ref, bv_ref)
        gates = (short_conv(z1_ref, w1_ref, b1_ref), short_conv(z2_ref, w2_ref, b2_ref))
        for o in range(_HY_ORDER):
            vb = v.astype(_BF16)
            p = jnp.dot(ft_ref[...], vb, preferred_element_type=_F32)
            q = jnp.dot(fb2_ref[...], vb, preferred_element_type=_F32)
            ki = ki_ref[o, :, cols]
            ytop = p * krt_ref[o, :, cols] - q * ki
            ybot = p * ki + q * krb_ref[o, :, cols]
            y = (jnp.dot(it_ref[...], ytop.astype(_BF16), preferred_element_type=_F32)
                 + jnp.dot(ib_ref[...], ybot.astype(_BF16), preferred_element_type=_F32))
            v = (y + v * fb_ref[o:o + 1, cols]) * gates[o]
        o_ref[:, cols] = v.astype(o_ref.dtype)


def _hyena_conv(z, conv_w, conv_b, fbias, krt, krb, ki):
    bsz, seq, d3 = z.shape
    d = d3 // 3
    dblk = _CONV_DBLK
    nb = d // dblk
    fwd, inv, _ = _dft_consts(seq)
    fwd = jnp.asarray(fwd).astype(_BF16)
    inv = jnp.asarray(inv).astype(_BF16)

    def mspec(r, c):
        return pl.BlockSpec((seq, seq), lambda j, b: (r, c), pipeline_mode=pl.Buffered(1))

    def zspec(k):
        return pl.BlockSpec((None, seq, dblk), lambda j, b: (b, 0, k * nb + j))

    def wspec(rows, k):
        return pl.BlockSpec((rows, dblk), lambda j, b: (0, k * nb + j))

    kspec = pl.BlockSpec((_HY_ORDER, seq, dblk), lambda j, b: (0, 0, j), pipeline_mode=pl.Buffered(1))
    return pl.pallas_call(
        _conv_kernel,
        grid=(nb, bsz),
        in_specs=[zspec(0), zspec(1), zspec(2),
                  wspec(3, 0), wspec(3, 1), wspec(3, 2),
                  wspec(1, 0), wspec(1, 1), wspec(1, 2),
                  wspec(_HY_ORDER, 0),
                  kspec, kspec, kspec,
                  mspec(0, 0), mspec(1, 0), mspec(0, 0), mspec(0, 1)],
        out_specs=pl.BlockSpec((None, seq, dblk), lambda j, b: (b, 0, j)),
        out_shape=jax.ShapeDtypeStruct((bsz, seq, d), _BF16),
        compiler_params=_params(("arbitrary", "arbitrary")),
        name=f"conv{seq}",
    )(z, z, z, conv_w, conv_w, conv_w, conv_b, conv_b, conv_b, fbias, krt, krb, ki,
      fwd, fwd, inv, inv)


def _post_kernel(kind, tail, names, *refs):
    r = dict(zip(names, refs))
    x = r["x"][...]
    if kind == "hyena":
        m = _dot_bf16(r["mix"][...], r["w_out"][...]) + r["b_out"][...]
    else:
        u = _rmsnorm(x, r["n1"][...]) * (1.0 + r["sc1"][...]) + r["sh1"][...]
        mix = jnp.swapaxes(r["mix"][...], 0, 1).reshape(x.shape)
        y = _gelu_tanh(mix + u * r["skip"][...])
        ag = _dot_bf16(y, r["w_out"][...]) + r["b_out"][...]
        half = ag.shape[-1] // 2
        m = ag[:, :half] * jax.nn.sigmoid(ag[:, half:])
    x = x + r["g1"][...] * m
    h = _rmsnorm(x, r["n2"][...]) * (1.0 + r["sc2"][...]) + r["sh2"][...]
    ab = _dot_bf16(h, r["w13"][...])
    ff = ab.shape[-1] // 2
    act = _silu(ab[:, :ff]) * ab[:, ff:]
    x = x + r["g2"][...] * _dot_bf16(act, r["w2"][...])
    if tail == "next":
        r["o_x"][...] = x
        un = _rmsnorm(x, r["nn"][...]) * (1.0 + r["scn"][...]) + r["shn"][...]
        steps = r["o_u"].shape[0]
        r["o_u"][...] = jnp.swapaxes(un.reshape(-1, steps, un.shape[-1]), 0, 1)
    else:
        r["o_x"][...] = _rmsnorm(x, r["nf"][...])


def _post_mixer(kind, tail, x, mix, mod5, row_of_b, layer, w_out, b_out, norm1_g, skip,
                norm2_g, w13, w2, final_g):
    bsz, seq, d = x.shape
    tm = min(_POST_TILE, seq)
    tok = pl.BlockSpec((None, tm, d), lambda b, t: (b, t, 0))
    nt = seq // tm
    steps = _S5_CHUNK
    stepmajor = pl.BlockSpec((steps, tm // steps, d), lambda b, t: (0, b * nt + t, 0))
    stepmajor_sds = jax.ShapeDtypeStruct((steps, bsz * seq // steps, d), _F32)
    n_out = w_out.shape[-1]
    f2 = w13.shape[-1]

    def mod(part, lyr=layer):
        return (mod5, _mod_spec(d, lyr, part, row_of_b))

    items = [("x", x, tok), ("mix", mix, stepmajor if kind == "s5" else tok),
             ("w_out", w_out, _layer_spec((d, n_out), 0)), ("b_out", b_out, _layer_spec((1, n_out), 0)),
             ("g1",) + mod(2), ("sh2",) + mod(3), ("sc2",) + mod(4), ("g2",) + mod(5),
             ("n2", norm2_g, _layer_spec((1, d), layer)),
             ("w13", w13, _layer_spec((d, f2), layer)), ("w2", w2, _layer_spec((f2 // 2, d), layer))]
    if kind == "s5":
        items += [("n1", norm1_g, _layer_spec((1, d), layer)), ("sh1",) + mod(0), ("sc1",) + mod(1),
                  ("skip", skip, _layer_spec((1, d), 0))]
    out_specs, out_shape = [tok], [jax.ShapeDtypeStruct(x.shape, _F32)]
    names_out = ["o_x"]
    if tail == "next":
        items += [("nn", norm1_g, _layer_spec((1, d), layer + 1)),
                  ("shn",) + mod(0, layer + 1), ("scn",) + mod(1, layer + 1)]
        out_specs.append(stepmajor)
        out_shape.append(stepmajor_sds)
        names_out.append("o_u")
    else:
        items.append(("nf", final_g, _const_spec((1, d))))
    names = tuple(i[0] for i in items) + tuple(names_out)
    out = pl.pallas_call(
        functools.partial(_post_kernel, kind, tail, names),
        grid=(bsz, seq // tm),
        in_specs=[i[2] for i in items], out_specs=out_specs, out_shape=out_shape,
        compiler_params=_params(("arbitrary", "arbitrary")),
        name=f"post_{kind}{seq}",
    )(*[i[1] for i in items])
    return out


_S5_GB = 8
_S5_LANES = 2 * _S5_P


def _cmul(ar, ai, br, bi):
    return ar * br - ai * bi, ar * bi + ai * br


def _s5_mats_kernel(prm_ref, btr_ref, bti_ref, cr_ref, ci_ref, causal_ref, anti_ref,
                    m_ref, w_ref, vf_ref, vb_ref, a_ref):
    c, h, lanes = _S5_CHUNK, _S5_H, _S5_LANES
    a_re = prm_ref[0:1, :]
    a_im = prm_ref[1:2, :]
    dt = jnp.exp(prm_ref[2:3, :])
    sr = dt * a_re
    ph = dt * a_im
    mag = jnp.exp(sr)
    nr = mag * jnp.cos(ph) - 1.0
    ni = mag * jnp.sin(ph)
    den = a_re * a_re + a_im * a_im
    co_re = (nr * a_re + ni * a_im) / den
    co_im = (ni * a_re - nr * a_im) / den
    bt = _cmul(co_re, co_im, btr_ref[...], bti_ref[...])
    ct = (cr_ref[...], ci_ref[...])

    def powers(k):
        kf = k.astype(_F32)
        e = jnp.exp(kf * sr)
        return e * jnp.cos(kf * ph), e * jnp.sin(kf * ph)

    def per_step(x):
        return jnp.broadcast_to(x[:, None, :], (c, h, lanes)).reshape(c * h, lanes)

    def per_chan(x):
        return jnp.broadcast_to(x[None, :, :], (c, h, lanes)).reshape(c * h, lanes)

    step = lax.broadcasted_iota(jnp.int32, (c, lanes), 0)
    fwd = lax.broadcasted_iota(jnp.int32, (c, lanes), 1) < _S5_P
    tb = tuple(per_chan(x) for x in bt)
    tc = tuple(per_chan(x) for x in ct)

    def table(base, k_fwd, k_bwd):
        return _cmul(*base, *(per_step(x) for x in powers(jnp.where(fwd, k_fwd, k_bwd))))

    lmat = table(tb, -step, step)
    rmat = table(tc, step, -step)
    wmat = table(tb, c - 1 - step, step)
    vmat = table(tc, step + 1, c - step)

    def nt3(x, y):
        xh, xl = _split_bf16(x)
        yh, yl = _split_bf16(y)
        return _dot_nt(xh, yh) + _dot_nt(xl, yh) + _dot_nt(xh, yl)

    fwd_rows = lax.broadcasted_iota(jnp.int32, (c * h, lanes), 1) < _S5_P

    def kernel_half(keep):
        return (nt3(jnp.where(keep, lmat[0], 0.0), rmat[0])
                - nt3(jnp.where(keep, lmat[1], 0.0), rmat[1]))

    m = causal_ref[...] * kernel_half(fwd_rows) + anti_ref[...] * kernel_half(jnp.logical_not(fwd_rows))
    m_ref[...] = m.astype(m_ref.dtype)
    w_ref[...] = jnp.concatenate([wmat[0], wmat[1]], axis=1).astype(w_ref.dtype)
    vcat = jnp.concatenate([vmat[0], -vmat[1]], axis=1)
    keep_f = (lax.broadcasted_iota(jnp.int32, vcat.shape, 1) % lanes) < _S5_P
    vf_ref[...] = jnp.where(keep_f, vcat, 0.0).T.astype(vf_ref.dtype)
    vb_ref[...] = jnp.where(keep_f, 0.0, vcat).T.astype(vb_ref.dtype)
    chunk_k = jnp.full((1, lanes), c, jnp.int32)
    a_chunk = powers(chunk_k)
    a_ref[0:1, :] = a_chunk[0]
    a_ref[1:2, :] = a_chunk[1]


def _s5_mats(prm, bt_re, bt_im, c_re, c_im):
    g = prm.shape[0]
    cw, lanes = _CHUNK_W, _S5_LANES
    causal, anti = (jnp.asarray(a) for a in _s5_masks())
    gspec = pl.BlockSpec((None, _S5_H, lanes), lambda i: (i, 0, 0))
    mat_spec = pl.BlockSpec((None, cw, cw), lambda i: (i, 0, 0))
    mat_sds = jax.ShapeDtypeStruct((g, cw, cw), _BF16)
    return pl.pallas_call(
        _s5_mats_kernel,
        grid=(g,),
        in_specs=[pl.BlockSpec((None, 3, lanes), lambda i: (i, 0, 0)), gspec, gspec, gspec, gspec,
                  _const_spec((cw, cw)), _const_spec((cw, cw))],
        out_specs=[mat_spec] * 4 + [pl.BlockSpec((None, 2, lanes), lambda i: (i, 0, 0))],
        out_shape=[mat_sds] * 4 + [jax.ShapeDtypeStruct((g, 2, lanes), _F32)],
        compiler_params=_params(("arbitrary",)),
        name="s5_mats",
    )(prm, bt_re, bt_im, c_re, c_im, causal, anti)


def _segment_transpose(xs):
    n = len(xs)
    seg = lax.broadcasted_iota(jnp.int32, xs[0].shape, 1) // _S5_H
    diags = []
    for k in range(n):
        z = xs[-k % n]
        for m in range(1, n):
            z = jnp.where(seg == m, xs[(m - k) % n], z)
        diags.append(pltpu.roll(z, ((n - k) % n) * _S5_H, 1) if k else z)
    ys = []
    for b in range(n):
        y = diags[b]
        for a in range(1, n):
            y = jnp.where(seg == a, diags[(b - a) % n], y)
        ys.append(y)
    return ys


def _s5_scan_kernel(nchunk, bsz, has_init, *refs):
    if has_init:
        u_ref, m_ref, w_ref, vf_ref, vb_ref, a_ref, h0_ref, y_ref = refs[:8]
    else:
        u_ref, m_ref, w_ref, vf_ref, vb_ref, a_ref, y_ref, fin_ref = refs[:8]
    s_ref, pf_ref, pb_ref, uall_ref, yall_ref = refs[8:]
    lanes, gb = _S5_LANES, _S5_GB
    rows = nchunk * bsz
    fwd = lax.broadcasted_iota(jnp.int32, (bsz, lanes), 1) < _S5_P

    for half in range(2):
        per_group = _segment_transpose([u_ref[half * 8 + s8] for s8 in range(8)])
        for gl in range(gb):
            uall_ref[gl, :, half * 128:(half + 1) * 128] = per_group[gl].astype(_BF16)

    for gl in range(gb):
        u = uall_ref[gl]
        yall_ref[gl] = jnp.dot(u, m_ref[gl], preferred_element_type=_F32)
        s = jnp.dot(u, w_ref[gl], preferred_element_type=_F32)
        s_ref[...] = jnp.swapaxes(s.reshape(bsz, nchunk, 2 * lanes), 0, 1).reshape(rows, 2 * lanes)

        a_re = a_ref[gl, 0:1, :]
        a_im = a_ref[gl, 1:2, :]
        if has_init:
            init = (h0_ref[gl, :, 0:lanes], h0_ref[gl, :, lanes:2 * lanes])
        else:
            init = (jnp.zeros((bsz, lanes), _F32), jnp.zeros((bsz, lanes), _F32))

        def step(k, carry):
            h_re, h_im = carry
            rowf = pl.ds(pl.multiple_of(k * bsz, bsz), bsz)
            rowb = pl.ds(pl.multiple_of((nchunk - 1 - k) * bsz, bsz), bsz)
            pf_ref[rowf, 0:lanes] = h_re
            pf_ref[rowf, lanes:2 * lanes] = h_im
            pb_ref[rowb, 0:lanes] = h_re
            pb_ref[rowb, lanes:2 * lanes] = h_im
            s_re = jnp.where(fwd, s_ref[rowf, 0:lanes], s_ref[rowb, 0:lanes])
            s_im = jnp.where(fwd, s_ref[rowf, lanes:2 * lanes], s_ref[rowb, lanes:2 * lanes])
            return (a_re * h_re - a_im * h_im + s_re, a_re * h_im + a_im * h_re + s_im)

        fin = lax.fori_loop(0, nchunk, step, init)
        if not has_init:
            fin_ref[gl, :, 0:lanes] = fin[0]
            fin_ref[gl, :, lanes:2 * lanes] = fin[1]
        carried = (jnp.dot(pf_ref[...].astype(_BF16), vf_ref[gl], preferred_element_type=_F32)
                   + jnp.dot(pb_ref[...].astype(_BF16), vb_ref[gl], preferred_element_type=_F32))
        carried = jnp.swapaxes(carried.reshape(nchunk, bsz, 2 * lanes), 0, 1).reshape(rows, 2 * lanes)
        yall_ref[gl] = yall_ref[gl] + carried

    for half in range(2):
        per_step = _segment_transpose([yall_ref[gl, :, half * 128:(half + 1) * 128] for gl in range(gb)])
        for t8 in range(8):
            y_ref[half * 8 + t8] = per_step[t8]


def _s5_scan(u, bsz, mats, init):
    steps, rows, d = u.shape
    g = d // _S5_H
    nchunk = rows // bsz
    cw, gb = _CHUNK_W, _S5_GB
    lanes = gb * _S5_H
    tok_spec = pl.BlockSpec((steps, rows, lanes), lambda i: (0, 0, i))
    mat_spec = pl.BlockSpec((gb, cw, cw), lambda i: (i, 0, 0))
    st_spec = pl.BlockSpec((gb, bsz, cw), lambda i: (i, 0, 0))
    ins = [u] + list(mats)
    specs = [tok_spec] + [mat_spec] * 4 + [pl.BlockSpec((gb, 2, _S5_LANES), lambda i: (i, 0, 0))]
    y_sds = jax.ShapeDtypeStruct(u.shape, _F32)
    if init is not None:
        ins.append(init)
        specs.append(st_spec)
        out_specs, out_shape = tok_spec, y_sds
    else:
        out_specs = [tok_spec, st_spec]
        out_shape = [y_sds, jax.ShapeDtypeStruct((g, bsz, cw), _F32)]
    out = pl.pallas_call(
        functools.partial(_s5_scan_kernel, nchunk, bsz, init is not None),
        grid=(g // gb,),
        in_specs=specs, out_specs=out_specs, out_shape=out_shape,
        scratch_shapes=[pltpu.VMEM((rows, cw), _F32)] * 3
        + [pltpu.VMEM((gb, rows, cw), _BF16), pltpu.VMEM((gb, rows, cw), _F32)],
        compiler_params=_params(("arbitrary",)),
        name=f"s5_{nchunk * steps}",
    )(*ins)
    if init is not None:
        return out, None
    return out[0], out[1]


def _trunk(x, pos, row_of_b, init_state, mod5, filt, s5_mats, wts):
    y0, z = _pre_hyena(x, pos, mod5, row_of_b, 0, wts["norm1_g"], wts["hy_in_w"], wts["hy_in_b"])
    v = _hyena_conv(z, wts["hy_conv_w"], wts["hy_conv_b"], wts["hy_fbias"], *filt)
    x1, u = _post_mixer("hyena", "next", y0, v, mod5, row_of_b, 0, wts["hy_out_w"], wts["hy_out_b"],
                        wts["norm1_g"], None, wts["norm2_g"], wts["ffn_w13"], wts["ffn_w2"], None)
    ys, fin = _s5_scan(u, x.shape[0], s5_mats, init_state)
    (out,) = _post_mixer("s5", "final", x1, ys, mod5, row_of_b, 1, wts["s5_glu_w"], wts["s5_glu_b"],
                         wts["norm1_g"], wts["s5_D"], wts["norm2_g"], wts["ffn_w13"], wts["ffn_w2"],
                         wts["final_g"])
    return out, fin


def kernel(x_prompt, x_sample, state_s5, c, c_ctx, norm1_g, norm2_g, final_g, ada_w, ada_b, ffn_w13, ffn_w2, hy_in_w, hy_in_b, hy_conv_w, hy_conv_b, hy_pe_w1, hy_pe_b1, hy_pe_w2, hy_pe_b2, hy_pe_w3, hy_freq, hy_fbias, hy_out_w, hy_out_b, s5_A_re, s5_A_im, s5_log_dt, s5_B_re, s5_B_im, s5_C_re, s5_C_im, s5_D, s5_glu_w, s5_glu_b):
    depth, d = norm1_g.shape
    assert depth == 2 and hy_in_w.shape[0] == 1 and s5_glu_w.shape[0] == 1
    dec_b, dec_seq, _ = x_sample.shape
    g = d // _S5_H
    p = _S5_P

    nrow = -(-(1 + dec_b) // 8) * 8
    cond = jnp.concatenate([c_ctx[None], c, jnp.zeros((nrow - 1 - dec_b, d), _F32)], axis=0)
    mod5 = _modulation(cond, ada_w, ada_b)

    row3 = lambda a: a.reshape(a.shape[0], 1, a.shape[-1])
    wts = dict(
        norm1_g=row3(norm1_g), norm2_g=row3(norm2_g), final_g=final_g[None],
        hy_in_w=hy_in_w.astype(_BF16), hy_in_b=row3(hy_in_b),
        hy_conv_w=hy_conv_w[0], hy_conv_b=hy_conv_b, hy_fbias=hy_fbias[0],
        hy_out_w=hy_out_w.astype(_BF16), hy_out_b=row3(hy_out_b),
        ffn_w13=ffn_w13.astype(_BF16), ffn_w2=ffn_w2.astype(_BF16),
        s5_glu_w=s5_glu_w.astype(_BF16), s5_glu_b=row3(s5_glu_b), s5_D=row3(s5_D),
    )
    dirs_on_lanes = lambda a: a.transpose(1, 0, 2).reshape(g, 2 * p)
    ldt = jnp.broadcast_to(s5_log_dt[0][:, :, None], (2, g, p))
    prm = jnp.stack([dirs_on_lanes(s5_A_re[0]), dirs_on_lanes(s5_A_im[0]), dirs_on_lanes(ldt)], axis=1)
    bt = lambda a: a.transpose(1, 3, 0, 2).reshape(g, _S5_H, 2 * p)
    ct = lambda a: a.transpose(1, 2, 0, 3).reshape(g, _S5_H, 2 * p)
    s5_mats = _s5_mats(prm, bt(s5_B_re[0]), bt(s5_B_im[0]), ct(s5_C_re[0]), ct(s5_C_im[0]))

    w1p = jnp.pad(hy_pe_w1[0], ((0, _HY_FO - _HY_EMB), (0, 0)))
    filt_args = (w1p, hy_pe_b1, hy_pe_w2[0], hy_pe_b2, hy_freq, hy_pe_w3[0])
    filt_ctx = _hyena_filters(x_prompt.shape[1], d, *filt_args)
    filt_lat = _hyena_filters(dec_seq, d, *filt_args)

    y_prompt, fin = _trunk(x_prompt, None, lambda b: 0, None, mod5, filt_ctx, s5_mats, wts)
    new_state = fin.reshape(g, -1, 2, 2, p).transpose(1, 3, 2, 0, 4)[:, None]

    init = state_s5[:, 0].transpose(3, 0, 2, 1, 4).reshape(g, dec_b, 4 * p)
    pos = jnp.asarray(_pos_embed(dec_seq // _GRID_W, d))
    y_sample, _ = _trunk(x_sample, pos, lambda b: b + 1, init, mod5, filt_lat, s5_mats, wts)
    return (y_prompt, y_sample, new_state)
```

```python
import functools
import math

import numpy as np
import jax
import jax.numpy as jnp
from jax import lax
from jax.experimental import pallas as pl
from jax.experimental.pallas import tpu as pltpu

_F32 = jnp.float32
_BF16 = jnp.bfloat16

_EPS = 1e-6
_GRID_W = 64
_POS_BASE = 10000.0
_HY_ORDER = 2
_HY_EMB = 33
_HY_FO = 64
_HY_TARGET = 1e-2
_HY_FAST = 0.3
_HY_SLOW = 1.5
_S5_H = 16
_S5_P = 64
_S5_CHUNK = 16
_CHUNK_W = _S5_CHUNK * _S5_H

_VMEM_LIMIT = 56 * 1024 * 1024
_TOKEN_TILE = 256
_POST_TILE = 512
_POST_SUB = 256
_CONV_DBLK = 512
_FILT_DBLK = 512


def _params(sem):
    return pltpu.CompilerParams(dimension_semantics=sem, vmem_limit_bytes=_VMEM_LIMIT)


def _dot_bf16(a, b):
    return jnp.dot(a.astype(_BF16), b.astype(_BF16), preferred_element_type=_F32)


def _split_bf16(x):
    hi = x.astype(_BF16)
    return hi, (x - hi.astype(_F32)).astype(_BF16)


def _dot_3pass(a, b):
    ah, al = _split_bf16(a)
    bh, bl = _split_bf16(b)
    return (jnp.dot(ah, bh, preferred_element_type=_F32) + jnp.dot(al, bh, preferred_element_type=_F32)
            + jnp.dot(ah, bl, preferred_element_type=_F32))


def _dot_nt(a, b, precision=None):
    return lax.dot_general(a, b, (((1,), (1,)), ((), ())), precision=precision,
                           preferred_element_type=_F32)


def _rmsnorm(x, g):
    ms = jnp.mean(x * x, axis=-1, keepdims=True)
    return x * lax.rsqrt(ms + _EPS) * g


def _silu(x):
    return x * jax.nn.sigmoid(x)


def _gelu_tanh(x):
    c = math.sqrt(2.0 / math.pi)
    return 0.5 * x * (1.0 + jnp.tanh(c * (x + 0.044715 * (x * x * x))))


def _const_spec(shape):
    nd = len(shape)
    return pl.BlockSpec(shape, lambda *_: (0,) * nd, pipeline_mode=pl.Buffered(1))


def _layer_spec(shape, layer):
    nd = len(shape)
    return pl.BlockSpec((None,) + tuple(shape), lambda *_: (layer,) + (0,) * nd,
                        pipeline_mode=pl.Buffered(1))


def _mod_spec(d, layer, part, row_of_b):
    return pl.BlockSpec((None, None, None, 1, d), lambda t, b: (layer, part, row_of_b(b), 0, 0))


def _mod_kernel(cond_ref, w_ref, b_ref, o_ref):
    o_ref[...] = _dot_3pass(_silu(cond_ref[...]), w_ref[...]) + b_ref[...]


def _modulation(cond, ada_w, ada_b):
    depth, d, d6 = ada_w.shape
    parts = d6 // d
    rows = cond.shape[0]
    out = pl.pallas_call(
        _mod_kernel,
        grid=(depth, parts),
        in_specs=[
            pl.BlockSpec((rows, d), lambda l, p: (0, 0)),
            pl.BlockSpec((None, d, d), lambda l, p: (l, 0, p)),
            pl.BlockSpec((None, None, 1, d), lambda l, p: (l, p, 0, 0)),
        ],
        out_specs=pl.BlockSpec((None, None, rows, d), lambda l, p: (l, p, 0, 0)),
        out_shape=jax.ShapeDtypeStruct((depth, parts, rows, d), _F32),
        compiler_params=_params(("arbitrary", "arbitrary")),
        name="mod",
    )(cond, ada_w, ada_b.reshape(depth, parts, 1, d))
    return out.reshape(depth, parts, rows, 1, d)


@functools.lru_cache(maxsize=None)
def _dft_consts(seq):
    f = np.arange(seq)[:, None]
    t = np.arange(seq)[None, :]
    ang = np.pi * ((f * t) % (2 * seq)) / seq
    cos = np.cos(ang)
    sin = np.sin(ang)
    alt = np.where(np.arange(seq) % 2 == 0, 1.0, -1.0)
    fwd_top = cos
    fwd_bot = -sin
    fwd_bot[0] = alt
    wgt = np.full((seq, 1), 2.0)
    wgt[0] = 1.0
    inv_top = (cos * wgt).T / (2 * seq)
    inv_bot = (-2.0 * sin).T / (2 * seq)
    inv_bot[:, 0] = alt / (2 * seq)
    fwd = np.concatenate([fwd_top, fwd_bot], axis=0).astype(np.float32)
    inv = np.concatenate([inv_top, inv_bot], axis=1).astype(np.float32)
    return fwd, inv, alt.astype(np.float32)[:, None]


@functools.lru_cache(maxsize=None)
def _filter_consts(seq, d):
    t = np.linspace(0.0, 1.0, seq)[:, None]
    w = 2.0 * np.pi * np.arange(seq)[:, None] / seq
    nb = (_HY_EMB - 1) // 2
    bands = np.linspace(1e-4, nb - 1, nb)[None, :]
    z = np.concatenate([t, np.cos(bands * w), -np.sin(bands * w)], axis=-1)
    zpad = np.zeros((seq, _HY_FO))
    zpad[:, :_HY_EMB] = z
    max_decay = math.log(_HY_TARGET) / _HY_FAST
    min_decay = math.log(_HY_TARGET) / _HY_SLOW
    deltas = np.abs(np.linspace(min_decay, max_decay, d))[None, :]
    return zpad.astype(np.float32), deltas.astype(np.float32)


@functools.lru_cache(maxsize=None)
def _pos_embed(rows, d):
    quarter = d // 4
    omega = 1.0 / (_POS_BASE ** (np.arange(quarter, dtype=np.float64) / quarter))

    def axis_embed(n):
        ang = np.arange(n, dtype=np.float64)[:, None] * omega[None]
        return np.concatenate([np.sin(ang), np.cos(ang)], axis=-1)

    er = np.broadcast_to(axis_embed(rows)[:, None], (rows, _GRID_W, d // 2))
    ec = np.broadcast_to(axis_embed(_GRID_W)[None], (rows, _GRID_W, d // 2))
    return np.concatenate([er, ec], axis=-1).reshape(rows * _GRID_W, d).astype(np.float32)


@functools.lru_cache(maxsize=None)
def _s5_masks():
    step = np.repeat(np.arange(_S5_CHUNK), _S5_H)
    causal = (step[None, :] >= step[:, None]).astype(np.float32)
    anti = (step[:, None] >= step[None, :]).astype(np.float32)
    return causal, anti


def _filter_kernel(z_ref, w1_ref, b1_ref, w2_ref, b2_ref, fr_ref, w3f_ref, w3b_ref, dl_ref,
                   at_ref, ab_ref, alt_ref, krt_ref, krb_ref, ki_ref):
    z = z_ref[...]
    fr = fr_ref[...]
    h = jnp.sin(fr * (_dot_3pass(z, w1_ref[...]) + b1_ref[...]))
    h = jnp.sin(fr * (_dot_3pass(h, w2_ref[...]) + b2_ref[...]))
    decay = jnp.exp(-z[:, 0:1] * dl_ref[...])
    hf = _dot_3pass(h, w3f_ref[...]) * decay
    hb = _dot_3pass(h, w3b_ref[...]) * decay
    norm = (jnp.sum(jnp.abs(hf), axis=0, keepdims=True)
            + jnp.sum(jnp.abs(hb), axis=0, keepdims=True) + _EPS)
    hf = hf / norm
    hb = hb / norm
    first = lax.broadcasted_iota(jnp.int32, hf.shape, 0) == 0
    hb = jnp.where(first, 0.0, hb)
    ksum = hf + hb
    kdiff = hf - hb
    kre = jnp.dot(at_ref[...], ksum.astype(_BF16), preferred_element_type=_F32)
    kim = jnp.dot(ab_ref[...], kdiff.astype(_BF16), preferred_element_type=_F32)
    nyq = jnp.sum(alt_ref[...] * ksum, axis=0, keepdims=True)
    krt_ref[...] = kre
    krb_ref[...] = jnp.where(first, nyq, kre)
    ki_ref[...] = jnp.where(first, 0.0, kim)


def _hyena_filters(seq, d, w1p, b1, w2, b2, freq, w3):
    zpad, deltas = _filter_consts(seq, d)
    fwd, _, alt = _dft_consts(seq)
    fwd = jnp.asarray(fwd).astype(_BF16)
    top = pl.BlockSpec((seq, seq), lambda o, j: (0, 0), pipeline_mode=pl.Buffered(1))
    bot = pl.BlockSpec((seq, seq), lambda o, j: (1, 0), pipeline_mode=pl.Buffered(1))
    nb = d // _FILT_DBLK
    fo = _HY_FO
    out_sds = jax.ShapeDtypeStruct((_HY_ORDER, seq, d), _F32)
    out_spec = pl.BlockSpec((None, seq, _FILT_DBLK), lambda o, j: (o, 0, j))
    return pl.pallas_call(
        _filter_kernel,
        grid=(_HY_ORDER, nb),
        in_specs=[
            _const_spec((seq, fo)), _const_spec((fo, fo)), _const_spec((1, fo)),
            _const_spec((fo, fo)), _const_spec((1, fo)), _const_spec((1, fo)),
            pl.BlockSpec((fo, _FILT_DBLK), lambda o, j: (0, (2 * o) * nb + j)),
            pl.BlockSpec((fo, _FILT_DBLK), lambda o, j: (0, (2 * o + 1) * nb + j)),
            pl.BlockSpec((1, _FILT_DBLK), lambda o, j: (0, j)),
            top, bot,
            _const_spec((seq, 1)),
        ],
        out_specs=[out_spec, out_spec, out_spec],
        out_shape=[out_sds, out_sds, out_sds],
        compiler_params=_params(("arbitrary", "arbitrary")),
        name=f"filt{seq}",
    )(jnp.asarray(zpad), w1p, b1, w2, b2, freq, w3, w3, jnp.asarray(deltas),
      fwd, fwd, jnp.asarray(alt))


def _pre_kernel(has_pos, *refs):
    if has_pos:
        x_ref, pos_ref, sh_ref, sc_ref, g_ref, w_ref, b_ref, z_ref = refs
        x = x_ref[...] + pos_ref[...]
    else:
        x_ref, sh_ref, sc_ref, g_ref, w_ref, b_ref, z_ref = refs
        x = x_ref[...]
    h = _rmsnorm(x, g_ref[...]) * (1.0 + sc_ref[...]) + sh_ref[...]
    z_ref[...] = (_dot_bf16(h, w_ref[...]) + b_ref[...]).astype(z_ref.dtype)


def _pre_hyena(x, pos, mod5, row_of_b, layer, norm_g, w_in, b_in):
    bsz, seq, d = x.shape
    n = w_in.shape[-1]
    tm = _TOKEN_TILE
    tok = pl.BlockSpec((None, tm, d), lambda t, b: (b, t, 0))
    ins, specs = [x], [tok]
    if pos is not None:
        ins.append(pos)
        specs.append(pl.BlockSpec((tm, d), lambda t, b: (t, 0)))
    ins += [mod5, mod5, norm_g, w_in, b_in]
    specs += [_mod_spec(d, layer, 0, row_of_b), _mod_spec(d, layer, 1, row_of_b),
              _layer_spec((1, d), layer), _layer_spec((d, n), 0), _layer_spec((1, n), 0)]
    return pl.pallas_call(
        functools.partial(_pre_kernel, pos is not None),
        grid=(seq // tm, bsz),
        in_specs=specs,
        out_specs=pl.BlockSpec((None, tm, n), lambda t, b: (b, t, 0)),
        out_shape=jax.ShapeDtypeStruct((bsz, seq, n), _BF16),
        compiler_params=_params(("arbitrary", "arbitrary")),
        name=f"pre{seq}",
    )(*ins)


def _conv_kernel(zv_ref, z1_ref, z2_ref, wv_ref, w1_ref, w2_ref, bv_ref, b1_ref, b2_ref, fb_ref,
                 krt_ref, krb_ref, ki_ref, ft_ref, fb2_ref, it_ref, ib_ref, o_ref):
    seq, dblk = zv_ref.shape
    row = lax.broadcasted_iota(jnp.int32, (seq, dblk), 0)
    first = row == 0
    last = row == seq - 1

    def short_conv(z_ref, w_ref, b_ref):
        z = z_ref[...].astype(_F32)
        prev = jnp.where(first, 0.0, pltpu.roll(z, 1, 0))
        nxt = jnp.where(last, 0.0, pltpu.roll(z, seq - 1, 0))
        w = w_ref[...]
        return prev * w[0:1] + z * w[1:2] + nxt * w[2:3] + b_ref[...]

    v = short_conv(zv_ref, wv_ref, bv_ref)
    gates = (short_conv(z1_ref, w1_ref, b1_ref), short_conv(z2_ref, w2_ref, b2_ref))
    for o in range(_HY_ORDER):
        vb = v.astype(_BF16)
        p = jnp.dot(ft_ref[...], vb, preferred_element_type=_F32)
        q = jnp.dot(fb2_ref[...], vb, preferred_element_type=_F32)
        ki = ki_ref[o]
        ytop = p * krt_ref[o] - q * ki
        ybot = p * ki + q * krb_ref[o]
        y = (jnp.dot(it_ref[...], ytop.astype(_BF16), preferred_element_type=_F32)
             + jnp.dot(ib_ref[...], ybot.astype(_BF16), preferred_element_type=_F32))
        v = (y + v * fb_ref[o:o + 1, :]) * gates[o]
    o_ref[...] = v.astype(o_ref.dtype)


def _hyena_conv(z, conv_w, conv_b, fbias, krt, krb, ki):
    bsz, seq, d3 = z.shape
    d = d3 // 3
    dblk = _CONV_DBLK
    nb = d // dblk
    fwd, inv, _ = _dft_consts(seq)
    fwd = jnp.asarray(fwd).astype(_BF16)
    inv = jnp.asarray(inv).astype(_BF16)

    def mspec(r, c):
        return pl.BlockSpec((seq, seq), lambda j, b: (r, c), pipeline_mode=pl.Buffered(1))

    def zspec(k):
        return pl.BlockSpec((None, seq, dblk), lambda j, b: (b, 0, k * nb + j))

    def wspec(rows, k):
        return pl.BlockSpec((rows, dblk), lambda j, b: (0, k * nb + j))

    kspec = pl.BlockSpec((_HY_ORDER, seq, dblk), lambda j, b: (0, 0, j), pipeline_mode=pl.Buffered(1))
    return pl.pallas_call(
        _conv_kernel,
        grid=(nb, bsz),
        in_specs=[zspec(0), zspec(1), zspec(2),
                  wspec(3, 0), wspec(3, 1), wspec(3, 2),
                  wspec(1, 0), wspec(1, 1), wspec(1, 2),
                  wspec(_HY_ORDER, 0),
                  kspec, kspec, kspec,
                  mspec(0, 0), mspec(1, 0), mspec(0, 0), mspec(0, 1)],
        out_specs=pl.BlockSpec((None, seq, dblk), lambda j, b: (b, 0, j)),
        out_shape=jax.ShapeDtypeStruct((bsz, seq, d), _BF16),
        compiler_params=_params(("arbitrary", "arbitrary")),
        name=f"conv{seq}",
    )(z, z, z, conv_w, conv_w, conv_w, conv_b, conv_b, conv_b, fbias, krt, krb, ki,
      fwd, fwd, inv, inv)


def _post_kernel(kind, tail, names, *refs):
    r = dict(zip(names, refs))
    tm, d = r["x"].shape
    steps = _S5_CHUNK
    nsub = max(tm // _POST_SUB, 1)
    sub = tm // nsub

    def stages(i):
        rows = slice(i * sub, (i + 1) * sub)
        crows = slice(i * sub // steps, (i + 1) * sub // steps)
        st = {}

        def head():
            st["x"] = r["x"][rows, :]
            if "pos" in r:
                st["x"] = st["x"] + r["pos"][rows, :]
            if kind == "hyena":
                st["mix"] = r["mix"][rows, :]
            else:
                u = _rmsnorm(st["x"], r["n1"][...]) * (1.0 + r["sc1"][...]) + r["sh1"][...]
                mix = jnp.swapaxes(r["mix"][:, crows, :], 0, 1).reshape(sub, d)
                st["mix"] = _gelu_tanh(mix + u * r["skip"][...]).astype(_BF16)

        def mixer_out():
            st["m"] = _dot_bf16(st.pop("mix"), r["w_out"][...]) + r["b_out"][...]

        def mid():
            m = st.pop("m")
            if kind == "s5":
                half = m.shape[-1] // 2
                m = m[:, :half] * jax.nn.sigmoid(m[:, half:])
            st["x"] = st["x"] + r["g1"][...] * m
            h = _rmsnorm(st["x"], r["n2"][...]) * (1.0 + r["sc2"][...]) + r["sh2"][...]
            st["h"] = h.astype(_BF16)

        def ffn_in():
            st["ab"] = _dot_bf16(st.pop("h"), r["w13"][...])

        def act():
            ab = st.pop("ab")
            ff = ab.shape[-1] // 2
            st["act"] = (_silu(ab[:, :ff]) * ab[:, ff:]).astype(_BF16)

        def ffn_out():
            st["f"] = _dot_bf16(st.pop("act"), r["w2"][...])

        def finish():
            x = st.pop("x") + r["g2"][...] * st.pop("f")
            if tail == "next":
                r["o_x"][rows, :] = x
                un = _rmsnorm(x, r["nn"][...]) * (1.0 + r["scn"][...]) + r["shn"][...]
                r["o_u"][:, crows, :] = jnp.swapaxes(un.reshape(-1, steps, d), 0, 1)
            else:
                r["o_x"][rows, :] = _rmsnorm(x, r["nf"][...])

        return [head, mixer_out, mid, ffn_in, act, ffn_out, finish]

    pipes = [stages(i) for i in range(nsub)]
    depth = len(pipes[0])
    for t in range(depth + nsub - 1):
        for i, pipe in enumerate(pipes):
            if 0 <= t - i < depth:
                pipe[t - i]()


def _post_mixer(kind, tail, x, pos, mix, mod5, row_of_b, layer, w_out, b_out, norm1_g, skip,
                norm2_g, w13, w2, final_g):
    bsz, seq, d = x.shape
    tm = min(_POST_TILE, seq)
    tok = pl.BlockSpec((None, tm, d), lambda t, b: (b, t, 0))
    nt = seq // tm
    steps = _S5_CHUNK
    stepmajor = pl.BlockSpec((steps, tm // steps, d), lambda t, b: (0, b * nt + t, 0))
    stepmajor_sds = jax.ShapeDtypeStruct((steps, bsz * seq // steps, d), _F32)
    n_out = w_out.shape[-1]
    f2 = w13.shape[-1]

    def mod(part, lyr=layer):
        return (mod5, _mod_spec(d, lyr, part, row_of_b))

    items = [("x", x, tok), ("mix", mix, stepmajor if kind == "s5" else tok),
             ("w_out", w_out, _layer_spec((d, n_out), 0)), ("b_out", b_out, _layer_spec((1, n_out), 0)),
             ("g1",) + mod(2), ("sh2",) + mod(3), ("sc2",) + mod(4), ("g2",) + mod(5),
             ("n2", norm2_g, _layer_spec((1, d), layer)),
             ("w13", w13, _layer_spec((d, f2), layer)), ("w2", w2, _layer_spec((f2 // 2, d), layer))]
    if kind == "s5":
        items += [("n1", norm1_g, _layer_spec((1, d), layer)), ("sh1",) + mod(0), ("sc1",) + mod(1),
                  ("skip", skip, _layer_spec((1, d), 0))]
    if pos is not None:
        items.append(("pos", pos, pl.BlockSpec((tm, d), lambda t, b: (t, 0))))
    out_specs, out_shape = [tok], [jax.ShapeDtypeStruct(x.shape, _F32)]
    names_out = ["o_x"]
    if tail == "next":
        items += [("nn", norm1_g, _layer_spec((1, d), layer + 1)),
                  ("shn",) + mod(0, layer + 1), ("scn",) + mod(1, layer + 1)]
        out_specs.append(stepmajor)
        out_shape.append(stepmajor_sds)
        names_out.append("o_u")
    else:
        items.append(("nf", final_g, _const_spec((1, d))))
    names = tuple(i[0] for i in items) + tuple(names_out)
    out = pl.pallas_call(
        functools.partial(_post_kernel, kind, tail, names),
        grid=(seq // tm, bsz),
        in_specs=[i[2] for i in items], out_specs=out_specs, out_shape=out_shape,
        compiler_params=_params(("arbitrary", "arbitrary")),
        name=f"post_{kind}{seq}",
    )(*[i[1] for i in items])
    return out


_S5_GB = 8
_S5_LANES = 2 * _S5_P
_S5_MATS_GB = 4


def _cmul(ar, ai, br, bi):
    return ar * br - ai * bi, ar * bi + ai * br


def _s5_mats_kernel(prm_ref, btr_ref, bti_ref, cr_ref, ci_ref, causal_ref, anti_ref,
                    m_ref, w_ref, vf_ref, vb_ref, a_ref):
    c, h, lanes = _S5_CHUNK, _S5_H, _S5_LANES
    a_re = prm_ref[0:1, :]
    a_im = prm_ref[1:2, :]
    dt = jnp.exp(prm_ref[2:3, :])
    sr = dt * a_re
    ph = dt * a_im
    mag = jnp.exp(sr)
    nr = mag * jnp.cos(ph) - 1.0
    ni = mag * jnp.sin(ph)
    den = a_re * a_re + a_im * a_im
    co_re = (nr * a_re + ni * a_im) / den
    co_im = (ni * a_re - nr * a_im) / den
    bt = _cmul(co_re, co_im, btr_ref[...], bti_ref[...])
    ct = (cr_ref[...], ci_ref[...])

    def powers(k):
        kf = k.astype(_F32)
        e = jnp.exp(kf * sr)
        return e * jnp.cos(kf * ph), e * jnp.sin(kf * ph)

    def per_step(x):
        return jnp.broadcast_to(x[:, None, :], (c, h, lanes)).reshape(c * h, lanes)

    def per_chan(x):
        return jnp.broadcast_to(x[None, :, :], (c, h, lanes)).reshape(c * h, lanes)

    step = lax.broadcasted_iota(jnp.int32, (c, lanes), 0)
    fwd = lax.broadcasted_iota(jnp.int32, (c, lanes), 1) < _S5_P
    tb = tuple(per_chan(x) for x in bt)
    tc = tuple(per_chan(x) for x in ct)

    def table(base, k_fwd, k_bwd):
        return _cmul(*base, *(per_step(x) for x in powers(jnp.where(fwd, k_fwd, k_bwd))))

    lmat = table(tb, -step, step)
    rmat = table(tc, step, -step)
    wmat = table(tb, c - 1 - step, step)
    vmat = table(tc, step + 1, c - step)

    def nt3(x, y):
        xh, xl = _split_bf16(x)
        yh, yl = _split_bf16(y)
        return _dot_nt(xh, yh) + _dot_nt(xl, yh) + _dot_nt(xh, yl)

    fwd_rows = lax.broadcasted_iota(jnp.int32, (c * h, lanes), 1) < _S5_P

    def kernel_half(keep):
        return (nt3(jnp.where(keep, lmat[0], 0.0), rmat[0])
                - nt3(jnp.where(keep, lmat[1], 0.0), rmat[1]))

    m = causal_ref[...] * kernel_half(fwd_rows) + anti_ref[...] * kernel_half(jnp.logical_not(fwd_rows))
    m_ref[...] = m.astype(m_ref.dtype)
    w_ref[...] = jnp.concatenate([wmat[0], wmat[1]], axis=1).astype(w_ref.dtype)
    vcat = jnp.concatenate([vmat[0], -vmat[1]], axis=1)
    keep_f = (lax.broadcasted_iota(jnp.int32, vcat.shape, 1) % lanes) < _S5_P
    vf_ref[...] = jnp.where(keep_f, vcat, 0.0).T.astype(vf_ref.dtype)
    vb_ref[...] = jnp.where(keep_f, 0.0, vcat).T.astype(vb_ref.dtype)
    chunk_k = jnp.full((1, lanes), c, jnp.int32)
    a_chunk = powers(chunk_k)
    a_ref[0:1, :] = a_chunk[0]
    a_ref[1:2, :] = a_chunk[1]


def _s5_mats_block_kernel(prm_ref, btr_ref, bti_ref, cr_ref, ci_ref, causal_ref, anti_ref,
                          m_ref, w_ref, vf_ref, vb_ref, a_ref):
    for j in range(prm_ref.shape[0]):
        _s5_mats_kernel(prm_ref.at[j], btr_ref.at[j], bti_ref.at[j], cr_ref.at[j], ci_ref.at[j],
                        causal_ref, anti_ref,
                        m_ref.at[j], w_ref.at[j], vf_ref.at[j], vb_ref.at[j], a_ref.at[j])


def _s5_mats(prm, bt_re, bt_im, c_re, c_im):
    g = prm.shape[0]
    cw, lanes = _CHUNK_W, _S5_LANES
    causal, anti = (jnp.asarray(a) for a in _s5_masks())
    gm = _S5_MATS_GB
    gspec = pl.BlockSpec((gm, _S5_H, lanes), lambda i: (i, 0, 0))
    mat_spec = pl.BlockSpec((gm, cw, cw), lambda i: (i, 0, 0))
    mat_sds = jax.ShapeDtypeStruct((g, cw, cw), _BF16)
    return pl.pallas_call(
        _s5_mats_block_kernel,
        grid=(g // gm,),
        in_specs=[pl.BlockSpec((gm, 3, lanes), lambda i: (i, 0, 0)), gspec, gspec, gspec, gspec,
                  _const_spec((cw, cw)), _const_spec((cw, cw))],
        out_specs=[mat_spec] * 4 + [pl.BlockSpec((gm, 2, lanes), lambda i: (i, 0, 0))],
        out_shape=[mat_sds] * 4 + [jax.ShapeDtypeStruct((g, 2, lanes), _F32)],
        compiler_params=_params(("arbitrary",)),
        name="s5_mats",
    )(prm, bt_re, bt_im, c_re, c_im, causal, anti)


def _segment_transpose(xs):
    n = len(xs)
    seg = lax.broadcasted_iota(jnp.int32, xs[0].shape, 1) // _S5_H
    diags = []
    for k in range(n):
        z = xs[-k % n]
        for m in range(1, n):
            z = jnp.where(seg == m, xs[(m - k) % n], z)
        diags.append(pltpu.roll(z, ((n - k) % n) * _S5_H, 1) if k else z)
    ys = []
    for b in range(n):
        y = diags[b]
        for a in range(1, n):
            y = jnp.where(seg == a, diags[(b - a) % n], y)
        ys.append(y)
    return ys


def _s5_scan_kernel(nchunk, bsz, has_init, *refs):
    if has_init:
        u_ref, m_ref, w_ref, vf_ref, vb_ref, a_ref, h0_ref, y_ref = refs[:8]
    else:
        u_ref, m_ref, w_ref, vf_ref, vb_ref, a_ref, y_ref, fin_ref = refs[:8]
    s_ref, pf_ref, pb_ref, uall_ref, yall_ref = refs[8:]
    lanes, gb = _S5_LANES, _S5_GB
    rows = nchunk * bsz
    fwd = lax.broadcasted_iota(jnp.int32, (bsz, lanes), 1) < _S5_P

    for half in range(2):
        per_group = _segment_transpose([u_ref[half * 8 + s8] for s8 in range(8)])
        for gl in range(gb):
            uall_ref[gl, :, half * 128:(half + 1) * 128] = per_group[gl].astype(_BF16)

    for gl in range(gb):
        u = uall_ref[gl]
        yall_ref[gl] = jnp.dot(u, m_ref[gl], preferred_element_type=_F32)
        s = jnp.dot(u, w_ref[gl], preferred_element_type=_F32)
        s_ref[...] = jnp.swapaxes(s.reshape(bsz, nchunk, 2 * lanes), 0, 1).reshape(rows, 2 * lanes)

        a_re = a_ref[gl, 0:1, :]
        a_im = a_ref[gl, 1:2, :]
        if has_init:
            init = (h0_ref[gl, :, 0:lanes], h0_ref[gl, :, lanes:2 * lanes])
        else:
            init = (jnp.zeros((bsz, lanes), _F32), jnp.zeros((bsz, lanes), _F32))

        def step(k, carry):
            h_re, h_im = carry
            rowf = pl.ds(pl.multiple_of(k * bsz, bsz), bsz)
            rowb = pl.ds(pl.multiple_of((nchunk - 1 - k) * bsz, bsz), bsz)
            pf_ref[rowf, 0:lanes] = h_re
            pf_ref[rowf, lanes:2 * lanes] = h_im
            pb_ref[rowb, 0:lanes] = h_re
            pb_ref[rowb, lanes:2 * lanes] = h_im
            s_re = jnp.where(fwd, s_ref[rowf, 0:lanes], s_ref[rowb, 0:lanes])
            s_im = jnp.where(fwd, s_ref[rowf, lanes:2 * lanes], s_ref[rowb, lanes:2 * lanes])
            return (a_re * h_re - a_im * h_im + s_re, a_re * h_im + a_im * h_re + s_im)

        fin = lax.fori_loop(0, nchunk, step, init)
        if not has_init:
            fin_ref[gl, :, 0:lanes] = fin[0]
            fin_ref[gl, :, lanes:2 * lanes] = fin[1]
        carried = (jnp.dot(pf_ref[...].astype(_BF16), vf_ref[gl], preferred_element_type=_F32)
                   + jnp.dot(pb_ref[...].astype(_BF16), vb_ref[gl], preferred_element_type=_F32))
        carried = jnp.swapaxes(carried.reshape(nchunk, bsz, 2 * lanes), 0, 1).reshape(rows, 2 * lanes)
        yall_ref[gl] = yall_ref[gl] + carried

    for half in range(2):
        per_step = _segment_transpose([yall_ref[gl, :, half * 128:(half + 1) * 128] for gl in range(gb)])
        for t8 in range(8):
            y_ref[half * 8 + t8] = per_step[t8]


def _s5_scan(u, bsz, mats, init):
    steps, rows, d = u.shape
    g = d // _S5_H
    nchunk = rows // bsz
    cw, gb = _CHUNK_W, _S5_GB
    lanes = gb * _S5_H
    tok_spec = pl.BlockSpec((steps, rows, lanes), lambda i: (0, 0, i))
    mat_spec = pl.BlockSpec((gb, cw, cw), lambda i: (i, 0, 0))
    st_spec = pl.BlockSpec((gb, bsz, cw), lambda i: (i, 0, 0))
    ins = [u] + list(mats)
    specs = [tok_spec] + [mat_spec] * 4 + [pl.BlockSpec((gb, 2, _S5_LANES), lambda i: (i, 0, 0))]
    y_sds = jax.ShapeDtypeStruct(u.shape, _F32)
    if init is not None:
        ins.append(init)
        specs.append(st_spec)
        out_specs, out_shape = tok_spec, y_sds
    else:
        out_specs = [tok_spec, st_spec]
        out_shape = [y_sds, jax.ShapeDtypeStruct((g, bsz, cw), _F32)]
    out = pl.pallas_call(
        functools.partial(_s5_scan_kernel, nchunk, bsz, init is not None),
        grid=(g // gb,),
        in_specs=specs, out_specs=out_specs, out_shape=out_shape,
        scratch_shapes=[pltpu.VMEM((rows, cw), _F32)] * 3
        + [pltpu.VMEM((gb, rows, cw), _BF16), pltpu.VMEM((gb, rows, cw), _F32)],
        compiler_params=_params(("arbitrary",)),
        name=f"s5_{nchunk * steps}",
    )(*ins)
    if init is not None:
        return out, None
    return out[0], out[1]


def _trunk(x, pos, row_of_b, init_state, mod5, filt, s5_mats, wts):
    z = _pre_hyena(x, pos, mod5, row_of_b, 0, wts["norm1_g"], wts["hy_in_w"], wts["hy_in_b"])
    v = _hyena_conv(z, wts["hy_conv_w"], wts["hy_conv_b"], wts["hy_fbias"], *filt)
    x1, u = _post_mixer("hyena", "next", x, pos, v, mod5, row_of_b, 0, wts["hy_out_w"], wts["hy_out_b"],
                        wts["norm1_g"], None, wts["norm2_g"], wts["ffn_w13"], wts["ffn_w2"], None)
    ys, fin = _s5_scan(u, x.shape[0], s5_mats, init_state)
    (out,) = _post_mixer("s5", "final", x1, None, ys, mod5, row_of_b, 1, wts["s5_glu_w"], wts["s5_glu_b"],
                         wts["norm1_g"], wts["s5_D"], wts["norm2_g"], wts["ffn_w13"], wts["ffn_w2"],
                         wts["final_g"])
    return out, fin


def kernel(x_prompt, x_sample, state_s5, c, c_ctx, norm1_g, norm2_g, final_g, ada_w, ada_b, ffn_w13, ffn_w2, hy_in_w, hy_in_b, hy_conv_w, hy_conv_b, hy_pe_w1, hy_pe_b1, hy_pe_w2, hy_pe_b2, hy_pe_w3, hy_freq, hy_fbias, hy_out_w, hy_out_b, s5_A_re, s5_A_im, s5_log_dt, s5_B_re, s5_B_im, s5_C_re, s5_C_im, s5_D, s5_glu_w, s5_glu_b):
    depth, d = norm1_g.shape
    assert depth == 2 and hy_in_w.shape[0] == 1 and s5_glu_w.shape[0] == 1
    dec_b, dec_seq, _ = x_sample.shape
    g = d // _S5_H
    p = _S5_P

    nrow = -(-(1 + dec_b) // 8) * 8
    cond = jnp.concatenate([c_ctx[None], c, jnp.zeros((nrow - 1 - dec_b, d), _F32)], axis=0)
    mod5 = _modulation(cond, ada_w, ada_b)

    row3 = lambda a: a.reshape(a.shape[0], 1, a.shape[-1])
    wts = dict(
        norm1_g=row3(norm1_g), norm2_g=row3(norm2_g), final_g=final_g[None],
        hy_in_w=hy_in_w.astype(_BF16), hy_in_b=row3(hy_in_b),
        hy_conv_w=hy_conv_w[0], hy_conv_b=hy_conv_b, hy_fbias=hy_fbias[0],
        hy_out_w=hy_out_w.astype(_BF16), hy_out_b=row3(hy_out_b),
        ffn_w13=ffn_w13.astype(_BF16), ffn_w2=ffn_w2.astype(_BF16),
        s5_glu_w=s5_glu_w.astype(_BF16), s5_glu_b=row3(s5_glu_b), s5_D=row3(s5_D),
    )
    dirs_on_lanes = lambda a: a.transpose(1, 0, 2).reshape(g, 2 * p)
    ldt = jnp.broadcast_to(s5_log_dt[0][:, :, None], (2, g, p))
    prm = jnp.stack([dirs_on_lanes(s5_A_re[0]), dirs_on_lanes(s5_A_im[0]), dirs_on_lanes(ldt)], axis=1)
    bt = lambda a: a.transpose(1, 3, 0, 2).reshape(g, _S5_H, 2 * p)
    ct = lambda a: a.transpose(1, 2, 0, 3).reshape(g, _S5_H, 2 * p)
    s5_mats = _s5_mats(prm, bt(s5_B_re[0]), bt(s5_B_im[0]), ct(s5_C_re[0]), ct(s5_C_im[0]))

    w1p = jnp.pad(hy_pe_w1[0], ((0, _HY_FO - _HY_EMB), (0, 0)))
    filt_args = (w1p, hy_pe_b1, hy_pe_w2[0], hy_pe_b2, hy_freq, hy_pe_w3[0])
    filt_ctx = _hyena_filters(x_prompt.shape[1], d, *filt_args)
    filt_lat = _hyena_filters(dec_seq, d, *filt_args)

    y_prompt, fin = _trunk(x_prompt, None, lambda b: 0, None, mod5, filt_ctx, s5_mats, wts)
    new_state = fin.reshape(g, -1, 2, 2, p).transpose(1, 3, 2, 0, 4)[:, None]

    init = state_s5[:, 0].transpose(3, 0, 2, 1, 4).reshape(g, dec_b, 4 * p)
    pos = jnp.asarray(_pos_embed(dec_seq // _GRID_W, d))
    y_sample, _ = _trunk(x_sample, pos, lambda b: b + 1, init, mod5, filt_lat, s5_mats, wts)
    return (y_prompt, y_sample, new_state)
```

```python
import functools
import math

import numpy as np
import jax
import jax.numpy as jnp
from jax import lax
from jax.experimental import pallas as pl
from jax.experimental.pallas import tpu as pltpu

_F32 = jnp.float32
_BF16 = jnp.bfloat16

_EPS = 1e-6
_GRID_W = 64
_POS_BASE = 10000.0
_HY_ORDER = 2
_HY_EMB = 33
_HY_FO = 64
_HY_TARGET = 1e-2
_HY_FAST = 0.3
_HY_SLOW = 1.5
_S5_H = 16
_S5_P = 64
_S5_CHUNK = 16
_CHUNK_W = _S5_CHUNK * _S5_H

_VMEM_LIMIT = 56 * 1024 * 1024
_TOKEN_TILE = 512
_POST_TILE = 512
_POST_SUB = 256
_CONV_DBLK = 512
_FILT_DBLK = 512


def _params(sem):
    return pltpu.CompilerParams(dimension_semantics=sem, vmem_limit_bytes=_VMEM_LIMIT)


def _dot_bf16(a, b):
    return jnp.dot(a.astype(_BF16), b.astype(_BF16), preferred_element_type=_F32)


def _split_bf16(x):
    hi = x.astype(_BF16)
    return hi, (x - hi.astype(_F32)).astype(_BF16)


def _dot_3pass(a, b):
    ah, al = _split_bf16(a)
    bh, bl = _split_bf16(b)
    return (jnp.dot(ah, bh, preferred_element_type=_F32) + jnp.dot(al, bh, preferred_element_type=_F32)
            + jnp.dot(ah, bl, preferred_element_type=_F32))


def _dot_nt(a, b, precision=None):
    return lax.dot_general(a, b, (((1,), (1,)), ((), ())), precision=precision,
                           preferred_element_type=_F32)


def _rmsnorm(x, g):
    ms = jnp.mean(x * x, axis=-1, keepdims=True)
    return x * lax.rsqrt(ms + _EPS) * g


def _silu(x):
    return x * jax.nn.sigmoid(x)


def _gelu_tanh(x):
    c = math.sqrt(2.0 / math.pi)
    return 0.5 * x * (1.0 + jnp.tanh(c * (x + 0.044715 * (x * x * x))))


def _const_spec(shape):
    nd = len(shape)
    return pl.BlockSpec(shape, lambda *_: (0,) * nd, pipeline_mode=pl.Buffered(1))


def _layer_spec(shape, layer):
    nd = len(shape)
    return pl.BlockSpec((None,) + tuple(shape), lambda *_: (layer,) + (0,) * nd,
                        pipeline_mode=pl.Buffered(1))


def _mod_spec(d, layer, part, row_of_b):
    return pl.BlockSpec((None, None, None, 1, d), lambda t, b: (layer, part, row_of_b(b), 0, 0))


def _mod_kernel(cond_ref, w_ref, b_ref, o_ref):
    o_ref[...] = _dot_3pass(_silu(cond_ref[...]), w_ref[...]) + b_ref[...]


def _modulation(cond, ada_w, ada_b):
    depth, d, d6 = ada_w.shape
    parts = d6 // d
    rows = cond.shape[0]
    out = pl.pallas_call(
        _mod_kernel,
        grid=(depth, parts),
        in_specs=[
            pl.BlockSpec((rows, d), lambda l, p: (0, 0)),
            pl.BlockSpec((None, d, d), lambda l, p: (l, 0, p)),
            pl.BlockSpec((None, None, 1, d), lambda l, p: (l, p, 0, 0)),
        ],
        out_specs=pl.BlockSpec((None, None, rows, d), lambda l, p: (l, p, 0, 0)),
        out_shape=jax.ShapeDtypeStruct((depth, parts, rows, d), _F32),
        compiler_params=_params(("arbitrary", "arbitrary")),
        name="mod",
    )(cond, ada_w, ada_b.reshape(depth, parts, 1, d))
    return out.reshape(depth, parts, rows, 1, d)


@functools.lru_cache(maxsize=None)
def _dft_consts(seq):
    f = np.arange(seq)[:, None]
    t = np.arange(seq)[None, :]
    ang = np.pi * ((f * t) % (2 * seq)) / seq
    cos = np.cos(ang)
    sin = np.sin(ang)
    alt = np.where(np.arange(seq) % 2 == 0, 1.0, -1.0)
    fwd_top = cos
    fwd_bot = -sin
    fwd_bot[0] = alt
    wgt = np.full((seq, 1), 2.0)
    wgt[0] = 1.0
    inv_top = (cos * wgt).T / (2 * seq)
    inv_bot = (-2.0 * sin).T / (2 * seq)
    inv_bot[:, 0] = alt / (2 * seq)
    fwd = np.concatenate([fwd_top, fwd_bot], axis=0).astype(np.float32)
    inv = np.concatenate([inv_top, inv_bot], axis=1).astype(np.float32)
    return fwd, inv, alt.astype(np.float32)[:, None]


@functools.lru_cache(maxsize=None)
def _filter_consts(seq, d):
    t = np.linspace(0.0, 1.0, seq)[:, None]
    w = 2.0 * np.pi * np.arange(seq)[:, None] / seq
    nb = (_HY_EMB - 1) // 2
    bands = np.linspace(1e-4, nb - 1, nb)[None, :]
    z = np.concatenate([t, np.cos(bands * w), -np.sin(bands * w)], axis=-1)
    zpad = np.zeros((seq, _HY_FO))
    zpad[:, :_HY_EMB] = z
    max_decay = math.log(_HY_TARGET) / _HY_FAST
    min_decay = math.log(_HY_TARGET) / _HY_SLOW
    deltas = np.abs(np.linspace(min_decay, max_decay, d))[None, :]
    return zpad.astype(np.float32), deltas.astype(np.float32)


@functools.lru_cache(maxsize=None)
def _pos_embed(rows, d):
    quarter = d // 4
    omega = 1.0 / (_POS_BASE ** (np.arange(quarter, dtype=np.float64) / quarter))

    def axis_embed(n):
        ang = np.arange(n, dtype=np.float64)[:, None] * omega[None]
        return np.concatenate([np.sin(ang), np.cos(ang)], axis=-1)

    er = np.broadcast_to(axis_embed(rows)[:, None], (rows, _GRID_W, d // 2))
    ec = np.broadcast_to(axis_embed(_GRID_W)[None], (rows, _GRID_W, d // 2))
    return np.concatenate([er, ec], axis=-1).reshape(rows * _GRID_W, d).astype(np.float32)


@functools.lru_cache(maxsize=None)
def _s5_masks():
    step = np.repeat(np.arange(_S5_CHUNK), _S5_H)
    causal = (step[None, :] >= step[:, None]).astype(np.float32)
    anti = (step[:, None] >= step[None, :]).astype(np.float32)
    return causal, anti


def _filter_kernel(z_ref, w1_ref, b1_ref, w2_ref, b2_ref, fr_ref, w3f_ref, w3b_ref, dl_ref,
                   at_ref, ab_ref, alt_ref, krt_ref, krb_ref, ki_ref):
    z = z_ref[...]
    fr = fr_ref[...]
    h = jnp.sin(fr * (_dot_3pass(z, w1_ref[...]) + b1_ref[...]))
    h = jnp.sin(fr * (_dot_3pass(h, w2_ref[...]) + b2_ref[...]))
    decay = jnp.exp(-z[:, 0:1] * dl_ref[...])
    hf = _dot_3pass(h, w3f_ref[...]) * decay
    hb = _dot_3pass(h, w3b_ref[...]) * decay
    norm = (jnp.sum(jnp.abs(hf), axis=0, keepdims=True)
            + jnp.sum(jnp.abs(hb), axis=0, keepdims=True) + _EPS)
    hf = hf / norm
    hb = hb / norm
    first = lax.broadcasted_iota(jnp.int32, hf.shape, 0) == 0
    hb = jnp.where(first, 0.0, hb)
    ksum = hf + hb
    kdiff = hf - hb
    kre = jnp.dot(at_ref[...], ksum.astype(_BF16), preferred_element_type=_F32)
    kim = jnp.dot(ab_ref[...], kdiff.astype(_BF16), preferred_element_type=_F32)
    nyq = jnp.sum(alt_ref[...] * ksum, axis=0, keepdims=True)
    krt_ref[...] = kre
    krb_ref[...] = jnp.where(first, nyq, kre)
    ki_ref[...] = jnp.where(first, 0.0, kim)


def _hyena_filters(seq, d, w1p, b1, w2, b2, freq, w3):
    zpad, deltas = _filter_consts(seq, d)
    fwd, _, alt = _dft_consts(seq)
    fwd = jnp.asarray(fwd).astype(_BF16)
    top = pl.BlockSpec((seq, seq), lambda o, j: (0, 0), pipeline_mode=pl.Buffered(1))
    bot = pl.BlockSpec((seq, seq), lambda o, j: (1, 0), pipeline_mode=pl.Buffered(1))
    nb = d // _FILT_DBLK
    fo = _HY_FO
    out_sds = jax.ShapeDtypeStruct((_HY_ORDER, seq, d), _F32)
    out_spec = pl.BlockSpec((None, seq, _FILT_DBLK), lambda o, j: (o, 0, j))
    return pl.pallas_call(
        _filter_kernel,
        grid=(_HY_ORDER, nb),
        in_specs=[
            _const_spec((seq, fo)), _const_spec((fo, fo)), _const_spec((1, fo)),
            _const_spec((fo, fo)), _const_spec((1, fo)), _const_spec((1, fo)),
            pl.BlockSpec((fo, _FILT_DBLK), lambda o, j: (0, (2 * o) * nb + j)),
            pl.BlockSpec((fo, _FILT_DBLK), lambda o, j: (0, (2 * o + 1) * nb + j)),
            pl.BlockSpec((1, _FILT_DBLK), lambda o, j: (0, j)),
            top, bot,
            _const_spec((seq, 1)),
        ],
        out_specs=[out_spec, out_spec, out_spec],
        out_shape=[out_sds, out_sds, out_sds],
        compiler_params=_params(("arbitrary", "arbitrary")),
        name=f"filt{seq}",
    )(jnp.asarray(zpad), w1p, b1, w2, b2, freq, w3, w3, jnp.asarray(deltas),
      fwd, fwd, jnp.asarray(alt))


def _pre_kernel(has_pos, *refs):
    if has_pos:
        x_ref, pos_ref, sh_ref, sc_ref, g_ref, w_ref, b_ref, z_ref = refs
        x = x_ref[...] + pos_ref[...]
    else:
        x_ref, sh_ref, sc_ref, g_ref, w_ref, b_ref, z_ref = refs
        x = x_ref[...]
    h = _rmsnorm(x, g_ref[...]) * (1.0 + sc_ref[...]) + sh_ref[...]
    z_ref[...] = (_dot_bf16(h, w_ref[...]) + b_ref[...]).astype(z_ref.dtype)


def _pre_hyena(x, pos, mod5, row_of_b, layer, norm_g, w_in, b_in):
    bsz, seq, d = x.shape
    n = w_in.shape[-1]
    tm = _TOKEN_TILE
    tok = pl.BlockSpec((None, tm, d), lambda t, b: (b, t, 0))
    ins, specs = [x], [tok]
    if pos is not None:
        ins.append(pos)
        specs.append(pl.BlockSpec((tm, d), lambda t, b: (t, 0)))
    ins += [mod5, mod5, norm_g, w_in, b_in]
    specs += [_mod_spec(d, layer, 0, row_of_b), _mod_spec(d, layer, 1, row_of_b),
              _layer_spec((1, d), layer), _layer_spec((d, n), 0), _layer_spec((1, n), 0)]
    return pl.pallas_call(
        functools.partial(_pre_kernel, pos is not None),
        grid=(seq // tm, bsz),
        in_specs=specs,
        out_specs=pl.BlockSpec((None, tm, n), lambda t, b: (b, t, 0)),
        out_shape=jax.ShapeDtypeStruct((bsz, seq, n), _BF16),
        compiler_params=_params(("arbitrary", "arbitrary")),
        name=f"pre{seq}",
    )(*ins)


def _conv_kernel(zv_ref, z1_ref, z2_ref, wv_ref, w1_ref, w2_ref, bv_ref, b1_ref, b2_ref, fb_ref,
                 krt_ref, krb_ref, ki_ref, ft_ref, fb2_ref, it_ref, ib_ref, o_ref):
    seq, dblk = zv_ref.shape
    row = lax.broadcasted_iota(jnp.int32, (seq, dblk), 0)
    first = row == 0
    last = row == seq - 1

    def short_conv(z_ref, w_ref, b_ref):
        z = z_ref[...].astype(_F32)
        prev = jnp.where(first, 0.0, pltpu.roll(z, 1, 0))
        nxt = jnp.where(last, 0.0, pltpu.roll(z, seq - 1, 0))
        w = w_ref[...]
        return prev * w[0:1] + z * w[1:2] + nxt * w[2:3] + b_ref[...]

    v = short_conv(zv_ref, wv_ref, bv_ref)
    gates = (short_conv(z1_ref, w1_ref, b1_ref), short_conv(z2_ref, w2_ref, b2_ref))
    for o in range(_HY_ORDER):
        vb = v.astype(_BF16)
        p = jnp.dot(ft_ref[...], vb, preferred_element_type=_F32)
        q = jnp.dot(fb2_ref[...], vb, preferred_element_type=_F32)
        ki = ki_ref[o]
        ytop = p * krt_ref[o] - q * ki
        ybot = p * ki + q * krb_ref[o]
        y = (jnp.dot(it_ref[...], ytop.astype(_BF16), preferred_element_type=_F32)
             + jnp.dot(ib_ref[...], ybot.astype(_BF16), preferred_element_type=_F32))
        v = (y + v * fb_ref[o:o + 1, :]) * gates[o]
    o_ref[...] = v.astype(o_ref.dtype)


def _hyena_conv(z, conv_w, conv_b, fbias, krt, krb, ki):
    bsz, seq, d3 = z.shape
    d = d3 // 3
    dblk = _CONV_DBLK
    nb = d // dblk
    fwd, inv, _ = _dft_consts(seq)
    fwd = jnp.asarray(fwd).astype(_BF16)
    inv = jnp.asarray(inv).astype(_BF16)

    def mspec(r, c):
        return pl.BlockSpec((seq, seq), lambda j, b: (r, c), pipeline_mode=pl.Buffered(1))

    def zspec(k):
        return pl.BlockSpec((None, seq, dblk), lambda j, b: (b, 0, k * nb + j))

    def wspec(rows, k):
        return pl.BlockSpec((rows, dblk), lambda j, b: (0, k * nb + j))

    kspec = pl.BlockSpec((_HY_ORDER, seq, dblk), lambda j, b: (0, 0, j), pipeline_mode=pl.Buffered(1))
    return pl.pallas_call(
        _conv_kernel,
        grid=(nb, bsz),
        in_specs=[zspec(0), zspec(1), zspec(2),
                  wspec(3, 0), wspec(3, 1), wspec(3, 2),
                  wspec(1, 0), wspec(1, 1), wspec(1, 2),
                  wspec(_HY_ORDER, 0),
                  kspec, kspec, kspec,
                  mspec(0, 0), mspec(1, 0), mspec(0, 0), mspec(0, 1)],
        out_specs=pl.BlockSpec((None, seq, dblk), lambda j, b: (b, 0, j)),
        out_shape=jax.ShapeDtypeStruct((bsz, seq, d), _BF16),
        compiler_params=_params(("arbitrary", "arbitrary")),
        name=f"conv{seq}",
    )(z, z, z, conv_w, conv_w, conv_w, conv_b, conv_b, conv_b, fbias, krt, krb, ki,
      fwd, fwd, inv, inv)


def _post_kernel(kind, tail, names, *refs):
    r = dict(zip(names, refs))
    tm, d = r["x"].shape
    steps = _S5_CHUNK
    nsub = max(tm // _POST_SUB, 1)
    sub = tm // nsub

    def stages(i):
        rows = slice(i * sub, (i + 1) * sub)
        crows = slice(i * sub // steps, (i + 1) * sub // steps)
        st = {}

        def head():
            st["x"] = r["x"][rows, :]
            if "pos" in r:
                st["x"] = st["x"] + r["pos"][rows, :]
            if kind == "hyena":
                st["mix"] = r["mix"][rows, :]
            else:
                u = _rmsnorm(st["x"], r["n1"][...]) * (1.0 + r["sc1"][...]) + r["sh1"][...]
                mix = jnp.swapaxes(r["mix"][:, crows, :], 0, 1).reshape(sub, d)
                st["mix"] = _gelu_tanh(mix + u * r["skip"][...]).astype(_BF16)

        def mixer_out():
            st["m"] = _dot_bf16(st.pop("mix"), r["w_out"][...]) + r["b_out"][...]

        def mid():
            m = st.pop("m")
            if kind == "s5":
                half = m.shape[-1] // 2
                m = m[:, :half] * jax.nn.sigmoid(m[:, half:])
            st["x"] = st["x"] + r["g1"][...] * m
            h = _rmsnorm(st["x"], r["n2"][...]) * (1.0 + r["sc2"][...]) + r["sh2"][...]
            st["h"] = h.astype(_BF16)

        def ffn_in():
            st["ab"] = _dot_bf16(st.pop("h"), r["w13"][...])

        def act():
            ab = st.pop("ab")
            ff = ab.shape[-1] // 2
            st["act"] = (_silu(ab[:, :ff]) * ab[:, ff:]).astype(_BF16)

        def ffn_out():
            st["f"] = _dot_bf16(st.pop("act"), r["w2"][...])

        def finish():
            x = st.pop("x") + r["g2"][...] * st.pop("f")
            if tail == "next":
                r["o_x"][rows, :] = x
                un = _rmsnorm(x, r["nn"][...]) * (1.0 + r["scn"][...]) + r["shn"][...]
                r["o_u"][:, crows, :] = jnp.swapaxes(un.reshape(-1, steps, d), 0, 1)
            else:
                r["o_x"][rows, :] = _rmsnorm(x, r["nf"][...])

        return [head, mixer_out, mid, ffn_in, act, ffn_out, finish]

    pipes = [stages(i) for i in range(nsub)]
    depth = len(pipes[0])
    for t in range(depth + nsub - 1):
        for i, pipe in enumerate(pipes):
            if 0 <= t - i < depth:
                pipe[t - i]()


def _post_mixer(kind, tail, x, pos, mix, mod5, row_of_b, layer, w_out, b_out, norm1_g, skip,
                norm2_g, w13, w2, final_g):
    bsz, seq, d = x.shape
    tm = min(_POST_TILE, seq)
    tok = pl.BlockSpec((None, tm, d), lambda t, b: (b, t, 0))
    nt = seq // tm
    steps = _S5_CHUNK
    stepmajor = pl.BlockSpec((steps, tm // steps, d), lambda t, b: (0, b * nt + t, 0))
    stepmajor_sds = jax.ShapeDtypeStruct((steps, bsz * seq // steps, d), _F32)
    n_out = w_out.shape[-1]
    f2 = w13.shape[-1]

    def mod(part, lyr=layer):
        return (mod5, _mod_spec(d, lyr, part, row_of_b))

    items = [("x", x, tok), ("mix", mix, stepmajor if kind == "s5" else tok),
             ("w_out", w_out, _layer_spec((d, n_out), 0)), ("b_out", b_out, _layer_spec((1, n_out), 0)),
             ("g1",) + mod(2), ("sh2",) + mod(3), ("sc2",) + mod(4), ("g2",) + mod(5),
             ("n2", norm2_g, _layer_spec((1, d), layer)),
             ("w13", w13, _layer_spec((d, f2), layer)), ("w2", w2, _layer_spec((f2 // 2, d), layer))]
    if kind == "s5":
        items += [("n1", norm1_g, _layer_spec((1, d), layer)), ("sh1",) + mod(0), ("sc1",) + mod(1),
                  ("skip", skip, _layer_spec((1, d), 0))]
    if pos is not None:
        items.append(("pos", pos, pl.BlockSpec((tm, d), lambda t, b: (t, 0))))
    out_specs, out_shape = [tok], [jax.ShapeDtypeStruct(x.shape, _F32)]
    names_out = ["o_x"]
    if tail == "next":
        items += [("nn", norm1_g, _layer_spec((1, d), layer + 1)),
                  ("shn",) + mod(0, layer + 1), ("scn",) + mod(1, layer + 1)]
        out_specs.append(stepmajor)
        out_shape.append(stepmajor_sds)
        names_out.append("o_u")
    else:
        items.append(("nf", final_g, _const_spec((1, d))))
    names = tuple(i[0] for i in items) + tuple(names_out)
    out = pl.pallas_call(
        functools.partial(_post_kernel, kind, tail, names),
        grid=(seq // tm, bsz),
        in_specs=[i[2] for i in items], out_specs=out_specs, out_shape=out_shape,
        compiler_params=_params(("arbitrary", "arbitrary")),
        name=f"post_{kind}{seq}",
    )(*[i[1] for i in items])
    return out


_S5_GB = 8
_S5_LANES = 2 * _S5_P
_S5_MATS_GB = 8


def _cmul(ar, ai, br, bi):
    return ar * br - ai * bi, ar * bi + ai * br


def _s5_mats_kernel(prm_ref, btr_ref, bti_ref, cr_ref, ci_ref, causal_ref, anti_ref,
                    m_ref, w_ref, vf_ref, vb_ref, a_ref):
    c, h, lanes = _S5_CHUNK, _S5_H, _S5_LANES
    a_re = prm_ref[0:1, :]
    a_im = prm_ref[1:2, :]
    dt = jnp.exp(prm_ref[2:3, :])
    sr = dt * a_re
    ph = dt * a_im
    mag = jnp.exp(sr)
    nr = mag * jnp.cos(ph) - 1.0
    ni = mag * jnp.sin(ph)
    den = a_re * a_re + a_im * a_im
    co_re = (nr * a_re + ni * a_im) / den
    co_im = (ni * a_re - nr * a_im) / den
    bt = _cmul(co_re, co_im, btr_ref[...], bti_ref[...])
    ct = (cr_ref[...], ci_ref[...])

    def powers(k):
        kf = k.astype(_F32)
        e = jnp.exp(kf * sr)
        return e * jnp.cos(kf * ph), e * jnp.sin(kf * ph)

    def per_step(x):
        return jnp.broadcast_to(x[:, None, :], (c, h, lanes)).reshape(c * h, lanes)

    def per_chan(x):
        return jnp.broadcast_to(x[None, :, :], (c, h, lanes)).reshape(c * h, lanes)

    step = lax.broadcasted_iota(jnp.int32, (c, lanes), 0)
    fwd = lax.broadcasted_iota(jnp.int32, (c, lanes), 1) < _S5_P
    tb = tuple(per_chan(x) for x in bt)
    tc = tuple(per_chan(x) for x in ct)

    def table(base, k_fwd, k_bwd):
        return _cmul(*base, *(per_step(x) for x in powers(jnp.where(fwd, k_fwd, k_bwd))))

    lmat = table(tb, -step, step)
    rmat = table(tc, step, -step)
    wmat = table(tb, c - 1 - step, step)
    vmat = table(tc, step + 1, c - step)

    def nt3(x, y):
        xh, xl = _split_bf16(x)
        yh, yl = _split_bf16(y)
        return _dot_nt(xh, yh) + _dot_nt(xl, yh) + _dot_nt(xh, yl)

    fwd_rows = lax.broadcasted_iota(jnp.int32, (c * h, lanes), 1) < _S5_P

    def kernel_half(keep):
        return (nt3(jnp.where(keep, lmat[0], 0.0), rmat[0])
                - nt3(jnp.where(keep, lmat[1], 0.0), rmat[1]))

    m = causal_ref[...] * kernel_half(fwd_rows) + anti_ref[...] * kernel_half(jnp.logical_not(fwd_rows))
    m_ref[...] = m.astype(m_ref.dtype)
    w_ref[...] = jnp.concatenate([wmat[0], wmat[1]], axis=1).astype(w_ref.dtype)
    vcat = jnp.concatenate([vmat[0], -vmat[1]], axis=1)
    keep_f = (lax.broadcasted_iota(jnp.int32, vcat.shape, 1) % lanes) < _S5_P
    vf_ref[...] = jnp.where(keep_f, vcat, 0.0).T.astype(vf_ref.dtype)
    vb_ref[...] = jnp.where(keep_f, 0.0, vcat).T.astype(vb_ref.dtype)
    chunk_k = jnp.full((1, lanes), c, jnp.int32)
    a_chunk = powers(chunk_k)
    a_ref[0:1, :] = a_chunk[0]
    a_ref[1:2, :] = a_chunk[1]


def _s5_mats_block_kernel(prm_ref, btr_ref, bti_ref, cr_ref, ci_ref, causal_ref, anti_ref,
                          m_ref, w_ref, vf_ref, vb_ref, a_ref):
    for j in range(prm_ref.shape[0]):
        _s5_mats_kernel(prm_ref.at[j], btr_ref.at[j], bti_ref.at[j], cr_ref.at[j], ci_ref.at[j],
                        causal_ref, anti_ref,
                        m_ref.at[j], w_ref.at[j], vf_ref.at[j], vb_ref.at[j], a_ref.at[j])


def _s5_mats(prm, bt_re, bt_im, c_re, c_im):
    g = prm.shape[0]
    cw, lanes = _CHUNK_W, _S5_LANES
    causal, anti = (jnp.asarray(a) for a in _s5_masks())
    gm = _S5_MATS_GB
    gspec = pl.BlockSpec((gm, _S5_H, lanes), lambda i: (i, 0, 0))
    mat_spec = pl.BlockSpec((gm, cw, cw), lambda i: (i, 0, 0))
    mat_sds = jax.ShapeDtypeStruct((g, cw, cw), _BF16)
    return pl.pallas_call(
        _s5_mats_block_kernel,
        grid=(g // gm,),
        in_specs=[pl.BlockSpec((gm, 3, lanes), lambda i: (i, 0, 0)), gspec, gspec, gspec, gspec,
                  _const_spec((cw, cw)), _const_spec((cw, cw))],
        out_specs=[mat_spec] * 4 + [pl.BlockSpec((gm, 2, lanes), lambda i: (i, 0, 0))],
        out_shape=[mat_sds] * 4 + [jax.ShapeDtypeStruct((g, 2, lanes), _F32)],
        compiler_params=_params(("arbitrary",)),
        name="s5_mats",
    )(prm, bt_re, bt_im, c_re, c_im, causal, anti)


def _segment_transpose(xs):
    n = len(xs)
    seg = lax.broadcasted_iota(jnp.int32, xs[0].shape, 1) // _S5_H
    diags = []
    for k in range(n):
        z = xs[-k % n]
        for m in range(1, n):
            z = jnp.where(seg == m, xs[(m - k) % n], z)
        diags.append(pltpu.roll(z, ((n - k) % n) * _S5_H, 1) if k else z)
    ys = []
    for b in range(n):
        y = diags[b]
        for a in range(1, n):
            y = jnp.where(seg == a, diags[(b - a) % n], y)
        ys.append(y)
    return ys


def _s5_scan_kernel(nchunk, bsz, has_init, *refs):
    if has_init:
        u_ref, m_ref, w_ref, vf_ref, vb_ref, a_ref, h0_ref, y_ref = refs[:8]
    else:
        u_ref, m_ref, w_ref, vf_ref, vb_ref, a_ref, y_ref, fin_ref = refs[:8]
    s_ref, pf_ref, pb_ref, uall_ref, yall_ref = refs[8:]
    lanes, gb = _S5_LANES, _S5_GB
    rows = nchunk * bsz
    fwd = lax.broadcasted_iota(jnp.int32, (bsz, lanes), 1) < _S5_P

    for half in range(2):
        per_group = _segment_transpose([u_ref[half * 8 + s8] for s8 in range(8)])
        for gl in range(gb):
            uall_ref[gl, :, half * 128:(half + 1) * 128] = per_group[gl].astype(_BF16)

    for gl in range(gb):
        u = uall_ref[gl]
        yall_ref[gl] = jnp.dot(u, m_ref[gl], preferred_element_type=_F32)
        s = jnp.dot(u, w_ref[gl], preferred_element_type=_F32)
        s_ref[...] = jnp.swapaxes(s.reshape(bsz, nchunk, 2 * lanes), 0, 1).reshape(rows, 2 * lanes)

        a_re = a_ref[gl, 0:1, :]
        a_im = a_ref[gl, 1:2, :]
        if has_init:
            init = (h0_ref[gl, :, 0:lanes], h0_ref[gl, :, lanes:2 * lanes])
        else:
            init = (jnp.zeros((bsz, lanes), _F32), jnp.zeros((bsz, lanes), _F32))

        def step(k, carry):
            h_re, h_im = carry
            rowf = pl.ds(pl.multiple_of(k * bsz, bsz), bsz)
            rowb = pl.ds(pl.multiple_of((nchunk - 1 - k) * bsz, bsz), bsz)
            pf_ref[rowf, 0:lanes] = h_re
            pf_ref[rowf, lanes:2 * lanes] = h_im
            pb_ref[rowb, 0:lanes] = h_re
            pb_ref[rowb, lanes:2 * lanes] = h_im
            s_re = jnp.where(fwd, s_ref[rowf, 0:lanes], s_ref[rowb, 0:lanes])
            s_im = jnp.where(fwd, s_ref[rowf, lanes:2 * lanes], s_ref[rowb, lanes:2 * lanes])
            return (a_re * h_re - a_im * h_im + s_re, a_re * h_im + a_im * h_re + s_im)

        fin = lax.fori_loop(0, nchunk, step, init)
        if not has_init:
            fin_ref[gl, :, 0:lanes] = fin[0]
            fin_ref[gl, :, lanes:2 * lanes] = fin[1]
        carried = (jnp.dot(pf_ref[...].astype(_BF16), vf_ref[gl], preferred_element_type=_F32)
                   + jnp.dot(pb_ref[...].astype(_BF16), vb_ref[gl], preferred_element_type=_F32))
        carried = jnp.swapaxes(carried.reshape(nchunk, bsz, 2 * lanes), 0, 1).reshape(rows, 2 * lanes)
        yall_ref[gl] = yall_ref[gl] + carried

    for half in range(2):
        per_step = _segment_transpose([yall_ref[gl, :, half * 128:(half + 1) * 128] for gl in range(gb)])
        for t8 in range(8):
            y_ref[half * 8 + t8] = per_step[t8]


def _s5_scan(u, bsz, mats, init):
    steps, rows, d = u.shape
    g = d // _S5_H
    nchunk = rows // bsz
    cw, gb = _CHUNK_W, _S5_GB
    lanes = gb * _S5_H
    tok_spec = pl.BlockSpec((steps, rows, lanes), lambda i: (0, 0, i))
    mat_spec = pl.BlockSpec((gb, cw, cw), lambda i: (i, 0, 0))
    st_spec = pl.BlockSpec((gb, bsz, cw), lambda i: (i, 0, 0))
    ins = [u] + list(mats)
    specs = [tok_spec] + [mat_spec] * 4 + [pl.BlockSpec((gb, 2, _S5_LANES), lambda i: (i, 0, 0))]
    y_sds = jax.ShapeDtypeStruct(u.shape, _F32)
    if init is not None:
        ins.append(init)
        specs.append(st_spec)
        out_specs, out_shape = tok_spec, y_sds
    else:
        out_specs = [tok_spec, st_spec]
        out_shape = [y_sds, jax.ShapeDtypeStruct((g, bsz, cw), _F32)]
    out = pl.pallas_call(
        functools.partial(_s5_scan_kernel, nchunk, bsz, init is not None),
        grid=(g // gb,),
        in_specs=specs, out_specs=out_specs, out_shape=out_shape,
        scratch_shapes=[pltpu.VMEM((rows, cw), _F32)] * 3
        + [pltpu.VMEM((gb, rows, cw), _BF16), pltpu.VMEM((gb, rows, cw), _F32)],
        compiler_params=_params(("arbitrary",)),
        name=f"s5_{nchunk * steps}",
    )(*ins)
    if init is not None:
        return out, None
    return out[0], out[1]


def _trunk(x, pos, row_of_b, shared_mod, init_state, mod5, filt, s5_mats, wts):
    bsz, seq, d = x.shape
    fold = _POST_TILE // seq if (shared_mod and seq < _POST_TILE) else 1
    rows = lambda a: a.reshape(bsz // fold, seq * fold, a.shape[-1])
    seqs = lambda a: a.reshape(bsz, seq, a.shape[-1])
    z = _pre_hyena(rows(x), pos, mod5, row_of_b, 0, wts["norm1_g"], wts["hy_in_w"], wts["hy_in_b"])
    v = _hyena_conv(seqs(z), wts["hy_conv_w"], wts["hy_conv_b"], wts["hy_fbias"], *filt)
    x1, u = _post_mixer("hyena", "next", rows(x), pos, rows(v), mod5, row_of_b, 0, wts["hy_out_w"],
                        wts["hy_out_b"], wts["norm1_g"], None, wts["norm2_g"], wts["ffn_w13"], wts["ffn_w2"],
                        None)
    ys, fin = _s5_scan(u, bsz, s5_mats, init_state)
    (out,) = _post_mixer("s5", "final", x1, None, ys, mod5, row_of_b, 1, wts["s5_glu_w"], wts["s5_glu_b"],
                         wts["norm1_g"], wts["s5_D"], wts["norm2_g"], wts["ffn_w13"], wts["ffn_w2"],
                         wts["final_g"])
    return seqs(out), fin


def kernel(x_prompt, x_sample, state_s5, c, c_ctx, norm1_g, norm2_g, final_g, ada_w, ada_b, ffn_w13, ffn_w2, hy_in_w, hy_in_b, hy_conv_w, hy_conv_b, hy_pe_w1, hy_pe_b1, hy_pe_w2, hy_pe_b2, hy_pe_w3, hy_freq, hy_fbias, hy_out_w, hy_out_b, s5_A_re, s5_A_im, s5_log_dt, s5_B_re, s5_B_im, s5_C_re, s5_C_im, s5_D, s5_glu_w, s5_glu_b):
    depth, d = norm1_g.shape
    assert depth == 2 and hy_in_w.shape[0] == 1 and s5_glu_w.shape[0] == 1
    dec_b, dec_seq, _ = x_sample.shape
    g = d // _S5_H
    p = _S5_P

    nrow = -(-(1 + dec_b) // 8) * 8
    cond = jnp.concatenate([c_ctx[None], c, jnp.zeros((nrow - 1 - dec_b, d), _F32)], axis=0)
    mod5 = _modulation(cond, ada_w, ada_b)

    row3 = lambda a: a.reshape(a.shape[0], 1, a.shape[-1])
    wts = dict(
        norm1_g=row3(norm1_g), norm2_g=row3(norm2_g), final_g=final_g[None],
        hy_in_w=hy_in_w.astype(_BF16), hy_in_b=row3(hy_in_b),
        hy_conv_w=hy_conv_w[0], hy_conv_b=hy_conv_b, hy_fbias=hy_fbias[0],
        hy_out_w=hy_out_w.astype(_BF16), hy_out_b=row3(hy_out_b),
        ffn_w13=ffn_w13.astype(_BF16), ffn_w2=ffn_w2.astype(_BF16),
        s5_glu_w=s5_glu_w.astype(_BF16), s5_glu_b=row3(s5_glu_b), s5_D=row3(s5_D),
    )
    dirs_on_lanes = lambda a: a.transpose(1, 0, 2).reshape(g, 2 * p)
    ldt = jnp.broadcast_to(s5_log_dt[0][:, :, None], (2, g, p))
    prm = jnp.stack([dirs_on_lanes(s5_A_re[0]), dirs_on_lanes(s5_A_im[0]), dirs_on_lanes(ldt)], axis=1)
    bt = lambda a: a.transpose(1, 3, 0, 2).reshape(g, _S5_H, 2 * p)
    ct = lambda a: a.transpose(1, 2, 0, 3).reshape(g, _S5_H, 2 * p)
    s5_mats = _s5_mats(prm, bt(s5_B_re[0]), bt(s5_B_im[0]), ct(s5_C_re[0]), ct(s5_C_im[0]))

    w1p = jnp.pad(hy_pe_w1[0], ((0, _HY_FO - _HY_EMB), (0, 0)))
    filt_args = (w1p, hy_pe_b1, hy_pe_w2[0], hy_pe_b2, hy_freq, hy_pe_w3[0])
    filt_ctx = _hyena_filters(x_prompt.shape[1], d, *filt_args)
    filt_lat = _hyena_filters(dec_seq, d, *filt_args)

    y_prompt, fin = _trunk(x_prompt, None, lambda b: 0, True, None, mod5, filt_ctx, s5_mats, wts)
    new_state = fin.reshape(g, -1, 2, 2, p).transpose(1, 3, 2, 0, 4)[:, None]

    init = state_s5[:, 0].transpose(3, 0, 2, 1, 4).reshape(g, dec_b, 4 * p)
    pos = jnp.asarray(_pos_embed(dec_seq // _GRID_W, d))
    y_sample, _ = _trunk(x_sample, pos, lambda b: b + 1, False, init, mod5, filt_lat, s5_mats, wts)
    return (y_prompt, y_sample, new_state)
```

```python
import functools
import math

import numpy as np
import jax
import jax.numpy as jnp
from jax import lax
from jax.experimental import pallas as pl
from jax.experimental.pallas import tpu as pltpu

_F32 = jnp.float32
_BF16 = jnp.bfloat16

_EPS = 1e-6
_GRID_W = 64
_POS_BASE = 10000.0
_HY_ORDER = 2
_HY_EMB = 33
_HY_FO = 64
_HY_TARGET = 1e-2
_HY_FAST = 0.3
_HY_SLOW = 1.5
_S5_H = 16
_S5_P = 64
_S5_CHUNK = 16
_CHUNK_W = _S5_CHUNK * _S5_H

_VMEM_LIMIT = 56 * 1024 * 1024
_TOKEN_TILE = 512
_POST_TILE = 512
_POST_SUB = 256
_CONV_DBLK = 512
_FILT_DBLK = 512
_MOD_PARTS = 2


def _params(sem):
    return pltpu.CompilerParams(dimension_semantics=sem, vmem_limit_bytes=_VMEM_LIMIT)


def _dot_bf16(a, b):
    return jnp.dot(a.astype(_BF16), b.astype(_BF16), preferred_element_type=_F32)


def _split_bf16(x):
    hi = x.astype(_BF16)
    return hi, (x - hi.astype(_F32)).astype(_BF16)


def _dot_3pass(a, b):
    ah, al = _split_bf16(a)
    bh, bl = _split_bf16(b)
    return (jnp.dot(ah, bh, preferred_element_type=_F32) + jnp.dot(al, bh, preferred_element_type=_F32)
            + jnp.dot(ah, bl, preferred_element_type=_F32))


def _dot_nt(a, b, precision=None):
    return lax.dot_general(a, b, (((1,), (1,)), ((), ())), precision=precision,
                           preferred_element_type=_F32)


def _rmsnorm(x, g):
    ms = jnp.mean(x * x, axis=-1, keepdims=True)
    return x * lax.rsqrt(ms + _EPS) * g


def _silu(x):
    return x * jax.nn.sigmoid(x)


def _gelu_tanh(x):
    c = math.sqrt(2.0 / math.pi)
    return 0.5 * x * (1.0 + jnp.tanh(c * (x + 0.044715 * (x * x * x))))


def _const_spec(shape):
    nd = len(shape)
    return pl.BlockSpec(shape, lambda *_: (0,) * nd, pipeline_mode=pl.Buffered(1))


def _layer_spec(shape, layer):
    nd = len(shape)
    return pl.BlockSpec((None,) + tuple(shape), lambda *_: (layer,) + (0,) * nd,
                        pipeline_mode=pl.Buffered(1))


def _mod_spec(d, layer, part, row_of_b):
    return pl.BlockSpec((None, None, None, 1, d), lambda t, b: (layer, part, row_of_b(b), 0, 0))


def _mod_kernel(cond_ref, w_ref, b_ref, o_ref):
    d = o_ref.shape[-1]
    r = _dot_3pass(_silu(cond_ref[...]), w_ref[...])
    for p in range(o_ref.shape[0]):
        o_ref[p] = r[:, p * d:(p + 1) * d] + b_ref[p]


def _modulation(cond, ada_w, ada_b):
    depth, d, d6 = ada_w.shape
    parts = d6 // d
    rows = cond.shape[0]
    pb = _MOD_PARTS
    out = pl.pallas_call(
        _mod_kernel,
        grid=(depth, parts // pb),
        in_specs=[
            pl.BlockSpec((rows, d), lambda l, p: (0, 0)),
            pl.BlockSpec((None, d, pb * d), lambda l, p: (l, 0, p)),
            pl.BlockSpec((None, pb, 1, d), lambda l, p: (l, p, 0, 0)),
        ],
        out_specs=pl.BlockSpec((None, pb, rows, d), lambda l, p: (l, p, 0, 0)),
        out_shape=jax.ShapeDtypeStruct((depth, parts, rows, d), _F32),
        compiler_params=_params(("arbitrary", "arbitrary")),
        name="mod",
    )(cond, ada_w, ada_b.reshape(depth, parts, 1, d))
    return out.reshape(depth, parts, rows, 1, d)


@functools.lru_cache(maxsize=None)
def _dft_consts(seq):
    f = np.arange(seq)[:, None]
    t = np.arange(seq)[None, :]
    ang = np.pi * ((f * t) % (2 * seq)) / seq
    cos = np.cos(ang)
    sin = np.sin(ang)
    alt = np.where(np.arange(seq) % 2 == 0, 1.0, -1.0)
    fwd_top = cos
    fwd_bot = -sin
    fwd_bot[0] = alt
    wgt = np.full((seq, 1), 2.0)
    wgt[0] = 1.0
    inv_top = (cos * wgt).T / (2 * seq)
    inv_bot = (-2.0 * sin).T / (2 * seq)
    inv_bot[:, 0] = alt / (2 * seq)
    fwd = np.concatenate([fwd_top, fwd_bot], axis=0).astype(np.float32)
    inv = np.concatenate([inv_top, inv_bot], axis=1).astype(np.float32)
    return fwd, inv, alt.astype(np.float32)[:, None]


@functools.lru_cache(maxsize=None)
def _filter_consts(seq, d):
    t = np.linspace(0.0, 1.0, seq)[:, None]
    w = 2.0 * np.pi * np.arange(seq)[:, None] / seq
    nb = (_HY_EMB - 1) // 2
    bands = np.linspace(1e-4, nb - 1, nb)[None, :]
    z = np.concatenate([t, np.cos(bands * w), -np.sin(bands * w)], axis=-1)
    zpad = np.zeros((seq, _HY_FO))
    zpad[:, :_HY_EMB] = z
    max_decay = math.log(_HY_TARGET) / _HY_FAST
    min_decay = math.log(_HY_TARGET) / _HY_SLOW
    deltas = np.abs(np.linspace(min_decay, max_decay, d))[None, :]
    return zpad.astype(np.float32), deltas.astype(np.float32)


@functools.lru_cache(maxsize=None)
def _pos_embed(rows, d):
    quarter = d // 4
    omega = 1.0 / (_POS_BASE ** (np.arange(quarter, dtype=np.float64) / quarter))

    def axis_embed(n):
        ang = np.arange(n, dtype=np.float64)[:, None] * omega[None]
        return np.concatenate([np.sin(ang), np.cos(ang)], axis=-1)

    er = np.broadcast_to(axis_embed(rows)[:, None], (rows, _GRID_W, d // 2))
    ec = np.broadcast_to(axis_embed(_GRID_W)[None], (rows, _GRID_W, d // 2))
    return np.concatenate([er, ec], axis=-1).reshape(rows * _GRID_W, d).astype(np.float32)


@functools.lru_cache(maxsize=None)
def _s5_masks():
    step = np.repeat(np.arange(_S5_CHUNK), _S5_H)
    causal = (step[None, :] >= step[:, None]).astype(np.float32)
    anti = (step[:, None] >= step[None, :]).astype(np.float32)
    return causal, anti


def _filter_kernel(z_ref, w1_ref, b1_ref, w2_ref, b2_ref, fr_ref, w3f_ref, w3b_ref, dl_ref,
                   at_ref, ab_ref, alt_ref, krt_ref, krb_ref, ki_ref, h_ref):
    @pl.when((pl.program_id(0) == 0) & (pl.program_id(1) == 0))
    def _():
        fr = fr_ref[...]
        h1 = jnp.sin(fr * (_dot_3pass(z_ref[...], w1_ref[...]) + b1_ref[...]))
        h_ref[...] = jnp.sin(fr * (_dot_3pass(h1, w2_ref[...]) + b2_ref[...]))

    h = h_ref[...]
    decay = jnp.exp(-z_ref[:, 0:1] * dl_ref[...])
    hf = _dot_3pass(h, w3f_ref[...]) * decay
    hb = _dot_3pass(h, w3b_ref[...]) * decay
    norm = (jnp.sum(jnp.abs(hf), axis=0, keepdims=True)
            + jnp.sum(jnp.abs(hb), axis=0, keepdims=True) + _EPS)
    hf = hf / norm
    hb = hb / norm
    first = lax.broadcasted_iota(jnp.int32, hf.shape, 0) == 0
    hb = jnp.where(first, 0.0, hb)
    ksum = hf + hb
    kdiff = hf - hb
    kre = jnp.dot(at_ref[...], ksum.astype(_BF16), preferred_element_type=_F32)
    kim = jnp.dot(ab_ref[...], kdiff.astype(_BF16), preferred_element_type=_F32)
    nyq = jnp.sum(alt_ref[...] * ksum, axis=0, keepdims=True)
    krt_ref[...] = kre
    krb_ref[...] = jnp.where(first, nyq, kre)
    ki_ref[...] = jnp.where(first, 0.0, kim)


def _hyena_filters(seq, d, w1p, b1, w2, b2, freq, w3):
    zpad, deltas = _filter_consts(seq, d)
    fwd, _, alt = _dft_consts(seq)
    fwd = jnp.asarray(fwd).astype(_BF16)
    top = pl.BlockSpec((seq, seq), lambda o, j: (0, 0), pipeline_mode=pl.Buffered(1))
    bot = pl.BlockSpec((seq, seq), lambda o, j: (1, 0), pipeline_mode=pl.Buffered(1))
    nb = d // _FILT_DBLK
    fo = _HY_FO
    out_sds = jax.ShapeDtypeStruct((_HY_ORDER, seq, d), _F32)
    out_spec = pl.BlockSpec((None, seq, _FILT_DBLK), lambda o, j: (o, 0, j))
    return pl.pallas_call(
        _filter_kernel,
        grid=(_HY_ORDER, nb),
        in_specs=[
            _const_spec((seq, fo)), _const_spec((fo, fo)), _const_spec((1, fo)),
            _const_spec((fo, fo)), _const_spec((1, fo)), _const_spec((1, fo)),
            pl.BlockSpec((fo, _FILT_DBLK), lambda o, j: (0, (2 * o) * nb + j)),
            pl.BlockSpec((fo, _FILT_DBLK), lambda o, j: (0, (2 * o + 1) * nb + j)),
            pl.BlockSpec((1, _FILT_DBLK), lambda o, j: (0, j)),
            top, bot,
            _const_spec((seq, 1)),
        ],
        out_specs=[out_spec, out_spec, out_spec],
        out_shape=[out_sds, out_sds, out_sds],
        scratch_shapes=[pltpu.VMEM((seq, fo), _F32)],
        compiler_params=_params(("arbitrary", "arbitrary")),
        name=f"filt{seq}",
    )(jnp.asarray(zpad), w1p, b1, w2, b2, freq, w3, w3, jnp.asarray(deltas),
      fwd, fwd, jnp.asarray(alt))


def _pre_kernel(has_pos, *refs):
    if has_pos:
        x_ref, pos_ref, sh_ref, sc_ref, g_ref, w_ref, b_ref, z_ref = refs
        x = x_ref[...] + pos_ref[...]
    else:
        x_ref, sh_ref, sc_ref, g_ref, w_ref, b_ref, z_ref = refs
        x = x_ref[...]
    h = _rmsnorm(x, g_ref[...]) * (1.0 + sc_ref[...]) + sh_ref[...]
    z_ref[...] = (_dot_bf16(h, w_ref[...]) + b_ref[...]).astype(z_ref.dtype)


def _pre_hyena(x, pos, mod5, row_of_b, layer, norm_g, w_in, b_in):
    bsz, seq, d = x.shape
    n = w_in.shape[-1]
    tm = _TOKEN_TILE
    tok = pl.BlockSpec((None, tm, d), lambda t, b: (b, t, 0))
    ins, specs = [x], [tok]
    if pos is not None:
        ins.append(pos)
        specs.append(pl.BlockSpec((tm, d), lambda t, b: (t, 0)))
    ins += [mod5, mod5, norm_g, w_in, b_in]
    specs += [_mod_spec(d, layer, 0, row_of_b), _mod_spec(d, layer, 1, row_of_b),
              _layer_spec((1, d), layer), _layer_spec((d, n), 0), _layer_spec((1, n), 0)]
    return pl.pallas_call(
        functools.partial(_pre_kernel, pos is not None),
        grid=(seq // tm, bsz),
        in_specs=specs,
        out_specs=pl.BlockSpec((None, tm, n), lambda t, b: (b, t, 0)),
        out_shape=jax.ShapeDtypeStruct((bsz, seq, n), _BF16),
        compiler_params=_params(("arbitrary", "arbitrary")),
        name=f"pre{seq}",
    )(*ins)


def _conv_kernel(zv_ref, z1_ref, z2_ref, wv_ref, w1_ref, w2_ref, bv_ref, b1_ref, b2_ref, fb_ref,
                 krt_ref, krb_ref, ki_ref, ft_ref, fb2_ref, it_ref, ib_ref, o_ref):
    seq, dblk = zv_ref.shape
    row = lax.broadcasted_iota(jnp.int32, (seq, dblk), 0)
    first = row == 0
    last = row == seq - 1

    def short_conv(z_ref, w_ref, b_ref):
        z = z_ref[...].astype(_F32)
        prev = jnp.where(first, 0.0, pltpu.roll(z, 1, 0))
        nxt = jnp.where(last, 0.0, pltpu.roll(z, seq - 1, 0))
        w = w_ref[...]
        return prev * w[0:1] + z * w[1:2] + nxt * w[2:3] + b_ref[...]

    v = short_conv(zv_ref, wv_ref, bv_ref)
    gates = (short_conv(z1_ref, w1_ref, b1_ref), short_conv(z2_ref, w2_ref, b2_ref))
    for o in range(_HY_ORDER):
        vb = v.astype(_BF16)
        p = jnp.dot(ft_ref[...], vb, preferred_element_type=_F32)
        q = jnp.dot(fb2_ref[...], vb, preferred_element_type=_F32)
        ki = ki_ref[o]
        ytop = p * krt_ref[o] - q * ki
        ybot = p * ki + q * krb_ref[o]
        y = (jnp.dot(it_ref[...], ytop.astype(_BF16), preferred_element_type=_F32)
             + jnp.dot(ib_ref[...], ybot.astype(_BF16), preferred_element_type=_F32))
        v = (y + v * fb_ref[o:o + 1, :]) * gates[o]
    o_ref[...] = v.astype(o_ref.dtype)


def _hyena_conv(z, conv_w, conv_b, fbias, krt, krb, ki):
    bsz, seq, d3 = z.shape
    d = d3 // 3
    dblk = _CONV_DBLK
    nb = d // dblk
    fwd, inv, _ = _dft_consts(seq)
    fwd = jnp.asarray(fwd).astype(_BF16)
    inv = jnp.asarray(inv).astype(_BF16)

    def mspec(r, c):
        return pl.BlockSpec((seq, seq), lambda j, b: (r, c), pipeline_mode=pl.Buffered(1))

    def zspec(k):
        return pl.BlockSpec((None, seq, dblk), lambda j, b: (b, 0, k * nb + j))

    def wspec(rows, k):
        return pl.BlockSpec((rows, dblk), lambda j, b: (0, k * nb + j))

    kspec = pl.BlockSpec((_HY_ORDER, seq, dblk), lambda j, b: (0, 0, j), pipeline_mode=pl.Buffered(1))
    return pl.pallas_call(
        _conv_kernel,
        grid=(nb, bsz),
        in_specs=[zspec(0), zspec(1), zspec(2),
                  wspec(3, 0), wspec(3, 1), wspec(3, 2),
                  wspec(1, 0), wspec(1, 1), wspec(1, 2),
                  wspec(_HY_ORDER, 0),
                  kspec, kspec, kspec,
                  mspec(0, 0), mspec(1, 0), mspec(0, 0), mspec(0, 1)],
        out_specs=pl.BlockSpec((None, seq, dblk), lambda j, b: (b, 0, j)),
        out_shape=jax.ShapeDtypeStruct((bsz, seq, d), _BF16),
        compiler_params=_params(("arbitrary", "arbitrary")),
        name=f"conv{seq}",
    )(z, z, z, conv_w, conv_w, conv_w, conv_b, conv_b, conv_b, fbias, krt, krb, ki,
      fwd, fwd, inv, inv)


def _post_kernel(kind, tail, names, *refs):
    r = dict(zip(names, refs))
    tm, d = r["x"].shape
    steps = _S5_CHUNK
    nsub = max(tm // _POST_SUB, 1)
    sub = tm // nsub

    def stages(i):
        rows = slice(i * sub, (i + 1) * sub)
        crows = slice(i * sub // steps, (i + 1) * sub // steps)
        st = {}

        def head():
            st["x"] = r["x"][rows, :]
            if "pos" in r:
                st["x"] = st["x"] + r["pos"][rows, :]
            if kind == "hyena":
                st["mix"] = r["mix"][rows, :]
            else:
                u = _rmsnorm(st["x"], r["n1"][...]) * (1.0 + r["sc1"][...]) + r["sh1"][...]
                mix = jnp.swapaxes(r["mix"][:, crows, :], 0, 1).reshape(sub, d)
                st["mix"] = _gelu_tanh(mix + u * r["skip"][...]).astype(_BF16)

        def mixer_out():
            st["m"] = _dot_bf16(st.pop("mix"), r["w_out"][...]) + r["b_out"][...]

        def mid():
            m = st.pop("m")
            if kind == "s5":
                half = m.shape[-1] // 2
                m = m[:, :half] * jax.nn.sigmoid(m[:, half:])
            st["x"] = st["x"] + r["g1"][...] * m
            h = _rmsnorm(st["x"], r["n2"][...]) * (1.0 + r["sc2"][...]) + r["sh2"][...]
            st["h"] = h.astype(_BF16)

        def ffn_in():
            st["ab"] = _dot_bf16(st.pop("h"), r["w13"][...])

        def act():
            ab = st.pop("ab")
            ff = ab.shape[-1] // 2
            st["act"] = (_silu(ab[:, :ff]) * ab[:, ff:]).astype(_BF16)

        def ffn_out():
            st["f"] = _dot_bf16(st.pop("act"), r["w2"][...])

        def finish():
            x = st.pop("x") + r["g2"][...] * st.pop("f")
            if tail == "next":
                r["o_x"][rows, :] = x
                un = _rmsnorm(x, r["nn"][...]) * (1.0 + r["scn"][...]) + r["shn"][...]
                r["o_u"][:, crows, :] = jnp.swapaxes(un.reshape(-1, steps, d), 0, 1)
            else:
                r["o_x"][rows, :] = _rmsnorm(x, r["nf"][...])

        return [head, mixer_out, mid, ffn_in, act, ffn_out, finish]

    pipes = [stages(i) for i in range(nsub)]
    depth = len(pipes[0])
    for t in range(depth + nsub - 1):
        for i, pipe in enumerate(pipes):
            if 0 <= t - i < depth:
                pipe[t - i]()


def _post_mixer(kind, tail, x, pos, mix, mod5, row_of_b, layer, w_out, b_out, norm1_g, skip,
                norm2_g, w13, w2, final_g):
    bsz, seq, d = x.shape
    tm = min(_POST_TILE, seq)
    tok = pl.BlockSpec((None, tm, d), lambda t, b: (b, t, 0))
    nt = seq // tm
    steps = _S5_CHUNK
    stepmajor = pl.BlockSpec((steps, tm // steps, d), lambda t, b: (0, b * nt + t, 0))
    stepmajor_sds = jax.ShapeDtypeStruct((steps, bsz * seq // steps, d), _F32)
    n_out = w_out.shape[-1]
    f2 = w13.shape[-1]

    def mod(part, lyr=layer):
        return (mod5, _mod_spec(d, lyr, part, row_of_b))

    items = [("x", x, tok), ("mix", mix, stepmajor if kind == "s5" else tok),
             ("w_out", w_out, _layer_spec((d, n_out), 0)), ("b_out", b_out, _layer_spec((1, n_out), 0)),
             ("g1",) + mod(2), ("sh2",) + mod(3), ("sc2",) + mod(4), ("g2",) + mod(5),
             ("n2", norm2_g, _layer_spec((1, d), layer)),
             ("w13", w13, _layer_spec((d, f2), layer)), ("w2", w2, _layer_spec((f2 // 2, d), layer))]
    if kind == "s5":
        items += [("n1", norm1_g, _layer_spec((1, d), layer)), ("sh1",) + mod(0), ("sc1",) + mod(1),
                  ("skip", skip, _layer_spec((1, d), 0))]
    if pos is not None:
        items.append(("pos", pos, pl.BlockSpec((tm, d), lambda t, b: (t, 0))))
    out_specs, out_shape = [tok], [jax.ShapeDtypeStruct(x.shape, _F32)]
    names_out = ["o_x"]
    if tail == "next":
        items += [("nn", norm1_g, _layer_spec((1, d), layer + 1)),
                  ("shn",) + mod(0, layer + 1), ("scn",) + mod(1, layer + 1)]
        out_specs.append(stepmajor)
        out_shape.append(stepmajor_sds)
        names_out.append("o_u")
    else:
        items.append(("nf", final_g, _const_spec((1, d))))
    names = tuple(i[0] for i in items) + tuple(names_out)
    out = pl.pallas_call(
        functools.partial(_post_kernel, kind, tail, names),
        grid=(seq // tm, bsz),
        in_specs=[i[2] for i in items], out_specs=out_specs, out_shape=out_shape,
        compiler_params=_params(("arbitrary", "arbitrary")),
        name=f"post_{kind}{seq}",
    )(*[i[1] for i in items])
    return out


_S5_GB = 8
_S5_LANES = 2 * _S5_P
_S5_MATS_GB = 8


def _cmul(ar, ai, br, bi):
    return ar * br - ai * bi, ar * bi + ai * br


def _s5_mats_kernel(prm_ref, btr_ref, bti_ref, cr_ref, ci_ref, causal_ref, anti_ref,
                    m_ref, w_ref, vf_ref, vb_ref, a_ref):
    c, h, lanes = _S5_CHUNK, _S5_H, _S5_LANES
    a_re = prm_ref[0:1, :]
    a_im = prm_ref[1:2, :]
    dt = jnp.exp(prm_ref[2:3, :])
    sr = dt * a_re
    ph = dt * a_im
    mag = jnp.exp(sr)
    nr = mag * jnp.cos(ph) - 1.0
    ni = mag * jnp.sin(ph)
    den = a_re * a_re + a_im * a_im
    co_re = (nr * a_re + ni * a_im) / den
    co_im = (ni * a_re - nr * a_im) / den
    bt = _cmul(co_re, co_im, btr_ref[...], bti_ref[...])
    ct = (cr_ref[...], ci_ref[...])

    def powers(k):
        kf = k.astype(_F32)
        e = jnp.exp(kf * sr)
        return e * jnp.cos(kf * ph), e * jnp.sin(kf * ph)

    def per_step(x):
        return jnp.broadcast_to(x[:, None, :], (c, h, lanes)).reshape(c * h, lanes)

    def per_chan(x):
        return jnp.broadcast_to(x[None, :, :], (c, h, lanes)).reshape(c * h, lanes)

    step = lax.broadcasted_iota(jnp.int32, (c, lanes), 0)
    fwd = lax.broadcasted_iota(jnp.int32, (c, lanes), 1) < _S5_P
    tb = tuple(per_chan(x) for x in bt)
    tc = tuple(per_chan(x) for x in ct)

    def table(base, k_fwd, k_bwd):
        return _cmul(*base, *(per_step(x) for x in powers(jnp.where(fwd, k_fwd, k_bwd))))

    lmat = table(tb, -step, step)
    rmat = table(tc, step, -step)
    wmat = table(tb, c - 1 - step, step)
    vmat = table(tc, step + 1, c - step)

    def nt3(x, y):
        xh, xl = _split_bf16(x)
        yh, yl = _split_bf16(y)
        return _dot_nt(xh, yh) + _dot_nt(xl, yh) + _dot_nt(xh, yl)

    fwd_rows = lax.broadcasted_iota(jnp.int32, (c * h, lanes), 1) < _S5_P

    def kernel_half(keep):
        return (nt3(jnp.where(keep, lmat[0], 0.0), rmat[0])
                - nt3(jnp.where(keep, lmat[1], 0.0), rmat[1]))

    m = causal_ref[...] * kernel_half(fwd_rows) + anti_ref[...] * kernel_half(jnp.logical_not(fwd_rows))
    m_ref[...] = m.astype(m_ref.dtype)
    w_ref[...] = jnp.concatenate([wmat[0], wmat[1]], axis=1).astype(w_ref.dtype)
    vcat = jnp.concatenate([vmat[0], -vmat[1]], axis=1)
    keep_f = (lax.broadcasted_iota(jnp.int32, vcat.shape, 1) % lanes) < _S5_P
    vf_ref[...] = jnp.where(keep_f, vcat, 0.0).T.astype(vf_ref.dtype)
    vb_ref[...] = jnp.where(keep_f, 0.0, vcat).T.astype(vb_ref.dtype)
    chunk_k = jnp.full((1, lanes), c, jnp.int32)
    a_chunk = powers(chunk_k)
    a_ref[0:1, :] = a_chunk[0]
    a_ref[1:2, :] = a_chunk[1]


def _s5_mats_block_kernel(prm_ref, btr_ref, bti_ref, cr_ref, ci_ref, causal_ref, anti_ref,
                          m_ref, w_ref, vf_ref, vb_ref, a_ref):
    for j in range(prm_ref.shape[0]):
        _s5_mats_kernel(prm_ref.at[j], btr_ref.at[j], bti_ref.at[j], cr_ref.at[j], ci_ref.at[j],
                        causal_ref, anti_ref,
                        m_ref.at[j], w_ref.at[j], vf_ref.at[j], vb_ref.at[j], a_ref.at[j])


def _s5_mats(prm, bt_re, bt_im, c_re, c_im):
    g = prm.shape[0]
    cw, lanes = _CHUNK_W, _S5_LANES
    causal, anti = (jnp.asarray(a) for a in _s5_masks())
    gm = _S5_MATS_GB
    gspec = pl.BlockSpec((gm, _S5_H, lanes), lambda i: (i, 0, 0))
    mat_spec = pl.BlockSpec((gm, cw, cw), lambda i: (i, 0, 0))
    mat_sds = jax.ShapeDtypeStruct((g, cw, cw), _BF16)
    return pl.pallas_call(
        _s5_mats_block_kernel,
        grid=(g // gm,),
        in_specs=[pl.BlockSpec((gm, 3, lanes), lambda i: (i, 0, 0)), gspec, gspec, gspec, gspec,
                  _const_spec((cw, cw)), _const_spec((cw, cw))],
        out_specs=[mat_spec] * 4 + [pl.BlockSpec((gm, 2, lanes), lambda i: (i, 0, 0))],
        out_shape=[mat_sds] * 4 + [jax.ShapeDtypeStruct((g, 2, lanes), _F32)],
        compiler_params=_params(("arbitrary",)),
        name="s5_mats",
    )(prm, bt_re, bt_im, c_re, c_im, causal, anti)


def _segment_transpose(xs):
    n = len(xs)
    seg = lax.broadcasted_iota(jnp.int32, xs[0].shape, 1) // _S5_H
    diags = []
    for k in range(n):
        z = xs[-k % n]
        for m in range(1, n):
            z = jnp.where(seg == m, xs[(m - k) % n], z)
        diags.append(pltpu.roll(z, ((n - k) % n) * _S5_H, 1) if k else z)
    ys = []
    for b in range(n):
        y = diags[b]
        for a in range(1, n):
            y = jnp.where(seg == a, diags[(b - a) % n], y)
        ys.append(y)
    return ys


def _s5_scan_kernel(nchunk, bsz, has_init, *refs):
    if has_init:
        u_ref, m_ref, w_ref, vf_ref, vb_ref, a_ref, h0_ref, y_ref = refs[:8]
    else:
        u_ref, m_ref, w_ref, vf_ref, vb_ref, a_ref, y_ref, fin_ref = refs[:8]
    s_ref, pf_ref, pb_ref, uall_ref, yall_ref = refs[8:]
    lanes, gb = _S5_LANES, _S5_GB
    rows = nchunk * bsz
    fwd = lax.broadcasted_iota(jnp.int32, (bsz, lanes), 1) < _S5_P

    for half in range(2):
        per_group = _segment_transpose([u_ref[half * 8 + s8] for s8 in range(8)])
        for gl in range(gb):
            uall_ref[gl, :, half * 128:(half + 1) * 128] = per_group[gl].astype(_BF16)

    for gl in range(gb):
        u = uall_ref[gl]
        yall_ref[gl] = jnp.dot(u, m_ref[gl], preferred_element_type=_F32)
        s = jnp.dot(u, w_ref[gl], preferred_element_type=_F32)
        s_ref[...] = jnp.swapaxes(s.reshape(bsz, nchunk, 2 * lanes), 0, 1).reshape(rows, 2 * lanes)

        a_re = a_ref[gl, 0:1, :]
        a_im = a_ref[gl, 1:2, :]
        if has_init:
            init = (h0_ref[gl, :, 0:lanes], h0_ref[gl, :, lanes:2 * lanes])
        else:
            init = (jnp.zeros((bsz, lanes), _F32), jnp.zeros((bsz, lanes), _F32))

        def step(k, carry):
            h_re, h_im = carry
            rowf = pl.ds(pl.multiple_of(k * bsz, bsz), bsz)
            rowb = pl.ds(pl.multiple_of((nchunk - 1 - k) * bsz, bsz), bsz)
            pf_ref[rowf, 0:lanes] = h_re
            pf_ref[rowf, lanes:2 * lanes] = h_im
            pb_ref[rowb, 0:lanes] = h_re
            pb_ref[rowb, lanes:2 * lanes] = h_im
            s_re = jnp.where(fwd, s_ref[rowf, 0:lanes], s_ref[rowb, 0:lanes])
            s_im = jnp.where(fwd, s_ref[rowf, lanes:2 * lanes], s_ref[rowb, lanes:2 * lanes])
            return (a_re * h_re - a_im * h_im + s_re, a_re * h_im + a_im * h_re + s_im)

        fin = lax.fori_loop(0, nchunk, step, init)
        if not has_init:
            fin_ref[gl, :, 0:lanes] = fin[0]
            fin_ref[gl, :, lanes:2 * lanes] = fin[1]
        carried = (jnp.dot(pf_ref[...].astype(_BF16), vf_ref[gl], preferred_element_type=_F32)
                   + jnp.dot(pb_ref[...].astype(_BF16), vb_ref[gl], preferred_element_type=_F32))
        carried = jnp.swapaxes(carried.reshape(nchunk, bsz, 2 * lanes), 0, 1).reshape(rows, 2 * lanes)
        yall_ref[gl] = yall_ref[gl] + carried

    for half in range(2):
        per_step = _segment_transpose([yall_ref[gl, :, half * 128:(half + 1) * 128] for gl in range(gb)])
        for t8 in range(8):
            y_ref[half * 8 + t8] = per_step[t8]


def _s5_scan(u, bsz, mats, init):
    steps, rows, d = u.shape
    g = d // _S5_H
    nchunk = rows // bsz
    cw, gb = _CHUNK_W, _S5_GB
    lanes = gb * _S5_H
    tok_spec = pl.BlockSpec((steps, rows, lanes), lambda i: (0, 0, i))
    mat_spec = pl.BlockSpec((gb, cw, cw), lambda i: (i, 0, 0))
    st_spec = pl.BlockSpec((gb, bsz, cw), lambda i: (i, 0, 0))
    ins = [u] + list(mats)
    specs = [tok_spec] + [mat_spec] * 4 + [pl.BlockSpec((gb, 2, _S5_LANES), lambda i: (i, 0, 0))]
    y_sds = jax.ShapeDtypeStruct(u.shape, _F32)
    if init is not None:
        ins.append(init)
        specs.append(st_spec)
        out_specs, out_shape = tok_spec, y_sds
    else:
        out_specs = [tok_spec, st_spec]
        out_shape = [y_sds, jax.ShapeDtypeStruct((g, bsz, cw), _F32)]
    out = pl.pallas_call(
        functools.partial(_s5_scan_kernel, nchunk, bsz, init is not None),
        grid=(g // gb,),
        in_specs=specs, out_specs=out_specs, out_shape=out_shape,
        scratch_shapes=[pltpu.VMEM((rows, cw), _F32)] * 3
        + [pltpu.VMEM((gb, rows, cw), _BF16), pltpu.VMEM((gb, rows, cw), _F32)],
        compiler_params=_params(("arbitrary",)),
        name=f"s5_{nchunk * steps}",
    )(*ins)
    if init is not None:
        return out, None
    return out[0], out[1]


def _trunk(x, pos, row_of_b, shared_mod, init_state, mod5, filt, s5_mats, wts):
    bsz, seq, d = x.shape
    fold = _POST_TILE // seq if (shared_mod and seq < _POST_TILE) else 1
    rows = lambda a: a.reshape(bsz // fold, seq * fold, a.shape[-1])
    seqs = lambda a: a.reshape(bsz, seq, a.shape[-1])
    z = _pre_hyena(rows(x), pos, mod5, row_of_b, 0, wts["norm1_g"], wts["hy_in_w"], wts["hy_in_b"])
    v = _hyena_conv(seqs(z), wts["hy_conv_w"], wts["hy_conv_b"], wts["hy_fbias"], *filt)
    x1, u = _post_mixer("hyena", "next", rows(x), pos, rows(v), mod5, row_of_b, 0, wts["hy_out_w"],
                        wts["hy_out_b"], wts["norm1_g"], None, wts["norm2_g"], wts["ffn_w13"], wts["ffn_w2"],
                        None)
    ys, fin = _s5_scan(u, bsz, s5_mats, init_state)
    (out,) = _post_mixer("s5", "final", x1, None, ys, mod5, row_of_b, 1, wts["s5_glu_w"], wts["s5_glu_b"],
                         wts["norm1_g"], wts["s5_D"], wts["norm2_g"], wts["ffn_w13"], wts["ffn_w2"],
                         wts["final_g"])
    return seqs(out), fin


def kernel(x_prompt, x_sample, state_s5, c, c_ctx, norm1_g, norm2_g, final_g, ada_w, ada_b, ffn_w13, ffn_w2, hy_in_w, hy_in_b, hy_conv_w, hy_conv_b, hy_pe_w1, hy_pe_b1, hy_pe_w2, hy_pe_b2, hy_pe_w3, hy_freq, hy_fbias, hy_out_w, hy_out_b, s5_A_re, s5_A_im, s5_log_dt, s5_B_re, s5_B_im, s5_C_re, s5_C_im, s5_D, s5_glu_w, s5_glu_b):
    depth, d = norm1_g.shape
    assert depth == 2 and hy_in_w.shape[0] == 1 and s5_glu_w.shape[0] == 1
    dec_b, dec_seq, _ = x_sample.shape
    g = d // _S5_H
    p = _S5_P

    nrow = -(-(1 + dec_b) // 8) * 8
    cond = jnp.concatenate([c_ctx[None], c, jnp.zeros((nrow - 1 - dec_b, d), _F32)], axis=0)
    mod5 = _modulation(cond, ada_w, ada_b)

    row3 = lambda a: a.reshape(a.shape[0], 1, a.shape[-1])
    wts = dict(
        norm1_g=row3(norm1_g), norm2_g=row3(norm2_g), final_g=final_g[None],
        hy_in_w=hy_in_w.astype(_BF16), hy_in_b=row3(hy_in_b),
        hy_conv_w=hy_conv_w[0], hy_conv_b=hy_conv_b, hy_fbias=hy_fbias[0],
        hy_out_w=hy_out_w.astype(_BF16), hy_out_b=row3(hy_out_b),
        ffn_w13=ffn_w13.astype(_BF16), ffn_w2=ffn_w2.astype(_BF16),
        s5_glu_w=s5_glu_w.astype(_BF16), s5_glu_b=row3(s5_glu_b), s5_D=row3(s5_D),
    )
    dirs_on_lanes = lambda a: a.transpose(1, 0, 2).reshape(g, 2 * p)
    ldt = jnp.broadcast_to(s5_log_dt[0][:, :, None], (2, g, p))
    prm = jnp.stack([dirs_on_lanes(s5_A_re[0]), dirs_on_lanes(s5_A_im[0]), dirs_on_lanes(ldt)], axis=1)
    bt = lambda a: a.transpose(1, 3, 0, 2).reshape(g, _S5_H, 2 * p)
    ct = lambda a: a.transpose(1, 2, 0, 3).reshape(g, _S5_H, 2 * p)
    s5_mats = _s5_mats(prm, bt(s5_B_re[0]), bt(s5_B_im[0]), ct(s5_C_re[0]), ct(s5_C_im[0]))

    w1p = jnp.pad(hy_pe_w1[0], ((0, _HY_FO - _HY_EMB), (0, 0)))
    filt_args = (w1p, hy_pe_b1, hy_pe_w2[0], hy_pe_b2, hy_freq, hy_pe_w3[0])
    filt_ctx = _hyena_filters(x_prompt.shape[1], d, *filt_args)
    filt_lat = _hyena_filters(dec_seq, d, *filt_args)

    y_prompt, fin = _trunk(x_prompt, None, lambda b: 0, True, None, mod5, filt_ctx, s5_mats, wts)
    new_state = fin.reshape(g, -1, 2, 2, p).transpose(1, 3, 2, 0, 4)[:, None]

    init = state_s5[:, 0].transpose(3, 0, 2, 1, 4).reshape(g, dec_b, 4 * p)
    pos = jnp.asarray(_pos_embed(dec_seq // _GRID_W, d))
    y_sample, _ = _trunk(x_sample, pos, lambda b: b + 1, False, init, mod5, filt_lat, s5_mats, wts)
    return (y_prompt, y_sample, new_state)
```

```python
import functools
import math

import numpy as np
import jax
import jax.numpy as jnp
from jax import lax
from jax.experimental import pallas as pl
from jax.experimental.pallas import tpu as pltpu

_F32 = jnp.float32
_BF16 = jnp.bfloat16

_EPS = 1e-6
_GRID_W = 64
_POS_BASE = 10000.0
_HY_ORDER = 2
_HY_EMB = 33
_HY_FO = 64
_HY_TARGET = 1e-2
_HY_FAST = 0.3
_HY_SLOW = 1.5
_S5_H = 16
_S5_P = 64
_S5_CHUNK = 16
_CHUNK_W = _S5_CHUNK * _S5_H

_VMEM_LIMIT = 56 * 1024 * 1024
_TOKEN_TILE = 512
_POST_TILE = 512
_POST_SUB = 256
_CONV_DBLK = 512
_FILT_DBLK = 512
_MOD_PARTS = 2


def _params(sem):
    return pltpu.CompilerParams(dimension_semantics=sem, vmem_limit_bytes=_VMEM_LIMIT)


def _dot_bf16(a, b):
    return jnp.dot(a.astype(_BF16), b.astype(_BF16), preferred_element_type=_F32)


def _split_bf16(x):
    hi = x.astype(_BF16)
    return hi, (x - hi.astype(_F32)).astype(_BF16)


def _dot_3pass(a, b):
    ah, al = _split_bf16(a)
    bh, bl = _split_bf16(b)
    return (jnp.dot(ah, bh, preferred_element_type=_F32) + jnp.dot(al, bh, preferred_element_type=_F32)
            + jnp.dot(ah, bl, preferred_element_type=_F32))


def _dot_nt(a, b, precision=None):
    return lax.dot_general(a, b, (((1,), (1,)), ((), ())), precision=precision,
                           preferred_element_type=_F32)


def _rmsnorm(x, g):
    ms = jnp.mean(x * x, axis=-1, keepdims=True)
    return x * lax.rsqrt(ms + _EPS) * g


def _silu(x):
    return x * jax.nn.sigmoid(x)


def _gelu_tanh(x):
    c = math.sqrt(2.0 / math.pi)
    return 0.5 * x * (1.0 + jnp.tanh(c * (x + 0.044715 * (x * x * x))))


def _const_spec(shape):
    nd = len(shape)
    return pl.BlockSpec(shape, lambda *_: (0,) * nd, pipeline_mode=pl.Buffered(1))


def _layer_spec(shape, layer):
    nd = len(shape)
    return pl.BlockSpec((None,) + tuple(shape), lambda *_: (layer,) + (0,) * nd,
                        pipeline_mode=pl.Buffered(1))


def _mod_spec(d, layer, part, row_of_b):
    return pl.BlockSpec((None, None, None, 1, d), lambda t, b: (layer, part, row_of_b(b), 0, 0))


def _mod_kernel(cond_ref, w_ref, b_ref, o_ref):
    d = o_ref.shape[-1]
    r = _dot_3pass(_silu(cond_ref[...]), w_ref[...])
    for p in range(o_ref.shape[0]):
        o_ref[p] = r[:, p * d:(p + 1) * d] + b_ref[p]


def _modulation(cond, ada_w, ada_b):
    depth, d, d6 = ada_w.shape
    parts = d6 // d
    rows = cond.shape[0]
    pb = _MOD_PARTS
    out = pl.pallas_call(
        _mod_kernel,
        grid=(depth, parts // pb),
        in_specs=[
            pl.BlockSpec((rows, d), lambda l, p: (0, 0)),
            pl.BlockSpec((None, d, pb * d), lambda l, p: (l, 0, p)),
            pl.BlockSpec((None, pb, 1, d), lambda l, p: (l, p, 0, 0)),
        ],
        out_specs=pl.BlockSpec((None, pb, rows, d), lambda l, p: (l, p, 0, 0)),
        out_shape=jax.ShapeDtypeStruct((depth, parts, rows, d), _F32),
        compiler_params=_params(("arbitrary", "arbitrary")),
        name="mod",
    )(cond, ada_w, ada_b.reshape(depth, parts, 1, d))
    return out.reshape(depth, parts, rows, 1, d)


@functools.lru_cache(maxsize=None)
def _dft_consts(seq):
    f = np.arange(seq)[:, None]
    t = np.arange(seq)[None, :]
    ang = np.pi * ((f * t) % (2 * seq)) / seq
    cos = np.cos(ang)
    sin = np.sin(ang)
    alt = np.where(np.arange(seq) % 2 == 0, 1.0, -1.0)
    fwd_top = cos
    fwd_bot = -sin
    fwd_bot[0] = alt
    wgt = np.full((seq, 1), 2.0)
    wgt[0] = 1.0
    inv_top = (cos * wgt).T / (2 * seq)
    inv_bot = (-2.0 * sin).T / (2 * seq)
    inv_bot[:, 0] = alt / (2 * seq)
    fwd = np.concatenate([fwd_top, fwd_bot], axis=0).astype(np.float32)
    inv = np.concatenate([inv_top, inv_bot], axis=1).astype(np.float32)
    return fwd, inv, alt.astype(np.float32)[:, None]


@functools.lru_cache(maxsize=None)
def _filter_consts(seq, d):
    t = np.linspace(0.0, 1.0, seq)[:, None]
    w = 2.0 * np.pi * np.arange(seq)[:, None] / seq
    nb = (_HY_EMB - 1) // 2
    bands = np.linspace(1e-4, nb - 1, nb)[None, :]
    z = np.concatenate([t, np.cos(bands * w), -np.sin(bands * w)], axis=-1)
    zpad = np.zeros((seq, _HY_FO))
    zpad[:, :_HY_EMB] = z
    max_decay = math.log(_HY_TARGET) / _HY_FAST
    min_decay = math.log(_HY_TARGET) / _HY_SLOW
    deltas = np.abs(np.linspace(min_decay, max_decay, d))[None, :]
    return zpad.astype(np.float32), deltas.astype(np.float32)


@functools.lru_cache(maxsize=None)
def _pos_embed(rows, d):
    quarter = d // 4
    omega = 1.0 / (_POS_BASE ** (np.arange(quarter, dtype=np.float64) / quarter))

    def axis_embed(n):
        ang = np.arange(n, dtype=np.float64)[:, None] * omega[None]
        return np.concatenate([np.sin(ang), np.cos(ang)], axis=-1)

    er = np.broadcast_to(axis_embed(rows)[:, None], (rows, _GRID_W, d // 2))
    ec = np.broadcast_to(axis_embed(_GRID_W)[None], (rows, _GRID_W, d // 2))
    return np.concatenate([er, ec], axis=-1).reshape(rows * _GRID_W, d).astype(np.float32)


@functools.lru_cache(maxsize=None)
def _s5_masks():
    step = np.repeat(np.arange(_S5_CHUNK), _S5_H)
    causal = (step[None, :] >= step[:, None]).astype(np.float32)
    anti = (step[:, None] >= step[None, :]).astype(np.float32)
    return causal, anti


def _filter_kernel(z_ref, w1_ref, b1_ref, w2_ref, b2_ref, fr_ref, w3f_ref, w3b_ref, dl_ref,
                   at_ref, ab_ref, alt_ref, krt_ref, krb_ref, ki_ref, h_ref):
    @pl.when((pl.program_id(0) == 0) & (pl.program_id(1) == 0))
    def _():
        fr = fr_ref[...]
        h1 = jnp.sin(fr * (_dot_3pass(z_ref[...], w1_ref[...]) + b1_ref[...]))
        h_ref[...] = jnp.sin(fr * (_dot_3pass(h1, w2_ref[...]) + b2_ref[...]))

    h = h_ref[...]
    decay = jnp.exp(-z_ref[:, 0:1] * dl_ref[...])
    hf = _dot_3pass(h, w3f_ref[...]) * decay
    hb = _dot_3pass(h, w3b_ref[...]) * decay
    norm = (jnp.sum(jnp.abs(hf), axis=0, keepdims=True)
            + jnp.sum(jnp.abs(hb), axis=0, keepdims=True) + _EPS)
    hf = hf / norm
    hb = hb / norm
    first = lax.broadcasted_iota(jnp.int32, hf.shape, 0) == 0
    hb = jnp.where(first, 0.0, hb)
    ksum = hf + hb
    kdiff = hf - hb
    kre = jnp.dot(at_ref[...], ksum.astype(_BF16), preferred_element_type=_F32)
    kim = jnp.dot(ab_ref[...], kdiff.astype(_BF16), preferred_element_type=_F32)
    nyq = jnp.sum(alt_ref[...] * ksum, axis=0, keepdims=True)
    krt_ref[...] = kre
    krb_ref[...] = jnp.where(first, nyq, kre)
    ki_ref[...] = jnp.where(first, 0.0, kim)


def _hyena_filters(seq, d, w1p, b1, w2, b2, freq, w3):
    zpad, deltas = _filter_consts(seq, d)
    fwd, _, alt = _dft_consts(seq)
    fwd = jnp.asarray(fwd).astype(_BF16)
    top = pl.BlockSpec((seq, seq), lambda o, j: (0, 0), pipeline_mode=pl.Buffered(1))
    bot = pl.BlockSpec((seq, seq), lambda o, j: (1, 0), pipeline_mode=pl.Buffered(1))
    nb = d // _FILT_DBLK
    fo = _HY_FO
    out_sds = jax.ShapeDtypeStruct((_HY_ORDER, seq, d), _F32)
    out_spec = pl.BlockSpec((None, seq, _FILT_DBLK), lambda o, j: (o, 0, j))
    return pl.pallas_call(
        _filter_kernel,
        grid=(_HY_ORDER, nb),
        in_specs=[
            _const_spec((seq, fo)), _const_spec((fo, fo)), _const_spec((1, fo)),
            _const_spec((fo, fo)), _const_spec((1, fo)), _const_spec((1, fo)),
            pl.BlockSpec((fo, _FILT_DBLK), lambda o, j: (0, (2 * o) * nb + j)),
            pl.BlockSpec((fo, _FILT_DBLK), lambda o, j: (0, (2 * o + 1) * nb + j)),
            pl.BlockSpec((1, _FILT_DBLK), lambda o, j: (0, j)),
            top, bot,
            _const_spec((seq, 1)),
        ],
        out_specs=[out_spec, out_spec, out_spec],
        out_shape=[out_sds, out_sds, out_sds],
        scratch_shapes=[pltpu.VMEM((seq, fo), _F32)],
        compiler_params=_params(("arbitrary", "arbitrary")),
        name=f"filt{seq}",
    )(jnp.asarray(zpad), w1p, b1, w2, b2, freq, w3, w3, jnp.asarray(deltas),
      fwd, fwd, jnp.asarray(alt))


def _pre_kernel(has_pos, *refs):
    if has_pos:
        x_ref, pos_ref, sh_ref, sc_ref, g_ref, w_ref, b_ref, z_ref = refs
        x = x_ref[...] + pos_ref[...]
    else:
        x_ref, sh_ref, sc_ref, g_ref, w_ref, b_ref, z_ref = refs
        x = x_ref[...]
    h = _rmsnorm(x, g_ref[...]) * (1.0 + sc_ref[...]) + sh_ref[...]
    z_ref[...] = (_dot_bf16(h, w_ref[...]) + b_ref[...]).astype(z_ref.dtype)


def _pre_hyena(x, pos, mod5, row_of_b, layer, norm_g, w_in, b_in):
    bsz, seq, d = x.shape
    n = w_in.shape[-1]
    tm = _TOKEN_TILE
    tok = pl.BlockSpec((None, tm, d), lambda t, b: (b, t, 0))
    ins, specs = [x], [tok]
    if pos is not None:
        ins.append(pos)
        specs.append(pl.BlockSpec((tm, d), lambda t, b: (t, 0)))
    ins += [mod5, mod5, norm_g, w_in, b_in]
    specs += [_mod_spec(d, layer, 0, row_of_b), _mod_spec(d, layer, 1, row_of_b),
              _layer_spec((1, d), layer), _layer_spec((d, n), 0), _layer_spec((1, n), 0)]
    return pl.pallas_call(
        functools.partial(_pre_kernel, pos is not None),
        grid=(seq // tm, bsz),
        in_specs=specs,
        out_specs=pl.BlockSpec((None, tm, n), lambda t, b: (b, t, 0)),
        out_shape=jax.ShapeDtypeStruct((bsz, seq, n), _BF16),
        compiler_params=_params(("arbitrary", "arbitrary")),
        name=f"pre{seq}",
    )(*ins)


def _conv_kernel(zv_ref, z1_ref, z2_ref, wv_ref, w1_ref, w2_ref, bv_ref, b1_ref, b2_ref, fb_ref,
                 krt_ref, krb_ref, ki_ref, ft_ref, fb2_ref, it_ref, ib_ref, o_ref):
    seq, dblk = zv_ref.shape
    row = lax.broadcasted_iota(jnp.int32, (seq, dblk), 0)
    first = row == 0
    last = row == seq - 1

    def short_conv(z_ref, w_ref, b_ref):
        z = z_ref[...].astype(_F32)
        prev = jnp.where(first, 0.0, pltpu.roll(z, 1, 0))
        nxt = jnp.where(last, 0.0, pltpu.roll(z, seq - 1, 0))
        w = w_ref[...]
        return prev * w[0:1] + z * w[1:2] + nxt * w[2:3] + b_ref[...]

    v = short_conv(zv_ref, wv_ref, bv_ref)
    gates = (short_conv(z1_ref, w1_ref, b1_ref), short_conv(z2_ref, w2_ref, b2_ref))
    for o in range(_HY_ORDER):
        vb = v.astype(_BF16)
        p = jnp.dot(ft_ref[...], vb, preferred_element_type=_F32)
        q = jnp.dot(fb2_ref[...], vb, preferred_element_type=_F32)
        ki = ki_ref[o]
        ytop = p * krt_ref[o] - q * ki
        ybot = p * ki + q * krb_ref[o]
        y = (jnp.dot(it_ref[...], ytop.astype(_BF16), preferred_element_type=_F32)
             + jnp.dot(ib_ref[...], ybot.astype(_BF16), preferred_element_type=_F32))
        v = (y + v * fb_ref[o:o + 1, :]) * gates[o]
    o_ref[...] = v.astype(o_ref.dtype)


def _hyena_conv(z, conv_w, conv_b, fbias, krt, krb, ki):
    bsz, seq, d3 = z.shape
    d = d3 // 3
    dblk = _CONV_DBLK
    nb = d // dblk
    fwd, inv, _ = _dft_consts(seq)
    fwd = jnp.asarray(fwd).astype(_BF16)
    inv = jnp.asarray(inv).astype(_BF16)

    def mspec(r, c):
        return pl.BlockSpec((seq, seq), lambda j, b: (r, c), pipeline_mode=pl.Buffered(1))

    def zspec(k):
        return pl.BlockSpec((None, seq, dblk), lambda j, b: (b, 0, k * nb + j))

    def wspec(rows, k):
        return pl.BlockSpec((rows, dblk), lambda j, b: (0, k * nb + j))

    kspec = pl.BlockSpec((_HY_ORDER, seq, dblk), lambda j, b: (0, 0, j), pipeline_mode=pl.Buffered(1))
    return pl.pallas_call(
        _conv_kernel,
        grid=(nb, bsz),
        in_specs=[zspec(0), zspec(1), zspec(2),
                  wspec(3, 0), wspec(3, 1), wspec(3, 2),
                  wspec(1, 0), wspec(1, 1), wspec(1, 2),
                  wspec(_HY_ORDER, 0),
                  kspec, kspec, kspec,
                  mspec(0, 0), mspec(1, 0), mspec(0, 0), mspec(0, 1)],
        out_specs=pl.BlockSpec((None, seq, dblk), lambda j, b: (b, 0, j)),
        out_shape=jax.ShapeDtypeStruct((bsz, seq, d), _BF16),
        compiler_params=_params(("arbitrary", "arbitrary")),
        name=f"conv{seq}",
    )(z, z, z, conv_w, conv_w, conv_w, conv_b, conv_b, conv_b, fbias, krt, krb, ki,
      fwd, fwd, inv, inv)


def _segment_transpose(xs):
    n = len(xs)
    seg = lax.broadcasted_iota(jnp.int32, xs[0].shape, 1) // _S5_H
    diags = []
    for k in range(n):
        z = xs[-k % n]
        for m in range(1, n):
            z = jnp.where(seg == m, xs[(m - k) % n], z)
        diags.append(pltpu.roll(z, ((n - k) % n) * _S5_H, 1) if k else z)
    ys = []
    for b in range(n):
        y = diags[b]
        for a in range(1, n):
            y = jnp.where(seg == a, diags[(b - a) % n], y)
        ys.append(y)
    return ys


def _to_group_major(x, o_ref, crows):
    sm = jnp.swapaxes(x, 0, 1)
    for j in range(x.shape[-1] // 128):
        for half in range(2):
            per_group = _segment_transpose([sm[half * 8 + s8, :, j * 128:(j + 1) * 128] for s8 in range(8)])
            for gl in range(8):
                o_ref[j * 8 + gl, crows, half * 128:(half + 1) * 128] = per_group[gl]


def _from_group_major(m_ref, crows, d):
    per_step = [[None] * (d // 128) for _ in range(_S5_CHUNK)]
    for j in range(d // 128):
        for half in range(2):
            cols = slice(half * 128, (half + 1) * 128)
            steps = _segment_transpose([m_ref[j * 8 + gl, crows, cols] for gl in range(8)])
            for t8 in range(8):
                per_step[half * 8 + t8][j] = steps[t8]
    sm = jnp.stack([jnp.concatenate(tiles, axis=1) for tiles in per_step])
    return jnp.swapaxes(sm, 0, 1)


def _post_kernel(kind, tail, names, *refs):
    r = dict(zip(names, refs))
    tm, d = r["x"].shape
    steps = _S5_CHUNK
    nsub = max(tm // _POST_SUB, 1)
    sub = tm // nsub

    def stages(i):
        rows = slice(i * sub, (i + 1) * sub)
        crows = slice(i * sub // steps, (i + 1) * sub // steps)
        st = {}

        def head():
            st["x"] = r["x"][rows, :]
            if "pos" in r:
                st["x"] = st["x"] + r["pos"][rows, :]
            if kind == "hyena":
                st["mix"] = r["mix"][rows, :]
            else:
                u = _rmsnorm(st["x"], r["n1"][...]) * (1.0 + r["sc1"][...]) + r["sh1"][...]
                mix = _from_group_major(r["mix"], crows, d).reshape(sub, d)
                st["mix"] = _gelu_tanh(mix + u * r["skip"][...]).astype(_BF16)

        def mixer_out():
            st["m"] = _dot_bf16(st.pop("mix"), r["w_out"][...]) + r["b_out"][...]

        def mid():
            m = st.pop("m")
            if kind == "s5":
                half = m.shape[-1] // 2
                m = m[:, :half] * jax.nn.sigmoid(m[:, half:])
            st["x"] = st["x"] + r["g1"][...] * m
            h = _rmsnorm(st["x"], r["n2"][...]) * (1.0 + r["sc2"][...]) + r["sh2"][...]
            st["h"] = h.astype(_BF16)

        def ffn_in():
            st["ab"] = _dot_bf16(st.pop("h"), r["w13"][...])

        def act():
            ab = st.pop("ab")
            ff = ab.shape[-1] // 2
            st["act"] = (_silu(ab[:, :ff]) * ab[:, ff:]).astype(_BF16)

        def ffn_out():
            st["f"] = _dot_bf16(st.pop("act"), r["w2"][...])

        def finish():
            x = st.pop("x") + r["g2"][...] * st.pop("f")
            if tail == "next":
                r["o_x"][rows, :] = x
                un = _rmsnorm(x, r["nn"][...]) * (1.0 + r["scn"][...]) + r["shn"][...]
                _to_group_major(un.reshape(-1, steps, d), r["o_u"], crows)
            else:
                r["o_x"][rows, :] = _rmsnorm(x, r["nf"][...])

        return [head, mixer_out, mid, ffn_in, act, ffn_out, finish]

    pipes = [stages(i) for i in range(nsub)]
    depth = len(pipes[0])
    for t in range(depth + nsub - 1):
        for i, pipe in enumerate(pipes):
            if 0 <= t - i < depth:
                pipe[t - i]()


def _post_mixer(kind, tail, x, pos, mix, mod5, row_of_b, layer, w_out, b_out, norm1_g, skip,
                norm2_g, w13, w2, final_g):
    bsz, seq, d = x.shape
    tm = min(_POST_TILE, seq)
    tok = pl.BlockSpec((None, tm, d), lambda t, b: (b, t, 0))
    nt = seq // tm
    steps = _S5_CHUNK
    groups = d // _S5_H
    stepmajor = pl.BlockSpec((groups, tm // steps, _CHUNK_W), lambda t, b: (0, b * nt + t, 0))
    stepmajor_sds = jax.ShapeDtypeStruct((groups, bsz * seq // steps, _CHUNK_W), _F32)
    n_out = w_out.shape[-1]
    f2 = w13.shape[-1]

    def mod(part, lyr=layer):
        return (mod5, _mod_spec(d, lyr, part, row_of_b))

    items = [("x", x, tok), ("mix", mix, stepmajor if kind == "s5" else tok),
             ("w_out", w_out, _layer_spec((d, n_out), 0)), ("b_out", b_out, _layer_spec((1, n_out), 0)),
             ("g1",) + mod(2), ("sh2",) + mod(3), ("sc2",) + mod(4), ("g2",) + mod(5),
             ("n2", norm2_g, _layer_spec((1, d), layer)),
             ("w13", w13, _layer_spec((d, f2), layer)), ("w2", w2, _layer_spec((f2 // 2, d), layer))]
    if kind == "s5":
        items += [("n1", norm1_g, _layer_spec((1, d), layer)), ("sh1",) + mod(0), ("sc1",) + mod(1),
                  ("skip", skip, _layer_spec((1, d), 0))]
    if pos is not None:
        items.append(("pos", pos, pl.BlockSpec((tm, d), lambda t, b: (t, 0))))
    out_specs, out_shape = [tok], [jax.ShapeDtypeStruct(x.shape, _F32)]
    names_out = ["o_x"]
    if tail == "next":
        items += [("nn", norm1_g, _layer_spec((1, d), layer + 1)),
                  ("shn",) + mod(0, layer + 1), ("scn",) + mod(1, layer + 1)]
        out_specs.append(stepmajor)
        out_shape.append(stepmajor_sds)
        names_out.append("o_u")
    else:
        items.append(("nf", final_g, _const_spec((1, d))))
    names = tuple(i[0] for i in items) + tuple(names_out)
    out = pl.pallas_call(
        functools.partial(_post_kernel, kind, tail, names),
        grid=(seq // tm, bsz),
        in_specs=[i[2] for i in items], out_specs=out_specs, out_shape=out_shape,
        compiler_params=_params(("arbitrary", "arbitrary")),
        name=f"post_{kind}{seq}",
    )(*[i[1] for i in items])
    return out


_S5_GB = 8
_S5_LANES = 2 * _S5_P
_S5_MATS_GB = 8


def _cmul(ar, ai, br, bi):
    return ar * br - ai * bi, ar * bi + ai * br


def _s5_mats_kernel(prm_ref, btr_ref, bti_ref, cr_ref, ci_ref, causal_ref, anti_ref,
                    m_ref, w_ref, vf_ref, vb_ref, a_ref):
    c, h, lanes = _S5_CHUNK, _S5_H, _S5_LANES
    a_re = prm_ref[0:1, :]
    a_im = prm_ref[1:2, :]
    dt = jnp.exp(prm_ref[2:3, :])
    sr = dt * a_re
    ph = dt * a_im
    mag = jnp.exp(sr)
    nr = mag * jnp.cos(ph) - 1.0
    ni = mag * jnp.sin(ph)
    den = a_re * a_re + a_im * a_im
    co_re = (nr * a_re + ni * a_im) / den
    co_im = (ni * a_re - nr * a_im) / den
    bt = _cmul(co_re, co_im, btr_ref[...], bti_ref[...])
    ct = (cr_ref[...], ci_ref[...])

    def powers(k):
        kf = k.astype(_F32)
        e = jnp.exp(kf * sr)
        return e * jnp.cos(kf * ph), e * jnp.sin(kf * ph)

    def per_step(x):
        return jnp.broadcast_to(x[:, None, :], (c, h, lanes)).reshape(c * h, lanes)

    def per_chan(x):
        return jnp.broadcast_to(x[None, :, :], (c, h, lanes)).reshape(c * h, lanes)

    step = lax.broadcasted_iota(jnp.int32, (c, lanes), 0)
    fwd = lax.broadcasted_iota(jnp.int32, (c, lanes), 1) < _S5_P
    tb = tuple(per_chan(x) for x in bt)
    tc = tuple(per_chan(x) for x in ct)

    def table(base, k_fwd, k_bwd):
        return _cmul(*base, *(per_step(x) for x in powers(jnp.where(fwd, k_fwd, k_bwd))))

    lmat = table(tb, -step, step)
    rmat = table(tc, step, -step)
    wmat = table(tb, c - 1 - step, step)
    vmat = table(tc, step + 1, c - step)

    def nt3(x, y):
        xh, xl = _split_bf16(x)
        yh, yl = _split_bf16(y)
        return _dot_nt(xh, yh) + _dot_nt(xl, yh) + _dot_nt(xh, yl)

    fwd_rows = lax.broadcasted_iota(jnp.int32, (c * h, lanes), 1) < _S5_P

    def kernel_half(keep):
        return (nt3(jnp.where(keep, lmat[0], 0.0), rmat[0])
                - nt3(jnp.where(keep, lmat[1], 0.0), rmat[1]))

    m = causal_ref[...] * kernel_half(fwd_rows) + anti_ref[...] * kernel_half(jnp.logical_not(fwd_rows))
    m_ref[...] = m.astype(m_ref.dtype)
    w_ref[...] = jnp.concatenate([wmat[0], wmat[1]], axis=1).astype(w_ref.dtype)
    vcat = jnp.concatenate([vmat[0], -vmat[1]], axis=1)
    keep_f = (lax.broadcasted_iota(jnp.int32, vcat.shape, 1) % lanes) < _S5_P
    vf_ref[...] = jnp.where(keep_f, vcat, 0.0).T.astype(vf_ref.dtype)
    vb_ref[...] = jnp.where(keep_f, 0.0, vcat).T.astype(vb_ref.dtype)
    chunk_k = jnp.full((1, lanes), c, jnp.int32)
    a_chunk = powers(chunk_k)
    a_ref[0:1, :] = a_chunk[0]
    a_ref[1:2, :] = a_chunk[1]


def _s5_mats_block_kernel(prm_ref, btr_ref, bti_ref, cr_ref, ci_ref, causal_ref, anti_ref,
                          m_ref, w_ref, vf_ref, vb_ref, a_ref):
    for j in range(prm_ref.shape[0]):
        _s5_mats_kernel(prm_ref.at[j], btr_ref.at[j], bti_ref.at[j], cr_ref.at[j], ci_ref.at[j],
                        causal_ref, anti_ref,
                        m_ref.at[j], w_ref.at[j], vf_ref.at[j], vb_ref.at[j], a_ref.at[j])


def _s5_mats(prm, bt_re, bt_im, c_re, c_im):
    g = prm.shape[0]
    cw, lanes = _CHUNK_W, _S5_LANES
    causal, anti = (jnp.asarray(a) for a in _s5_masks())
    gm = _S5_MATS_GB
    gspec = pl.BlockSpec((gm, _S5_H, lanes), lambda i: (i, 0, 0))
    mat_spec = pl.BlockSpec((gm, cw, cw), lambda i: (i, 0, 0))
    mat_sds = jax.ShapeDtypeStruct((g, cw, cw), _BF16)
    return pl.pallas_call(
        _s5_mats_block_kernel,
        grid=(g // gm,),
        in_specs=[pl.BlockSpec((gm, 3, lanes), lambda i: (i, 0, 0)), gspec, gspec, gspec, gspec,
                  _const_spec((cw, cw)), _const_spec((cw, cw))],
        out_specs=[mat_spec] * 4 + [pl.BlockSpec((gm, 2, lanes), lambda i: (i, 0, 0))],
        out_shape=[mat_sds] * 4 + [jax.ShapeDtypeStruct((g, 2, lanes), _F32)],
        compiler_params=_params(("arbitrary",)),
        name="s5_mats",
    )(prm, bt_re, bt_im, c_re, c_im, causal, anti)


def _s5_scan_kernel(nchunk, bsz, has_init, *refs):
    if has_init:
        u_ref, m_ref, w_ref, vf_ref, vb_ref, a_ref, h0_ref, y_ref = refs[:8]
    else:
        u_ref, m_ref, w_ref, vf_ref, vb_ref, a_ref, y_ref, fin_ref = refs[:8]
    s_ref, pf_ref, pb_ref = refs[8:]
    lanes, gb = _S5_LANES, _S5_GB
    rows = nchunk * bsz
    fwd = lax.broadcasted_iota(jnp.int32, (bsz, lanes), 1) < _S5_P

    for gl in range(gb):
        u = u_ref[gl].astype(_BF16)
        y_ref[gl] = jnp.dot(u, m_ref[gl], preferred_element_type=_F32)
        s = jnp.dot(u, w_ref[gl], preferred_element_type=_F32)
        s_ref[...] = jnp.swapaxes(s.reshape(bsz, nchunk, 2 * lanes), 0, 1).reshape(rows, 2 * lanes)

        a_re = a_ref[gl, 0:1, :]
        a_im = a_ref[gl, 1:2, :]
        if has_init:
            init = (h0_ref[gl, :, 0:lanes], h0_ref[gl, :, lanes:2 * lanes])
        else:
            init = (jnp.zeros((bsz, lanes), _F32), jnp.zeros((bsz, lanes), _F32))

        def step(k, carry):
            h_re, h_im = carry
            rowf = pl.ds(pl.multiple_of(k * bsz, bsz), bsz)
            rowb = pl.ds(pl.multiple_of((nchunk - 1 - k) * bsz, bsz), bsz)
            pf_ref[rowf, 0:lanes] = h_re
            pf_ref[rowf, lanes:2 * lanes] = h_im
            pb_ref[rowb, 0:lanes] = h_re
            pb_ref[rowb, lanes:2 * lanes] = h_im
            s_re = jnp.where(fwd, s_ref[rowf, 0:lanes], s_ref[rowb, 0:lanes])
            s_im = jnp.where(fwd, s_ref[rowf, lanes:2 * lanes], s_ref[rowb, lanes:2 * lanes])
            return (a_re * h_re - a_im * h_im + s_re, a_re * h_im + a_im * h_re + s_im)

        fin = lax.fori_loop(0, nchunk, step, init)
        if not has_init:
            fin_ref[gl, :, 0:lanes] = fin[0]
            fin_ref[gl, :, lanes:2 * lanes] = fin[1]
        carried = (jnp.dot(pf_ref[...].astype(_BF16), vf_ref[gl], preferred_element_type=_F32)
                   + jnp.dot(pb_ref[...].astype(_BF16), vb_ref[gl], preferred_element_type=_F32))
        carried = jnp.swapaxes(carried.reshape(nchunk, bsz, 2 * lanes), 0, 1).reshape(rows, 2 * lanes)
        y_ref[gl] = y_ref[gl] + carried


def _s5_scan(u, bsz, mats, init):
    g, rows, cw = u.shape
    nchunk = rows // bsz
    gb = _S5_GB
    tok_spec = pl.BlockSpec((gb, rows, cw), lambda i: (i, 0, 0))
    mat_spec = pl.BlockSpec((gb, cw, cw), lambda i: (i, 0, 0))
    st_spec = pl.BlockSpec((gb, bsz, cw), lambda i: (i, 0, 0))
    ins = [u] + list(mats)
    specs = [tok_spec] + [mat_spec] * 4 + [pl.BlockSpec((gb, 2, _S5_LANES), lambda i: (i, 0, 0))]
    y_sds = jax.ShapeDtypeStruct(u.shape, _F32)
    if init is not None:
        ins.append(init)
        specs.append(st_spec)
        out_specs, out_shape = tok_spec, y_sds
    else:
        out_specs = [tok_spec, st_spec]
        out_shape = [y_sds, jax.ShapeDtypeStruct((g, bsz, cw), _F32)]
    out = pl.pallas_call(
        functools.partial(_s5_scan_kernel, nchunk, bsz, init is not None),
        grid=(g // gb,),
        in_specs=specs, out_specs=out_specs, out_shape=out_shape,
        scratch_shapes=[pltpu.VMEM((rows, cw), _F32)] * 3,
        compiler_params=_params(("arbitrary",)),
        name=f"s5_{nchunk * _S5_CHUNK}",
    )(*ins)
    if init is not None:
        return out, None
    return out[0], out[1]


def _trunk(x, pos, row_of_b, shared_mod, init_state, mod5, filt, s5_mats, wts):
    bsz, seq, d = x.shape
    fold = _POST_TILE // seq if (shared_mod and seq < _POST_TILE) else 1
    rows = lambda a: a.reshape(bsz // fold, seq * fold, a.shape[-1])
    seqs = lambda a: a.reshape(bsz, seq, a.shape[-1])
    z = _pre_hyena(rows(x), pos, mod5, row_of_b, 0, wts["norm1_g"], wts["hy_in_w"], wts["hy_in_b"])
    v = _hyena_conv(seqs(z), wts["hy_conv_w"], wts["hy_conv_b"], wts["hy_fbias"], *filt)
    x1, u = _post_mixer("hyena", "next", rows(x), pos, rows(v), mod5, row_of_b, 0, wts["hy_out_w"],
                        wts["hy_out_b"], wts["norm1_g"], None, wts["norm2_g"], wts["ffn_w13"], wts["ffn_w2"],
                        None)
    ys, fin = _s5_scan(u, bsz, s5_mats, init_state)
    (out,) = _post_mixer("s5", "final", x1, None, ys, mod5, row_of_b, 1, wts["s5_glu_w"], wts["s5_glu_b"],
                         wts["norm1_g"], wts["s5_D"], wts["norm2_g"], wts["ffn_w13"], wts["ffn_w2"],
                         wts["final_g"])
    return seqs(out), fin


def kernel(x_prompt, x_sample, state_s5, c, c_ctx, norm1_g, norm2_g, final_g, ada_w, ada_b, ffn_w13, ffn_w2, hy_in_w, hy_in_b, hy_conv_w, hy_conv_b, hy_pe_w1, hy_pe_b1, hy_pe_w2, hy_pe_b2, hy_pe_w3, hy_freq, hy_fbias, hy_out_w, hy_out_b, s5_A_re, s5_A_im, s5_log_dt, s5_B_re, s5_B_im, s5_C_re, s5_C_im, s5_D, s5_glu_w, s5_glu_b):
    depth, d = norm1_g.shape
    assert depth == 2 and hy_in_w.shape[0] == 1 and s5_glu_w.shape[0] == 1
    dec_b, dec_seq, _ = x_sample.shape
    g = d // _S5_H
    p = _S5_P

    nrow = -(-(1 + dec_b) // 8) * 8
    cond = jnp.concatenate([c_ctx[None], c, jnp.zeros((nrow - 1 - dec_b, d), _F32)], axis=0)
    mod5 = _modulation(cond, ada_w, ada_b)

    row3 = lambda a: a.reshape(a.shape[0], 1, a.shape[-1])
    wts = dict(
        norm1_g=row3(norm1_g), norm2_g=row3(norm2_g), final_g=final_g[None],
        hy_in_w=hy_in_w.astype(_BF16), hy_in_b=row3(hy_in_b),
        hy_conv_w=hy_conv_w[0], hy_conv_b=hy_conv_b, hy_fbias=hy_fbias[0],
        hy_out_w=hy_out_w.astype(_BF16), hy_out_b=row3(hy_out_b),
        ffn_w13=ffn_w13.astype(_BF16), ffn_w2=ffn_w2.astype(_BF16),
        s5_glu_w=s5_glu_w.astype(_BF16), s5_glu_b=row3(s5_glu_b), s5_D=row3(s5_D),
    )
    dirs_on_lanes = lambda a: a.transpose(1, 0, 2).reshape(g, 2 * p)
    ldt = jnp.broadcast_to(s5_log_dt[0][:, :, None], (2, g, p))
    prm = jnp.stack([dirs_on_lanes(s5_A_re[0]), dirs_on_lanes(s5_A_im[0]), dirs_on_lanes(ldt)], axis=1)
    bt = lambda a: a.transpose(1, 3, 0, 2).reshape(g, _S5_H, 2 * p)
    ct = lambda a: a.transpose(1, 2, 0, 3).reshape(g, _S5_H, 2 * p)
    s5_mats = _s5_mats(prm, bt(s5_B_re[0]), bt(s5_B_im[0]), ct(s5_C_re[0]), ct(s5_C_im[0]))

    w1p = jnp.pad(hy_pe_w1[0], ((0, _HY_FO - _HY_EMB), (0, 0)))
    filt_args = (w1p, hy_pe_b1, hy_pe_w2[0], hy_pe_b2, hy_freq, hy_pe_w3[0])
    filt_ctx = _hyena_filters(x_prompt.shape[1], d, *filt_args)
    filt_lat = _hyena_filters(dec_seq, d, *filt_args)

    y_prompt, fin = _trunk(x_prompt, None, lambda b: 0, True, None, mod5, filt_ctx, s5_mats, wts)
    new_state = fin.reshape(g, -1, 2, 2, p).transpose(1, 3, 2, 0, 4)[:, None]

    init = state_s5[:, 0].transpose(3, 0, 2, 1, 4).reshape(g, dec_b, 4 * p)
    pos = jnp.asarray(_pos_embed(dec_seq // _GRID_W, d))
    y_sample, _ = _trunk(x_sample, pos, lambda b: b + 1, False, init, mod5, filt_lat, s5_mats, wts)
    return (y_prompt, y_sample, new_state)
```

```python
import functools
import math

import numpy as np
import jax
import jax.numpy as jnp
from jax import lax
from jax.experimental import pallas as pl
from jax.experimental.pallas import tpu as pltpu

_F32 = jnp.float32
_BF16 = jnp.bfloat16

_EPS = 1e-6
_GRID_W = 64
_POS_BASE = 10000.0
_HY_ORDER = 2
_HY_EMB = 33
_HY_FO = 64
_HY_TARGET = 1e-2
_HY_FAST = 0.3
_HY_SLOW = 1.5
_S5_H = 16
_S5_P = 64
_S5_CHUNK = 16
_CHUNK_W = _S5_CHUNK * _S5_H

_VMEM_LIMIT = 56 * 1024 * 1024
_TOKEN_TILE = 512
_POST_TILE = 512
_POST_SUB = 128
_CONV_DBLK = 512
_FILT_DBLK = 512
_MOD_PARTS = 2


def _params(sem):
    return pltpu.CompilerParams(dimension_semantics=sem, vmem_limit_bytes=_VMEM_LIMIT)


def _dot_bf16(a, b):
    return jnp.dot(a.astype(_BF16), b.astype(_BF16), preferred_element_type=_F32)


def _split_bf16(x):
    hi = x.astype(_BF16)
    return hi, (x - hi.astype(_F32)).astype(_BF16)


def _dot_3pass(a, b):
    ah, al = _split_bf16(a)
    bh, bl = _split_bf16(b)
    return (jnp.dot(ah, bh, preferred_element_type=_F32) + jnp.dot(al, bh, preferred_element_type=_F32)
            + jnp.dot(ah, bl, preferred_element_type=_F32))


def _dot_nt(a, b, precision=None):
    return lax.dot_general(a, b, (((1,), (1,)), ((), ())), precision=precision,
                           preferred_element_type=_F32)


def _rmsnorm(x, g):
    ms = jnp.mean(x * x, axis=-1, keepdims=True)
    return x * lax.rsqrt(ms + _EPS) * g


def _silu(x):
    return x * jax.nn.sigmoid(x)


def _gelu_tanh(x):
    c = math.sqrt(2.0 / math.pi)
    return 0.5 * x * (1.0 + jnp.tanh(c * (x + 0.044715 * (x * x * x))))


def _const_spec(shape):
    nd = len(shape)
    return pl.BlockSpec(shape, lambda *_: (0,) * nd, pipeline_mode=pl.Buffered(1))


def _layer_spec(shape, layer):
    nd = len(shape)
    return pl.BlockSpec((None,) + tuple(shape), lambda *_: (layer,) + (0,) * nd,
                        pipeline_mode=pl.Buffered(1))


def _mod_spec(d, layer, part, row_of_b):
    return pl.BlockSpec((None, None, None, 1, d), lambda t, b: (layer, part, row_of_b(b), 0, 0))


def _mod_kernel(cond_ref, w_ref, b_ref, o_ref):
    d = o_ref.shape[-1]
    r = _dot_3pass(_silu(cond_ref[...]), w_ref[...])
    for p in range(o_ref.shape[0]):
        o_ref[p] = r[:, p * d:(p + 1) * d] + b_ref[p]


def _modulation(cond, ada_w, ada_b):
    depth, d, d6 = ada_w.shape
    parts = d6 // d
    rows = cond.shape[0]
    pb = _MOD_PARTS
    out = pl.pallas_call(
        _mod_kernel,
        grid=(depth, parts // pb),
        in_specs=[
            pl.BlockSpec((rows, d), lambda l, p: (0, 0)),
            pl.BlockSpec((None, d, pb * d), lambda l, p: (l, 0, p)),
            pl.BlockSpec((None, pb, 1, d), lambda l, p: (l, p, 0, 0)),
        ],
        out_specs=pl.BlockSpec((None, pb, rows, d), lambda l, p: (l, p, 0, 0)),
        out_shape=jax.ShapeDtypeStruct((depth, parts, rows, d), _F32),
        compiler_params=_params(("arbitrary", "arbitrary")),
        name="mod",
    )(cond, ada_w, ada_b.reshape(depth, parts, 1, d))
    return out.reshape(depth, parts, rows, 1, d)


@functools.lru_cache(maxsize=None)
def _dft_consts(seq):
    f = np.arange(seq)[:, None]
    t = np.arange(seq)[None, :]
    ang = np.pi * ((f * t) % (2 * seq)) / seq
    cos = np.cos(ang)
    sin = np.sin(ang)
    alt = np.where(np.arange(seq) % 2 == 0, 1.0, -1.0)
    fwd_top = cos
    fwd_bot = -sin
    fwd_bot[0] = alt
    wgt = np.full((seq, 1), 2.0)
    wgt[0] = 1.0
    inv_top = (cos * wgt).T / (2 * seq)
    inv_bot = (-2.0 * sin).T / (2 * seq)
    inv_bot[:, 0] = alt / (2 * seq)
    fwd = np.concatenate([fwd_top, fwd_bot], axis=0).astype(np.float32)
    inv = np.concatenate([inv_top, inv_bot], axis=1).astype(np.float32)
    return fwd, inv, alt.astype(np.float32)[:, None]


@functools.lru_cache(maxsize=None)
def _filter_consts(seq, d):
    t = np.linspace(0.0, 1.0, seq)[:, None]
    w = 2.0 * np.pi * np.arange(seq)[:, None] / seq
    nb = (_HY_EMB - 1) // 2
    bands = np.linspace(1e-4, nb - 1, nb)[None, :]
    z = np.concatenate([t, np.cos(bands * w), -np.sin(bands * w)], axis=-1)
    zpad = np.zeros((seq, _HY_FO))
    zpad[:, :_HY_EMB] = z
    max_decay = math.log(_HY_TARGET) / _HY_FAST
    min_decay = math.log(_HY_TARGET) / _HY_SLOW
    deltas = np.abs(np.linspace(min_decay, max_decay, d))[None, :]
    return zpad.astype(np.float32), deltas.astype(np.float32)


@functools.lru_cache(maxsize=None)
def _pos_embed(rows, d):
    quarter = d // 4
    omega = 1.0 / (_POS_BASE ** (np.arange(quarter, dtype=np.float64) / quarter))

    def axis_embed(n):
        ang = np.arange(n, dtype=np.float64)[:, None] * omega[None]
        return np.concatenate([np.sin(ang), np.cos(ang)], axis=-1)

    er = np.broadcast_to(axis_embed(rows)[:, None], (rows, _GRID_W, d // 2))
    ec = np.broadcast_to(axis_embed(_GRID_W)[None], (rows, _GRID_W, d // 2))
    return np.concatenate([er, ec], axis=-1).reshape(rows * _GRID_W, d).astype(np.float32)


@functools.lru_cache(maxsize=None)
def _s5_masks():
    step = np.repeat(np.arange(_S5_CHUNK), _S5_H)
    causal = (step[None, :] >= step[:, None]).astype(np.float32)
    anti = (step[:, None] >= step[None, :]).astype(np.float32)
    return causal, anti


def _filter_kernel(z_ref, w1_ref, b1_ref, w2_ref, b2_ref, fr_ref, w3f_ref, w3b_ref, dl_ref,
                   at_ref, ab_ref, alt_ref, krt_ref, krb_ref, ki_ref, h_ref):
    @pl.when((pl.program_id(0) == 0) & (pl.program_id(1) == 0))
    def _():
        fr = fr_ref[...]
        h1 = jnp.sin(fr * (_dot_3pass(z_ref[...], w1_ref[...]) + b1_ref[...]))
        h_ref[...] = jnp.sin(fr * (_dot_3pass(h1, w2_ref[...]) + b2_ref[...]))

    h = h_ref[...]
    decay = jnp.exp(-z_ref[:, 0:1] * dl_ref[...])
    hf = _dot_3pass(h, w3f_ref[...]) * decay
    hb = _dot_3pass(h, w3b_ref[...]) * decay
    norm = (jnp.sum(jnp.abs(hf), axis=0, keepdims=True)
            + jnp.sum(jnp.abs(hb), axis=0, keepdims=True) + _EPS)
    hf = hf / norm
    hb = hb / norm
    first = lax.broadcasted_iota(jnp.int32, hf.shape, 0) == 0
    hb = jnp.where(first, 0.0, hb)
    ksum = hf + hb
    kdiff = hf - hb
    kre = jnp.dot(at_ref[...], ksum.astype(_BF16), preferred_element_type=_F32)
    kim = jnp.dot(ab_ref[...], kdiff.astype(_BF16), preferred_element_type=_F32)
    nyq = jnp.sum(alt_ref[...] * ksum, axis=0, keepdims=True)
    krt_ref[...] = kre
    krb_ref[...] = jnp.where(first, nyq, kre)
    ki_ref[...] = jnp.where(first, 0.0, kim)


def _hyena_filters(seq, d, w1p, b1, w2, b2, freq, w3):
    zpad, deltas = _filter_consts(seq, d)
    fwd, _, alt = _dft_consts(seq)
    fwd = jnp.asarray(fwd).astype(_BF16)
    top = pl.BlockSpec((seq, seq), lambda o, j: (0, 0), pipeline_mode=pl.Buffered(1))
    bot = pl.BlockSpec((seq, seq), lambda o, j: (1, 0), pipeline_mode=pl.Buffered(1))
    nb = d // _FILT_DBLK
    fo = _HY_FO
    out_sds = jax.ShapeDtypeStruct((_HY_ORDER, seq, d), _F32)
    out_spec = pl.BlockSpec((None, seq, _FILT_DBLK), lambda o, j: (o, 0, j))
    return pl.pallas_call(
        _filter_kernel,
        grid=(_HY_ORDER, nb),
        in_specs=[
            _const_spec((seq, fo)), _const_spec((fo, fo)), _const_spec((1, fo)),
            _const_spec((fo, fo)), _const_spec((1, fo)), _const_spec((1, fo)),
            pl.BlockSpec((fo, _FILT_DBLK), lambda o, j: (0, (2 * o) * nb + j)),
            pl.BlockSpec((fo, _FILT_DBLK), lambda o, j: (0, (2 * o + 1) * nb + j)),
            pl.BlockSpec((1, _FILT_DBLK), lambda o, j: (0, j)),
            top, bot,
            _const_spec((seq, 1)),
        ],
        out_specs=[out_spec, out_spec, out_spec],
        out_shape=[out_sds, out_sds, out_sds],
        scratch_shapes=[pltpu.VMEM((seq, fo), _F32)],
        compiler_params=_params(("arbitrary", "arbitrary")),
        name=f"filt{seq}",
    )(jnp.asarray(zpad), w1p, b1, w2, b2, freq, w3, w3, jnp.asarray(deltas),
      fwd, fwd, jnp.asarray(alt))


def _pre_kernel(has_pos, *refs):
    if has_pos:
        x_ref, pos_ref, sh_ref, sc_ref, g_ref, w_ref, b_ref, z_ref = refs
        x = x_ref[...] + pos_ref[...]
    else:
        x_ref, sh_ref, sc_ref, g_ref, w_ref, b_ref, z_ref = refs
        x = x_ref[...]
    h = _rmsnorm(x, g_ref[...]) * (1.0 + sc_ref[...]) + sh_ref[...]
    z_ref[...] = (_dot_bf16(h, w_ref[...]) + b_ref[...]).astype(z_ref.dtype)


def _pre_hyena(x, pos, mod5, row_of_b, layer, norm_g, w_in, b_in):
    bsz, seq, d = x.shape
    n = w_in.shape[-1]
    tm = _TOKEN_TILE
    tok = pl.BlockSpec((None, tm, d), lambda t, b: (b, t, 0))
    ins, specs = [x], [tok]
    if pos is not None:
        ins.append(pos)
        specs.append(pl.BlockSpec((tm, d), lambda t, b: (t, 0)))
    ins += [mod5, mod5, norm_g, w_in, b_in]
    specs += [_mod_spec(d, layer, 0, row_of_b), _mod_spec(d, layer, 1, row_of_b),
              _layer_spec((1, d), layer), _layer_spec((d, n), 0), _layer_spec((1, n), 0)]
    return pl.pallas_call(
        functools.partial(_pre_kernel, pos is not None),
        grid=(seq // tm, bsz),
        in_specs=specs,
        out_specs=pl.BlockSpec((None, tm, n), lambda t, b: (b, t, 0)),
        out_shape=jax.ShapeDtypeStruct((bsz, seq, n), _BF16),
        compiler_params=_params(("arbitrary", "arbitrary")),
        name=f"pre{seq}",
    )(*ins)


def _conv_kernel(zv_ref, z1_ref, z2_ref, wv_ref, w1_ref, w2_ref, bv_ref, b1_ref, b2_ref, fb_ref,
                 krt_ref, krb_ref, ki_ref, ft_ref, fb2_ref, it_ref, ib_ref, o_ref):
    seq, dblk = zv_ref.shape
    row = lax.broadcasted_iota(jnp.int32, (seq, dblk), 0)
    first = row == 0
    last = row == seq - 1

    def short_conv(z_ref, w_ref, b_ref):
        z = z_ref[...].astype(_F32)
        prev = jnp.where(first, 0.0, pltpu.roll(z, 1, 0))
        nxt = jnp.where(last, 0.0, pltpu.roll(z, seq - 1, 0))
        w = w_ref[...]
        return prev * w[0:1] + z * w[1:2] + nxt * w[2:3] + b_ref[...]

    v = short_conv(zv_ref, wv_ref, bv_ref)
    gates = (short_conv(z1_ref, w1_ref, b1_ref), short_conv(z2_ref, w2_ref, b2_ref))
    for o in range(_HY_ORDER):
        vb = v.astype(_BF16)
        p = jnp.dot(ft_ref[...], vb, preferred_element_type=_F32)
        q = jnp.dot(fb2_ref[...], vb, preferred_element_type=_F32)
        ki = ki_ref[o]
        ytop = p * krt_ref[o] - q * ki
        ybot = p * ki + q * krb_ref[o]
        y = (jnp.dot(it_ref[...], ytop.astype(_BF16), preferred_element_type=_F32)
             + jnp.dot(ib_ref[...], ybot.astype(_BF16), preferred_element_type=_F32))
        v = (y + v * fb_ref[o:o + 1, :]) * gates[o]
    o_ref[...] = v.astype(o_ref.dtype)


def _hyena_conv(z, conv_w, conv_b, fbias, krt, krb, ki):
    bsz, seq, d3 = z.shape
    d = d3 // 3
    dblk = _CONV_DBLK
    nb = d // dblk
    fwd, inv, _ = _dft_consts(seq)
    fwd = jnp.asarray(fwd).astype(_BF16)
    inv = jnp.asarray(inv).astype(_BF16)

    def mspec(r, c):
        return pl.BlockSpec((seq, seq), lambda j, b: (r, c), pipeline_mode=pl.Buffered(1))

    def zspec(k):
        return pl.BlockSpec((None, seq, dblk), lambda j, b: (b, 0, k * nb + j))

    def wspec(rows, k):
        return pl.BlockSpec((rows, dblk), lambda j, b: (0, k * nb + j))

    kspec = pl.BlockSpec((_HY_ORDER, seq, dblk), lambda j, b: (0, 0, j))
    return pl.pallas_call(
        _conv_kernel,
        grid=(nb, bsz),
        in_specs=[zspec(0), zspec(1), zspec(2),
                  wspec(3, 0), wspec(3, 1), wspec(3, 2),
                  wspec(1, 0), wspec(1, 1), wspec(1, 2),
                  wspec(_HY_ORDER, 0),
                  kspec, kspec, kspec,
                  mspec(0, 0), mspec(1, 0), mspec(0, 0), mspec(0, 1)],
        out_specs=pl.BlockSpec((None, seq, dblk), lambda j, b: (b, 0, j)),
        out_shape=jax.ShapeDtypeStruct((bsz, seq, d), _BF16),
        compiler_params=_params(("arbitrary", "arbitrary")),
        name=f"conv{seq}",
    )(z, z, z, conv_w, conv_w, conv_w, conv_b, conv_b, conv_b, fbias, krt, krb, ki,
      fwd, fwd, inv, inv)


def _segment_transpose(xs):
    n = len(xs)
    seg = lax.broadcasted_iota(jnp.int32, xs[0].shape, 1) // _S5_H
    diags = []
    for k in range(n):
        z = xs[-k % n]
        for m in range(1, n):
            z = jnp.where(seg == m, xs[(m - k) % n], z)
        diags.append(pltpu.roll(z, ((n - k) % n) * _S5_H, 1) if k else z)
    ys = []
    for b in range(n):
        y = diags[b]
        for a in range(1, n):
            y = jnp.where(seg == a, diags[(b - a) % n], y)
        ys.append(y)
    return ys


def _to_group_major(x, o_ref, crows):
    sm = jnp.swapaxes(x, 0, 1)
    for j in range(x.shape[-1] // 128):
        for half in range(2):
            per_group = _segment_transpose([sm[half * 8 + s8, :, j * 128:(j + 1) * 128] for s8 in range(8)])
            for gl in range(8):
                o_ref[j * 8 + gl, crows, half * 128:(half + 1) * 128] = per_group[gl]


def _from_group_major(m_ref, crows, d):
    per_step = [[None] * (d // 128) for _ in range(_S5_CHUNK)]
    for j in range(d // 128):
        for half in range(2):
            cols = slice(half * 128, (half + 1) * 128)
            steps = _segment_transpose([m_ref[j * 8 + gl, crows, cols] for gl in range(8)])
            for t8 in range(8):
                per_step[half * 8 + t8][j] = steps[t8]
    sm = jnp.stack([jnp.concatenate(tiles, axis=1) for tiles in per_step])
    return jnp.swapaxes(sm, 0, 1)


def _post_kernel(kind, tail, names, *refs):
    r = dict(zip(names, refs))
    tm, d = r["x"].shape
    steps = _S5_CHUNK
    nsub = max(tm // _POST_SUB, 1)
    sub = tm // nsub

    def stages(i):
        rows = slice(i * sub, (i + 1) * sub)
        crows = slice(i * sub // steps, (i + 1) * sub // steps)
        st = {}

        def head():
            st["x"] = r["x"][rows, :]
            if "pos" in r:
                st["x"] = st["x"] + r["pos"][rows, :]
            if kind == "hyena":
                st["mix"] = r["mix"][rows, :]
            else:
                u = _rmsnorm(st["x"], r["n1"][...]) * (1.0 + r["sc1"][...]) + r["sh1"][...]
                mix = _from_group_major(r["mix"], crows, d).reshape(sub, d)
                st["mix"] = _gelu_tanh(mix + u * r["skip"][...]).astype(_BF16)

        def mixer_out():
            st["m"] = _dot_bf16(st.pop("mix"), r["w_out"][...]) + r["b_out"][...]

        def mid():
            m = st.pop("m")
            if kind == "s5":
                half = m.shape[-1] // 2
                m = m[:, :half] * jax.nn.sigmoid(m[:, half:])
            st["x"] = st["x"] + r["g1"][...] * m
            h = _rmsnorm(st["x"], r["n2"][...]) * (1.0 + r["sc2"][...]) + r["sh2"][...]
            st["h"] = h.astype(_BF16)

        def ffn_in():
            st["ab"] = _dot_bf16(st.pop("h"), r["w13"][...])

        def act():
            ab = st.pop("ab")
            ff = ab.shape[-1] // 2
            st["act"] = (_silu(ab[:, :ff]) * ab[:, ff:]).astype(_BF16)

        def ffn_out():
            st["f"] = _dot_bf16(st.pop("act"), r["w2"][...])

        def finish():
            x = st.pop("x") + r["g2"][...] * st.pop("f")
            if tail == "next":
                r["o_x"][rows, :] = x
                un = _rmsnorm(x, r["nn"][...]) * (1.0 + r["scn"][...]) + r["shn"][...]
                _to_group_major(un.reshape(-1, steps, d), r["o_u"], crows)
            else:
                r["o_x"][rows, :] = _rmsnorm(x, r["nf"][...])

        return [head, mixer_out, mid, ffn_in, act, ffn_out, finish]

    pipes = [stages(i) for i in range(nsub)]
    depth = len(pipes[0])
    for t in range(depth + nsub - 1):
        for i, pipe in enumerate(pipes):
            if 0 <= t - i < depth:
                pipe[t - i]()


def _post_mixer(kind, tail, x, pos, mix, mod5, row_of_b, layer, w_out, b_out, norm1_g, skip,
                norm2_g, w13, w2, final_g):
    bsz, seq, d = x.shape
    tm = min(_POST_TILE, seq)
    tok = pl.BlockSpec((None, tm, d), lambda t, b: (b, t, 0))
    nt = seq // tm
    steps = _S5_CHUNK
    groups = d // _S5_H
    stepmajor = pl.BlockSpec((groups, tm // steps, _CHUNK_W), lambda t, b: (0, b * nt + t, 0))
    stepmajor_sds = jax.ShapeDtypeStruct((groups, bsz * seq // steps, _CHUNK_W), _F32)
    n_out = w_out.shape[-1]
    f2 = w13.shape[-1]

    def mod(part, lyr=layer):
        return (mod5, _mod_spec(d, lyr, part, row_of_b))

    items = [("x", x, tok), ("mix", mix, stepmajor if kind == "s5" else tok),
             ("w_out", w_out, _layer_spec((d, n_out), 0)), ("b_out", b_out, _layer_spec((1, n_out), 0)),
             ("g1",) + mod(2), ("sh2",) + mod(3), ("sc2",) + mod(4), ("g2",) + mod(5),
             ("n2", norm2_g, _layer_spec((1, d), layer)),
             ("w13", w13, _layer_spec((d, f2), layer)), ("w2", w2, _layer_spec((f2 // 2, d), layer))]
    if kind == "s5":
        items += [("n1", norm1_g, _layer_spec((1, d), layer)), ("sh1",) + mod(0), ("sc1",) + mod(1),
                  ("skip", skip, _layer_spec((1, d), 0))]
    if pos is not None:
        items.append(("pos", pos, pl.BlockSpec((tm, d), lambda t, b: (t, 0))))
    out_specs, out_shape = [tok], [jax.ShapeDtypeStruct(x.shape, _F32)]
    names_out = ["o_x"]
    if tail == "next":
        items += [("nn", norm1_g, _layer_spec((1, d), layer + 1)),
                  ("shn",) + mod(0, layer + 1), ("scn",) + mod(1, layer + 1)]
        out_specs.append(stepmajor)
        out_shape.append(stepmajor_sds)
        names_out.append("o_u")
    else:
        items.append(("nf", final_g, _const_spec((1, d))))
    names = tuple(i[0] for i in items) + tuple(names_out)
    out = pl.pallas_call(
        functools.partial(_post_kernel, kind, tail, names),
        grid=(seq // tm, bsz),
        in_specs=[i[2] for i in items], out_specs=out_specs, out_shape=out_shape,
        compiler_params=_params(("arbitrary", "arbitrary")),
        name=f"post_{kind}{seq}",
    )(*[i[1] for i in items])
    return out


_S5_GB = 8
_S5_LANES = 2 * _S5_P
_S5_MATS_GB = 8


def _cmul(ar, ai, br, bi):
    return ar * br - ai * bi, ar * bi + ai * br


def _s5_mats_kernel(prm_ref, btr_ref, bti_ref, cr_ref, ci_ref, causal_ref, anti_ref,
                    m_ref, w_ref, vf_ref, vb_ref, a_ref):
    c, h, lanes = _S5_CHUNK, _S5_H, _S5_LANES
    a_re = prm_ref[0:1, :]
    a_im = prm_ref[1:2, :]
    dt = jnp.exp(prm_ref[2:3, :])
    sr = dt * a_re
    ph = dt * a_im
    mag = jnp.exp(sr)
    nr = mag * jnp.cos(ph) - 1.0
    ni = mag * jnp.sin(ph)
    den = a_re * a_re + a_im * a_im
    co_re = (nr * a_re + ni * a_im) / den
    co_im = (ni * a_re - nr * a_im) / den
    bt = _cmul(co_re, co_im, btr_ref[...], bti_ref[...])
    ct = (cr_ref[...], ci_ref[...])

    def powers(k):
        kf = k.astype(_F32)
        e = jnp.exp(kf * sr)
        return e * jnp.cos(kf * ph), e * jnp.sin(kf * ph)

    def per_step(x):
        return jnp.broadcast_to(x[:, None, :], (c, h, lanes)).reshape(c * h, lanes)

    def per_chan(x):
        return jnp.broadcast_to(x[None, :, :], (c, h, lanes)).reshape(c * h, lanes)

    step = lax.broadcasted_iota(jnp.int32, (c, lanes), 0)
    fwd = lax.broadcasted_iota(jnp.int32, (c, lanes), 1) < _S5_P
    tb = tuple(per_chan(x) for x in bt)
    tc = tuple(per_chan(x) for x in ct)

    def table(base, k_fwd, k_bwd):
        return _cmul(*base, *(per_step(x) for x in powers(jnp.where(fwd, k_fwd, k_bwd))))

    lmat = table(tb, -step, step)
    rmat = table(tc, step, -step)
    wmat = table(tb, c - 1 - step, step)
    vmat = table(tc, step + 1, c - step)

    def nt3(x, y):
        xh, xl = _split_bf16(x)
        yh, yl = _split_bf16(y)
        return _dot_nt(xh, yh) + _dot_nt(xl, yh) + _dot_nt(xh, yl)

    fwd_rows = lax.broadcasted_iota(jnp.int32, (c * h, lanes), 1) < _S5_P

    def kernel_half(keep):
        return (nt3(jnp.where(keep, lmat[0], 0.0), rmat[0])
                - nt3(jnp.where(keep, lmat[1], 0.0), rmat[1]))

    m = causal_ref[...] * kernel_half(fwd_rows) + anti_ref[...] * kernel_half(jnp.logical_not(fwd_rows))
    m_ref[...] = m.astype(m_ref.dtype)
    w_ref[...] = jnp.concatenate([wmat[0], wmat[1]], axis=1).astype(w_ref.dtype)
    vcat = jnp.concatenate([vmat[0], -vmat[1]], axis=1)
    keep_f = (lax.broadcasted_iota(jnp.int32, vcat.shape, 1) % lanes) < _S5_P
    vf_ref[...] = jnp.where(keep_f, vcat, 0.0).T.astype(vf_ref.dtype)
    vb_ref[...] = jnp.where(keep_f, 0.0, vcat).T.astype(vb_ref.dtype)
    chunk_k = jnp.full((1, lanes), c, jnp.int32)
    a_chunk = powers(chunk_k)
    a_ref[0:1, :] = a_chunk[0]
    a_ref[1:2, :] = a_chunk[1]


def _s5_mats_block_kernel(prm_ref, btr_ref, bti_ref, cr_ref, ci_ref, causal_ref, anti_ref,
                          m_ref, w_ref, vf_ref, vb_ref, a_ref):
    for j in range(prm_ref.shape[0]):
        _s5_mats_kernel(prm_ref.at[j], btr_ref.at[j], bti_ref.at[j], cr_ref.at[j], ci_ref.at[j],
                        causal_ref, anti_ref,
                        m_ref.at[j], w_ref.at[j], vf_ref.at[j], vb_ref.at[j], a_ref.at[j])


def _s5_mats(prm, bt_re, bt_im, c_re, c_im):
    g = prm.shape[0]
    cw, lanes = _CHUNK_W, _S5_LANES
    causal, anti = (jnp.asarray(a) for a in _s5_masks())
    gm = _S5_MATS_GB
    gspec = pl.BlockSpec((gm, _S5_H, lanes), lambda i: (i, 0, 0))
    mat_spec = pl.BlockSpec((gm, cw, cw), lambda i: (i, 0, 0))
    mat_sds = jax.ShapeDtypeStruct((g, cw, cw), _BF16)
    return pl.pallas_call(
        _s5_mats_block_kernel,
        grid=(g // gm,),
        in_specs=[pl.BlockSpec((gm, 3, lanes), lambda i: (i, 0, 0)), gspec, gspec, gspec, gspec,
                  _const_spec((cw, cw)), _const_spec((cw, cw))],
        out_specs=[mat_spec] * 4 + [pl.BlockSpec((gm, 2, lanes), lambda i: (i, 0, 0))],
        out_shape=[mat_sds] * 4 + [jax.ShapeDtypeStruct((g, 2, lanes), _F32)],
        compiler_params=_params(("arbitrary",)),
        name="s5_mats",
    )(prm, bt_re, bt_im, c_re, c_im, causal, anti)


def _s5_scan_kernel(nchunk, bsz, has_init, *refs):
    if has_init:
        u_ref, m_ref, w_ref, vf_ref, vb_ref, a_ref, h0_ref, y_ref = refs[:8]
    else:
        u_ref, m_ref, w_ref, vf_ref, vb_ref, a_ref, y_ref, fin_ref = refs[:8]
    s_ref, pf_ref, pb_ref = refs[8:]
    lanes, gb = _S5_LANES, _S5_GB
    rows = nchunk * bsz
    fwd = lax.broadcasted_iota(jnp.int32, (bsz, lanes), 1) < _S5_P

    for gl in range(gb):
        u = u_ref[gl].astype(_BF16)
        y_ref[gl] = jnp.dot(u, m_ref[gl], preferred_element_type=_F32)
        s = jnp.dot(u, w_ref[gl], preferred_element_type=_F32)
        s_ref[...] = jnp.swapaxes(s.reshape(bsz, nchunk, 2 * lanes), 0, 1).reshape(rows, 2 * lanes)

        a_re = a_ref[gl, 0:1, :]
        a_im = a_ref[gl, 1:2, :]
        if has_init:
            init = (h0_ref[gl, :, 0:lanes], h0_ref[gl, :, lanes:2 * lanes])
        else:
            init = (jnp.zeros((bsz, lanes), _F32), jnp.zeros((bsz, lanes), _F32))

        def step(k, carry):
            h_re, h_im = carry
            rowf = pl.ds(pl.multiple_of(k * bsz, bsz), bsz)
            rowb = pl.ds(pl.multiple_of((nchunk - 1 - k) * bsz, bsz), bsz)
            pf_ref[rowf, 0:lanes] = h_re
            pf_ref[rowf, lanes:2 * lanes] = h_im
            pb_ref[rowb, 0:lanes] = h_re
            pb_ref[rowb, lanes:2 * lanes] = h_im
            s_re = jnp.where(fwd, s_ref[rowf, 0:lanes], s_ref[rowb, 0:lanes])
            s_im = jnp.where(fwd, s_ref[rowf, lanes:2 * lanes], s_ref[rowb, lanes:2 * lanes])
            return (a_re * h_re - a_im * h_im + s_re, a_re * h_im + a_im * h_re + s_im)

        fin = lax.fori_loop(0, nchunk, step, init)
        if not has_init:
            fin_ref[gl, :, 0:lanes] = fin[0]
            fin_ref[gl, :, lanes:2 * lanes] = fin[1]
        carried = (jnp.dot(pf_ref[...].astype(_BF16), vf_ref[gl], preferred_element_type=_F32)
                   + jnp.dot(pb_ref[...].astype(_BF16), vb_ref[gl], preferred_element_type=_F32))
        carried = jnp.swapaxes(carried.reshape(nchunk, bsz, 2 * lanes), 0, 1).reshape(rows, 2 * lanes)
        y_ref[gl] = y_ref[gl] + carried


def _s5_scan(u, bsz, mats, init):
    g, rows, cw = u.shape
    nchunk = rows // bsz
    gb = _S5_GB
    tok_spec = pl.BlockSpec((gb, rows, cw), lambda i: (i, 0, 0))
    mat_spec = pl.BlockSpec((gb, cw, cw), lambda i: (i, 0, 0))
    st_spec = pl.BlockSpec((gb, bsz, cw), lambda i: (i, 0, 0))
    ins = [u] + list(mats)
    specs = [tok_spec] + [mat_spec] * 4 + [pl.BlockSpec((gb, 2, _S5_LANES), lambda i: (i, 0, 0))]
    y_sds = jax.ShapeDtypeStruct(u.shape, _F32)
    if init is not None:
        ins.append(init)
        specs.append(st_spec)
        out_specs, out_shape = tok_spec, y_sds
    else:
        out_specs = [tok_spec, st_spec]
        out_shape = [y_sds, jax.ShapeDtypeStruct((g, bsz, cw), _F32)]
    out = pl.pallas_call(
        functools.partial(_s5_scan_kernel, nchunk, bsz, init is not None),
        grid=(g // gb,),
        in_specs=specs, out_specs=out_specs, out_shape=out_shape,
        scratch_shapes=[pltpu.VMEM((rows, cw), _F32)] * 3,
        compiler_params=_params(("arbitrary",)),
        name=f"s5_{nchunk * _S5_CHUNK}",
    )(*ins)
    if init is not None:
        return out, None
    return out[0], out[1]


def _trunk(x, pos, row_of_b, shared_mod, init_state, mod5, filt, s5_mats, wts):
    bsz, seq, d = x.shape
    fold = _POST_TILE // seq if (shared_mod and seq < _POST_TILE) else 1
    rows = lambda a: a.reshape(bsz // fold, seq * fold, a.shape[-1])
    seqs = lambda a: a.reshape(bsz, seq, a.shape[-1])
    z = _pre_hyena(rows(x), pos, mod5, row_of_b, 0, wts["norm1_g"], wts["hy_in_w"], wts["hy_in_b"])
    v = _hyena_conv(seqs(z), wts["hy_conv_w"], wts["hy_conv_b"], wts["hy_fbias"], *filt)
    x1, u = _post_mixer("hyena", "next", rows(x), pos, rows(v), mod5, row_of_b, 0, wts["hy_out_w"],
                        wts["hy_out_b"], wts["norm1_g"], None, wts["norm2_g"], wts["ffn_w13"], wts["ffn_w2"],
                        None)
    ys, fin = _s5_scan(u, bsz, s5_mats, init_state)
    (out,) = _post_mixer("s5", "final", x1, None, ys, mod5, row_of_b, 1, wts["s5_glu_w"], wts["s5_glu_b"],
                         wts["norm1_g"], wts["s5_D"], wts["norm2_g"], wts["ffn_w13"], wts["ffn_w2"],
                         wts["final_g"])
    return seqs(out), fin


def kernel(x_prompt, x_sample, state_s5, c, c_ctx, norm1_g, norm2_g, final_g, ada_w, ada_b, ffn_w13, ffn_w2, hy_in_w, hy_in_b, hy_conv_w, hy_conv_b, hy_pe_w1, hy_pe_b1, hy_pe_w2, hy_pe_b2, hy_pe_w3, hy_freq, hy_fbias, hy_out_w, hy_out_b, s5_A_re, s5_A_im, s5_log_dt, s5_B_re, s5_B_im, s5_C_re, s5_C_im, s5_D, s5_glu_w, s5_glu_b):
    depth, d = norm1_g.shape
    assert depth == 2 and hy_in_w.shape[0] == 1 and s5_glu_w.shape[0] == 1
    dec_b, dec_seq, _ = x_sample.shape
    g = d // _S5_H
    p = _S5_P

    nrow = -(-(1 + dec_b) // 8) * 8
    cond = jnp.concatenate([c_ctx[None], c, jnp.zeros((nrow - 1 - dec_b, d), _F32)], axis=0)
    mod5 = _modulation(cond, ada_w, ada_b)

    row3 = lambda a: a.reshape(a.shape[0], 1, a.shape[-1])
    wts = dict(
        norm1_g=row3(norm1_g), norm2_g=row3(norm2_g), final_g=final_g[None],
        hy_in_w=hy_in_w.astype(_BF16), hy_in_b=row3(hy_in_b),
        hy_conv_w=hy_conv_w[0], hy_conv_b=hy_conv_b, hy_fbias=hy_fbias[0],
        hy_out_w=hy_out_w.astype(_BF16), hy_out_b=row3(hy_out_b),
        ffn_w13=ffn_w13.astype(_BF16), ffn_w2=ffn_w2.astype(_BF16),
        s5_glu_w=s5_glu_w.astype(_BF16), s5_glu_b=row3(s5_glu_b), s5_D=row3(s5_D),
    )
    dirs_on_lanes = lambda a: a.transpose(1, 0, 2).reshape(g, 2 * p)
    ldt = jnp.broadcast_to(s5_log_dt[0][:, :, None], (2, g, p))
    prm = jnp.stack([dirs_on_lanes(s5_A_re[0]), dirs_on_lanes(s5_A_im[0]), dirs_on_lanes(ldt)], axis=1)
    bt = lambda a: a.transpose(1, 3, 0, 2).reshape(g, _S5_H, 2 * p)
    ct = lambda a: a.transpose(1, 2, 0, 3).reshape(g, _S5_H, 2 * p)
    s5_mats = _s5_mats(prm, bt(s5_B_re[0]), bt(s5_B_im[0]), ct(s5_C_re[0]), ct(s5_C_im[0]))

    w1p = jnp.pad(hy_pe_w1[0], ((0, _HY_FO - _HY_EMB), (0, 0)))
    filt_args = (w1p, hy_pe_b1, hy_pe_w2[0], hy_pe_b2, hy_freq, hy_pe_w3[0])
    filt_ctx = _hyena_filters(x_prompt.shape[1], d, *filt_args)
    filt_lat = _hyena_filters(dec_seq, d, *filt_args)

    y_prompt, fin = _trunk(x_prompt, None, lambda b: 0, True, None, mod5, filt_ctx, s5_mats, wts)
    new_state = fin.reshape(g, -1, 2, 2, p).transpose(1, 3, 2, 0, 4)[:, None]

    init = state_s5[:, 0].transpose(3, 0, 2, 1, 4).reshape(g, dec_b, 4 * p)
    pos = jnp.asarray(_pos_embed(dec_seq // _GRID_W, d))
    y_sample, _ = _trunk(x_sample, pos, lambda b: b + 1, False, init, mod5, filt_lat, s5_mats, wts)
    return (y_prompt, y_sample, new_state)
```

```python
import functools
import math

import numpy as np
import jax
import jax.numpy as jnp
from jax import lax
from jax.experimental import pallas as pl
from jax.experimental.pallas import tpu as pltpu

_F32 = jnp.float32
_BF16 = jnp.bfloat16

_EPS = 1e-6
_GRID_W = 64
_POS_BASE = 10000.0
_HY_ORDER = 2
_HY_EMB = 33
_HY_FO = 64
_HY_TARGET = 1e-2
_HY_FAST = 0.3
_HY_SLOW = 1.5
_S5_H = 16
_S5_P = 64
_S5_CHUNK = 16
_CHUNK_W = _S5_CHUNK * _S5_H

_VMEM_LIMIT = 56 * 1024 * 1024
_TOKEN_TILE = 512
_POST_TILE = 512
_POST_SUB = 128
_CONV_DBLK = 512
_CONV_SEQS = 2
_FILT_DBLK = 512
_MOD_PARTS = 2


def _params(sem):
    return pltpu.CompilerParams(dimension_semantics=sem, vmem_limit_bytes=_VMEM_LIMIT)


def _dot_bf16(a, b):
    return jnp.dot(a.astype(_BF16), b.astype(_BF16), preferred_element_type=_F32)


def _split_bf16(x):
    hi = x.astype(_BF16)
    return hi, (x - hi.astype(_F32)).astype(_BF16)


def _dot_3pass(a, b):
    ah, al = _split_bf16(a)
    bh, bl = _split_bf16(b)
    return (jnp.dot(ah, bh, preferred_element_type=_F32) + jnp.dot(al, bh, preferred_element_type=_F32)
            + jnp.dot(ah, bl, preferred_element_type=_F32))


def _dot_nt(a, b, precision=None):
    return lax.dot_general(a, b, (((1,), (1,)), ((), ())), precision=precision,
                           preferred_element_type=_F32)


def _rmsnorm(x, g):
    ms = jnp.mean(x * x, axis=-1, keepdims=True)
    return x * lax.rsqrt(ms + _EPS) * g


def _silu(x):
    return x * jax.nn.sigmoid(x)


def _gelu_tanh(x):
    c = math.sqrt(2.0 / math.pi)
    return 0.5 * x * (1.0 + jnp.tanh(c * (x + 0.044715 * (x * x * x))))


def _const_spec(shape):
    nd = len(shape)
    return pl.BlockSpec(shape, lambda *_: (0,) * nd, pipeline_mode=pl.Buffered(1))


def _layer_spec(shape, layer):
    nd = len(shape)
    return pl.BlockSpec((None,) + tuple(shape), lambda *_: (layer,) + (0,) * nd,
                        pipeline_mode=pl.Buffered(1))


def _mod_spec(d, layer, part, row_of_b):
    return pl.BlockSpec((None, None, None, 1, d), lambda t, b: (layer, part, row_of_b(b), 0, 0))


def _mod_kernel(cond_ref, w_ref, b_ref, o_ref):
    d = o_ref.shape[-1]
    r = _dot_3pass(_silu(cond_ref[...]), w_ref[...])
    for p in range(o_ref.shape[0]):
        o_ref[p] = r[:, p * d:(p + 1) * d] + b_ref[p]


def _modulation(cond, ada_w, ada_b):
    depth, d, d6 = ada_w.shape
    parts = d6 // d
    rows = cond.shape[0]
    pb = _MOD_PARTS
    out = pl.pallas_call(
        _mod_kernel,
        grid=(depth, parts // pb),
        in_specs=[
            pl.BlockSpec((rows, d), lambda l, p: (0, 0)),
            pl.BlockSpec((None, d, pb * d), lambda l, p: (l, 0, p)),
            pl.BlockSpec((None, pb, 1, d), lambda l, p: (l, p, 0, 0)),
        ],
        out_specs=pl.BlockSpec((None, pb, rows, d), lambda l, p: (l, p, 0, 0)),
        out_shape=jax.ShapeDtypeStruct((depth, parts, rows, d), _F32),
        compiler_params=_params(("arbitrary", "arbitrary")),
        name="mod",
    )(cond, ada_w, ada_b.reshape(depth, parts, 1, d))
    return out.reshape(depth, parts, rows, 1, d)


@functools.lru_cache(maxsize=None)
def _dft_consts(seq):
    f = np.arange(seq)[:, None]
    t = np.arange(seq)[None, :]
    ang = np.pi * ((f * t) % (2 * seq)) / seq
    cos = np.cos(ang)
    sin = np.sin(ang)
    alt = np.where(np.arange(seq) % 2 == 0, 1.0, -1.0)
    fwd_top = cos
    fwd_bot = -sin
    fwd_bot[0] = alt
    wgt = np.full((seq, 1), 2.0)
    wgt[0] = 1.0
    inv_top = (cos * wgt).T / (2 * seq)
    inv_bot = (-2.0 * sin).T / (2 * seq)
    inv_bot[:, 0] = alt / (2 * seq)
    fwd = np.concatenate([fwd_top, fwd_bot], axis=0).astype(np.float32)
    inv = np.concatenate([inv_top, inv_bot], axis=1).astype(np.float32)
    return fwd, inv, alt.astype(np.float32)[:, None]


@functools.lru_cache(maxsize=None)
def _filter_consts(seq, d):
    t = np.linspace(0.0, 1.0, seq)[:, None]
    w = 2.0 * np.pi * np.arange(seq)[:, None] / seq
    nb = (_HY_EMB - 1) // 2
    bands = np.linspace(1e-4, nb - 1, nb)[None, :]
    z = np.concatenate([t, np.cos(bands * w), -np.sin(bands * w)], axis=-1)
    zpad = np.zeros((seq, _HY_FO))
    zpad[:, :_HY_EMB] = z
    max_decay = math.log(_HY_TARGET) / _HY_FAST
    min_decay = math.log(_HY_TARGET) / _HY_SLOW
    deltas = np.abs(np.linspace(min_decay, max_decay, d))[None, :]
    return zpad.astype(np.float32), deltas.astype(np.float32)


@functools.lru_cache(maxsize=None)
def _pos_embed(rows, d):
    quarter = d // 4
    omega = 1.0 / (_POS_BASE ** (np.arange(quarter, dtype=np.float64) / quarter))

    def axis_embed(n):
        ang = np.arange(n, dtype=np.float64)[:, None] * omega[None]
        return np.concatenate([np.sin(ang), np.cos(ang)], axis=-1)

    er = np.broadcast_to(axis_embed(rows)[:, None], (rows, _GRID_W, d // 2))
    ec = np.broadcast_to(axis_embed(_GRID_W)[None], (rows, _GRID_W, d // 2))
    return np.concatenate([er, ec], axis=-1).reshape(rows * _GRID_W, d).astype(np.float32)


@functools.lru_cache(maxsize=None)
def _s5_masks():
    step = np.repeat(np.arange(_S5_CHUNK), _S5_H)
    causal = (step[None, :] >= step[:, None]).astype(np.float32)
    anti = (step[:, None] >= step[None, :]).astype(np.float32)
    return causal, anti


def _filter_kernel(z_ref, w1_ref, b1_ref, w2_ref, b2_ref, fr_ref, w3f_ref, w3b_ref, dl_ref,
                   at_ref, ab_ref, alt_ref, krt_ref, krb_ref, ki_ref, h_ref):
    @pl.when((pl.program_id(0) == 0) & (pl.program_id(1) == 0))
    def _():
        fr = fr_ref[...]
        h1 = jnp.sin(fr * (_dot_3pass(z_ref[...], w1_ref[...]) + b1_ref[...]))
        h_ref[...] = jnp.sin(fr * (_dot_3pass(h1, w2_ref[...]) + b2_ref[...]))

    h = h_ref[...]
    decay = jnp.exp(-z_ref[:, 0:1] * dl_ref[...])
    hf = _dot_3pass(h, w3f_ref[...]) * decay
    hb = _dot_3pass(h, w3b_ref[...]) * decay
    norm = (jnp.sum(jnp.abs(hf), axis=0, keepdims=True)
            + jnp.sum(jnp.abs(hb), axis=0, keepdims=True) + _EPS)
    hf = hf / norm
    hb = hb / norm
    first = lax.broadcasted_iota(jnp.int32, hf.shape, 0) == 0
    hb = jnp.where(first, 0.0, hb)
    ksum = hf + hb
    kdiff = hf - hb
    kre = jnp.dot(at_ref[...], ksum.astype(_BF16), preferred_element_type=_F32)
    kim = jnp.dot(ab_ref[...], kdiff.astype(_BF16), preferred_element_type=_F32)
    nyq = jnp.sum(alt_ref[...] * ksum, axis=0, keepdims=True)
    krt_ref[...] = kre
    krb_ref[...] = jnp.where(first, nyq, kre)
    ki_ref[...] = jnp.where(first, 0.0, kim)


def _hyena_filters(seq, d, w1p, b1, w2, b2, freq, w3):
    zpad, deltas = _filter_consts(seq, d)
    fwd, _, alt = _dft_consts(seq)
    fwd = jnp.asarray(fwd).astype(_BF16)
    top = pl.BlockSpec((seq, seq), lambda o, j: (0, 0), pipeline_mode=pl.Buffered(1))
    bot = pl.BlockSpec((seq, seq), lambda o, j: (1, 0), pipeline_mode=pl.Buffered(1))
    nb = d // _FILT_DBLK
    fo = _HY_FO
    out_sds = jax.ShapeDtypeStruct((_HY_ORDER, seq, d), _F32)
    out_spec = pl.BlockSpec((None, seq, _FILT_DBLK), lambda o, j: (o, 0, j))
    return pl.pallas_call(
        _filter_kernel,
        grid=(_HY_ORDER, nb),
        in_specs=[
            _const_spec((seq, fo)), _const_spec((fo, fo)), _const_spec((1, fo)),
            _const_spec((fo, fo)), _const_spec((1, fo)), _const_spec((1, fo)),
            pl.BlockSpec((fo, _FILT_DBLK), lambda o, j: (0, (2 * o) * nb + j)),
            pl.BlockSpec((fo, _FILT_DBLK), lambda o, j: (0, (2 * o + 1) * nb + j)),
            pl.BlockSpec((1, _FILT_DBLK), lambda o, j: (0, j)),
            top, bot,
            _const_spec((seq, 1)),
        ],
        out_specs=[out_spec, out_spec, out_spec],
        out_shape=[out_sds, out_sds, out_sds],
        scratch_shapes=[pltpu.VMEM((seq, fo), _F32)],
        compiler_params=_params(("arbitrary", "arbitrary")),
        name=f"filt{seq}",
    )(jnp.asarray(zpad), w1p, b1, w2, b2, freq, w3, w3, jnp.asarray(deltas),
      fwd, fwd, jnp.asarray(alt))


def _pre_kernel(has_pos, *refs):
    if has_pos:
        x_ref, pos_ref, sh_ref, sc_ref, g_ref, w_ref, b_ref, z_ref = refs
        x = x_ref[...] + pos_ref[...]
    else:
        x_ref, sh_ref, sc_ref, g_ref, w_ref, b_ref, z_ref = refs
        x = x_ref[...]
    h = _rmsnorm(x, g_ref[...]) * (1.0 + sc_ref[...]) + sh_ref[...]
    z_ref[...] = (_dot_bf16(h, w_ref[...]) + b_ref[...]).astype(z_ref.dtype)


def _pre_hyena(x, pos, mod5, row_of_b, layer, norm_g, w_in, b_in):
    bsz, seq, d = x.shape
    n = w_in.shape[-1]
    tm = _TOKEN_TILE
    tok = pl.BlockSpec((None, tm, d), lambda t, b: (b, t, 0))
    ins, specs = [x], [tok]
    if pos is not None:
        ins.append(pos)
        specs.append(pl.BlockSpec((tm, d), lambda t, b: (t, 0)))
    ins += [mod5, mod5, norm_g, w_in, b_in]
    specs += [_mod_spec(d, layer, 0, row_of_b), _mod_spec(d, layer, 1, row_of_b),
              _layer_spec((1, d), layer), _layer_spec((d, n), 0), _layer_spec((1, n), 0)]
    return pl.pallas_call(
        functools.partial(_pre_kernel, pos is not None),
        grid=(seq // tm, bsz),
        in_specs=specs,
        out_specs=pl.BlockSpec((None, tm, n), lambda t, b: (b, t, 0)),
        out_shape=jax.ShapeDtypeStruct((bsz, seq, n), _BF16),
        compiler_params=_params(("arbitrary", "arbitrary")),
        name=f"pre{seq}",
    )(*ins)


def _conv_kernel(zv_ref, z1_ref, z2_ref, wv_ref, w1_ref, w2_ref, bv_ref, b1_ref, b2_ref, fb_ref,
                 krt_ref, krb_ref, ki_ref, ft_ref, fb2_ref, it_ref, ib_ref, o_ref):
    nseq, seq, dblk = zv_ref.shape
    row = lax.broadcasted_iota(jnp.int32, (seq, dblk), 0)
    first = row == 0
    last = row == seq - 1

    def short_conv(z_ref, w_ref, b_ref, i):
        z = z_ref[i].astype(_F32)
        prev = jnp.where(first, 0.0, pltpu.roll(z, 1, 0))
        nxt = jnp.where(last, 0.0, pltpu.roll(z, seq - 1, 0))
        w = w_ref[...]
        return prev * w[0:1] + z * w[1:2] + nxt * w[2:3] + b_ref[...]

    gate_refs = ((z1_ref, w1_ref, b1_ref), (z2_ref, w2_ref, b2_ref))

    def stages(i):
        st = {}

        def start():
            st["v"] = short_conv(zv_ref, wv_ref, bv_ref, i)

        def forward():
            vb = st["v"].astype(_BF16)
            st["p"] = jnp.dot(ft_ref[...], vb, preferred_element_type=_F32)
            st["q"] = jnp.dot(fb2_ref[...], vb, preferred_element_type=_F32)

        def spectrum(o):
            p, q, ki = st.pop("p"), st.pop("q"), ki_ref[o]
            st["yt"] = (p * krt_ref[o] - q * ki).astype(_BF16)
            st["yb"] = (p * ki + q * krb_ref[o]).astype(_BF16)
            st["g"] = short_conv(*gate_refs[o], i)

        def inverse():
            st["y"] = (jnp.dot(it_ref[...], st.pop("yt"), preferred_element_type=_F32)
                       + jnp.dot(ib_ref[...], st.pop("yb"), preferred_element_type=_F32))

        def update(o):
            st["v"] = (st.pop("y") + st["v"] * fb_ref[o:o + 1, :]) * st.pop("g")
            if o == _HY_ORDER - 1:
                o_ref[i] = st.pop("v").astype(o_ref.dtype)

        pipe = [start]
        for o in range(_HY_ORDER):
            pipe += [forward, functools.partial(spectrum, o), inverse, functools.partial(update, o)]
        return pipe

    pipes = [stages(i) for i in range(nseq)]
    depth = len(pipes[0])
    for t in range(depth + nseq - 1):
        for i, pipe in enumerate(pipes):
            if 0 <= t - i < depth:
                pipe[t - i]()


def _hyena_conv(z, conv_w, conv_b, fbias, krt, krb, ki):
    bsz, seq, d3 = z.shape
    d = d3 // 3
    dblk = _CONV_DBLK
    nb = d // dblk
    fwd, inv, _ = _dft_consts(seq)
    fwd = jnp.asarray(fwd).astype(_BF16)
    inv = jnp.asarray(inv).astype(_BF16)

    def mspec(r, c):
        return pl.BlockSpec((seq, seq), lambda j, b: (r, c), pipeline_mode=pl.Buffered(1))

    nseq = _CONV_SEQS

    def zspec(k):
        return pl.BlockSpec((nseq, seq, dblk), lambda j, b: (b, 0, k * nb + j))

    def wspec(rows, k):
        return pl.BlockSpec((rows, dblk), lambda j, b: (0, k * nb + j))

    kspec = pl.BlockSpec((_HY_ORDER, seq, dblk), lambda j, b: (0, 0, j), pipeline_mode=pl.Buffered(1))
    return pl.pallas_call(
        _conv_kernel,
        grid=(nb, bsz // nseq),
        in_specs=[zspec(0), zspec(1), zspec(2),
                  wspec(3, 0), wspec(3, 1), wspec(3, 2),
                  wspec(1, 0), wspec(1, 1), wspec(1, 2),
                  wspec(_HY_ORDER, 0),
                  kspec, kspec, kspec,
                  mspec(0, 0), mspec(1, 0), mspec(0, 0), mspec(0, 1)],
        out_specs=pl.BlockSpec((nseq, seq, dblk), lambda j, b: (b, 0, j)),
        out_shape=jax.ShapeDtypeStruct((bsz, seq, d), _BF16),
        compiler_params=_params(("arbitrary", "arbitrary")),
        name=f"conv{seq}",
    )(z, z, z, conv_w, conv_w, conv_w, conv_b, conv_b, conv_b, fbias, krt, krb, ki,
      fwd, fwd, inv, inv)


def _segment_transpose(xs):
    n = len(xs)
    seg = lax.broadcasted_iota(jnp.int32, xs[0].shape, 1) // _S5_H
    diags = []
    for k in range(n):
        z = xs[-k % n]
        for m in range(1, n):
            z = jnp.where(seg == m, xs[(m - k) % n], z)
        diags.append(pltpu.roll(z, ((n - k) % n) * _S5_H, 1) if k else z)
    ys = []
    for b in range(n):
        y = diags[b]
        for a in range(1, n):
            y = jnp.where(seg == a, diags[(b - a) % n], y)
        ys.append(y)
    return ys


def _to_group_major(x, o_ref, crows):
    sm = jnp.swapaxes(x, 0, 1)
    for j in range(x.shape[-1] // 128):
        for half in range(2):
            per_group = _segment_transpose([sm[half * 8 + s8, :, j * 128:(j + 1) * 128] for s8 in range(8)])
            for gl in range(8):
                o_ref[j * 8 + gl, crows, half * 128:(half + 1) * 128] = per_group[gl]


def _from_group_major(m_ref, crows, d):
    per_step = [[None] * (d // 128) for _ in range(_S5_CHUNK)]
    for j in range(d // 128):
        for half in range(2):
            cols = slice(half * 128, (half + 1) * 128)
            steps = _segment_transpose([m_ref[j * 8 + gl, crows, cols] for gl in range(8)])
            for t8 in range(8):
                per_step[half * 8 + t8][j] = steps[t8]
    sm = jnp.stack([jnp.concatenate(tiles, axis=1) for tiles in per_step])
    return jnp.swapaxes(sm, 0, 1)


def _post_kernel(kind, tail, names, *refs):
    r = dict(zip(names, refs))
    tm, d = r["x"].shape
    steps = _S5_CHUNK
    nsub = max(tm // _POST_SUB, 1)
    sub = tm // nsub

    def stages(i):
        rows = slice(i * sub, (i + 1) * sub)
        crows = slice(i * sub // steps, (i + 1) * sub // steps)
        st = {}

        def head():
            st["x"] = r["x"][rows, :]
            if "pos" in r:
                st["x"] = st["x"] + r["pos"][rows, :]
            if kind == "hyena":
                st["mix"] = r["mix"][rows, :]
            else:
                u = _rmsnorm(st["x"], r["n1"][...]) * (1.0 + r["sc1"][...]) + r["sh1"][...]
                mix = _from_group_major(r["mix"], crows, d).reshape(sub, d)
                st["mix"] = _gelu_tanh(mix + u * r["skip"][...]).astype(_BF16)

        def mixer_out():
            st["m"] = _dot_bf16(st.pop("mix"), r["w_out"][...]) + r["b_out"][...]

        def mid():
            m = st.pop("m")
            if kind == "s5":
                half = m.shape[-1] // 2
                m = m[:, :half] * jax.nn.sigmoid(m[:, half:])
            st["x"] = st["x"] + r["g1"][...] * m
            h = _rmsnorm(st["x"], r["n2"][...]) * (1.0 + r["sc2"][...]) + r["sh2"][...]
            st["h"] = h.astype(_BF16)

        def ffn_in():
            st["ab"] = _dot_bf16(st.pop("h"), r["w13"][...])

        def act():
            ab = st.pop("ab")
            ff = ab.shape[-1] // 2
            st["act"] = (_silu(ab[:, :ff]) * ab[:, ff:]).astype(_BF16)

        def ffn_out():
            st["f"] = _dot_bf16(st.pop("act"), r["w2"][...])

        def finish():
            x = st.pop("x") + r["g2"][...] * st.pop("f")
            if tail == "next":
                r["o_x"][rows, :] = x
                un = _rmsnorm(x, r["nn"][...]) * (1.0 + r["scn"][...]) + r["shn"][...]
                _to_group_major(un.reshape(-1, steps, d), r["o_u"], crows)
            else:
                r["o_x"][rows, :] = _rmsnorm(x, r["nf"][...])

        return [head, mixer_out, mid, ffn_in, act, ffn_out, finish]

    pipes = [stages(i) for i in range(nsub)]
    depth = len(pipes[0])
    for t in range(depth + nsub - 1):
        for i, pipe in enumerate(pipes):
            if 0 <= t - i < depth:
                pipe[t - i]()


def _post_mixer(kind, tail, x, pos, mix, mod5, row_of_b, layer, w_out, b_out, norm1_g, skip,
                norm2_g, w13, w2, final_g):
    bsz, seq, d = x.shape
    tm = min(_POST_TILE, seq)
    tok = pl.BlockSpec((None, tm, d), lambda t, b: (b, t, 0))
    nt = seq // tm
    steps = _S5_CHUNK
    groups = d // _S5_H
    stepmajor = pl.BlockSpec((groups, tm // steps, _CHUNK_W), lambda t, b: (0, b * nt + t, 0))
    stepmajor_sds = jax.ShapeDtypeStruct((groups, bsz * seq // steps, _CHUNK_W), _F32)
    n_out = w_out.shape[-1]
    f2 = w13.shape[-1]

    def mod(part, lyr=layer):
        return (mod5, _mod_spec(d, lyr, part, row_of_b))

    items = [("x", x, tok), ("mix", mix, stepmajor if kind == "s5" else tok),
             ("w_out", w_out, _layer_spec((d, n_out), 0)), ("b_out", b_out, _layer_spec((1, n_out), 0)),
             ("g1",) + mod(2), ("sh2",) + mod(3), ("sc2",) + mod(4), ("g2",) + mod(5),
             ("n2", norm2_g, _layer_spec((1, d), layer)),
             ("w13", w13, _layer_spec((d, f2), layer)), ("w2", w2, _layer_spec((f2 // 2, d), layer))]
    if kind == "s5":
        items += [("n1", norm1_g, _layer_spec((1, d), layer)), ("sh1",) + mod(0), ("sc1",) + mod(1),
                  ("skip", skip, _layer_spec((1, d), 0))]
    if pos is not None:
        items.append(("pos", pos, pl.BlockSpec((tm, d), lambda t, b: (t, 0))))
    out_specs, out_shape = [tok], [jax.ShapeDtypeStruct(x.shape, _F32)]
    names_out = ["o_x"]
    if tail == "next":
        items += [("nn", norm1_g, _layer_spec((1, d), layer + 1)),
                  ("shn",) + mod(0, layer + 1), ("scn",) + mod(1, layer + 1)]
        out_specs.append(stepmajor)
        out_shape.append(stepmajor_sds)
        names_out.append("o_u")
    else:
        items.append(("nf", final_g, _const_spec((1, d))))
    names = tuple(i[0] for i in items) + tuple(names_out)
    out = pl.pallas_call(
        functools.partial(_post_kernel, kind, tail, names),
        grid=(seq // tm, bsz),
        in_specs=[i[2] for i in items], out_specs=out_specs, out_shape=out_shape,
        compiler_params=_params(("arbitrary", "arbitrary")),
        name=f"post_{kind}{seq}",
    )(*[i[1] for i in items])
    return out


_S5_GB = 8
_S5_LANES = 2 * _S5_P
_S5_MATS_GB = 8


def _cmul(ar, ai, br, bi):
    return ar * br - ai * bi, ar * bi + ai * br


def _s5_mats_kernel(prm_ref, btr_ref, bti_ref, cr_ref, ci_ref, causal_ref, anti_ref,
                    m_ref, w_ref, vf_ref, vb_ref, a_ref):
    c, h, lanes = _S5_CHUNK, _S5_H, _S5_LANES
    a_re = prm_ref[0:1, :]
    a_im = prm_ref[1:2, :]
    dt = jnp.exp(prm_ref[2:3, :])
    sr = dt * a_re
    ph = dt * a_im
    mag = jnp.exp(sr)
    a_one = (mag * jnp.cos(ph), mag * jnp.sin(ph))
    nr = a_one[0] - 1.0
    ni = a_one[1]
    den = a_re * a_re + a_im * a_im
    co_re = (nr * a_re + ni * a_im) / den
    co_im = (ni * a_re - nr * a_im) / den
    bt = _cmul(co_re, co_im, btr_ref[...], bti_ref[...])
    ct = (cr_ref[...], ci_ref[...])

    def powers(k):
        kf = k.astype(_F32)
        e = jnp.exp(kf * sr)
        return e * jnp.cos(kf * ph), e * jnp.sin(kf * ph)

    def per_step(x):
        return jnp.broadcast_to(x[:, None, :], (c, h, lanes)).reshape(c * h, lanes)

    def per_chan(x):
        return jnp.broadcast_to(x[None, :, :], (c, h, lanes)).reshape(c * h, lanes)

    step = lax.broadcasted_iota(jnp.int32, (c, lanes), 0)
    fwd = lax.broadcasted_iota(jnp.int32, (c, lanes), 1) < _S5_P
    tb = tuple(per_chan(x) for x in bt)
    tc = tuple(per_chan(x) for x in ct)

    asc = powers(step)
    desc = powers(-step)
    a_last = (asc[0][c - 1:c, :], asc[1][c - 1:c, :])
    a_chunk = _cmul(*a_last, *a_one)

    def table(base, p_fwd, p_bwd):
        pw = tuple(jnp.where(fwd, f, b) for f, b in zip(p_fwd, p_bwd))
        return _cmul(*base, *(per_step(x) for x in pw))

    lmat = table(tb, desc, asc)
    rmat = table(tc, asc, desc)
    wmat = table(tb, _cmul(*a_last, *desc), asc)
    vmat = table(tc, _cmul(*a_one, *asc), _cmul(*a_chunk, *desc))

    def nt3(x, y):
        xh, xl = _split_bf16(x)
        yh, yl = _split_bf16(y)
        return _dot_nt(xh, yh) + _dot_nt(xl, yh) + _dot_nt(xh, yl)

    fwd_rows = lax.broadcasted_iota(jnp.int32, (c * h, lanes), 1) < _S5_P

    def kernel_half(keep):
        return (nt3(jnp.where(keep, lmat[0], 0.0), rmat[0])
                - nt3(jnp.where(keep, lmat[1], 0.0), rmat[1]))

    m = causal_ref[...] * kernel_half(fwd_rows) + anti_ref[...] * kernel_half(jnp.logical_not(fwd_rows))
    m_ref[...] = m.astype(m_ref.dtype)
    w_ref[...] = jnp.concatenate([wmat[0], wmat[1]], axis=1).astype(w_ref.dtype)
    vcat = jnp.concatenate([vmat[0], -vmat[1]], axis=1)
    keep_f = (lax.broadcasted_iota(jnp.int32, vcat.shape, 1) % lanes) < _S5_P
    vf_ref[...] = jnp.where(keep_f, vcat, 0.0).T.astype(vf_ref.dtype)
    vb_ref[...] = jnp.where(keep_f, 0.0, vcat).T.astype(vb_ref.dtype)
    a_ref[0:1, :] = a_chunk[0]
    a_ref[1:2, :] = a_chunk[1]


def _s5_mats_block_kernel(prm_ref, btr_ref, bti_ref, cr_ref, ci_ref, causal_ref, anti_ref,
                          m_ref, w_ref, vf_ref, vb_ref, a_ref):
    for j in range(prm_ref.shape[0]):
        _s5_mats_kernel(prm_ref.at[j], btr_ref.at[j], bti_ref.at[j], cr_ref.at[j], ci_ref.at[j],
                        causal_ref, anti_ref,
                        m_ref.at[j], w_ref.at[j], vf_ref.at[j], vb_ref.at[j], a_ref.at[j])


def _s5_mats(prm, bt_re, bt_im, c_re, c_im):
    g = prm.shape[0]
    cw, lanes = _CHUNK_W, _S5_LANES
    causal, anti = (jnp.asarray(a) for a in _s5_masks())
    gm = _S5_MATS_GB
    gspec = pl.BlockSpec((gm, _S5_H, lanes), lambda i: (i, 0, 0))
    mat_spec = pl.BlockSpec((gm, cw, cw), lambda i: (i, 0, 0))
    mat_sds = jax.ShapeDtypeStruct((g, cw, cw), _BF16)
    return pl.pallas_call(
        _s5_mats_block_kernel,
        grid=(g // gm,),
        in_specs=[pl.BlockSpec((gm, 3, lanes), lambda i: (i, 0, 0)), gspec, gspec, gspec, gspec,
                  _const_spec((cw, cw)), _const_spec((cw, cw))],
        out_specs=[mat_spec] * 4 + [pl.BlockSpec((gm, 2, lanes), lambda i: (i, 0, 0))],
        out_shape=[mat_sds] * 4 + [jax.ShapeDtypeStruct((g, 2, lanes), _F32)],
        compiler_params=_params(("arbitrary",)),
        name="s5_mats",
    )(prm, bt_re, bt_im, c_re, c_im, causal, anti)


def _s5_scan_kernel(nchunk, bsz, has_init, *refs):
    if has_init:
        u_ref, m_ref, w_ref, vf_ref, vb_ref, a_ref, h0_ref, y_ref = refs[:8]
    else:
        u_ref, m_ref, w_ref, vf_ref, vb_ref, a_ref, y_ref, fin_ref = refs[:8]
    s_ref, pf_ref, pb_ref = refs[8:]
    lanes, gb = _S5_LANES, _S5_GB
    rows = nchunk * bsz
    fwd = lax.broadcasted_iota(jnp.int32, (bsz, lanes), 1) < _S5_P

    for gl in range(gb):
        u = u_ref[gl].astype(_BF16)
        y_ref[gl] = jnp.dot(u, m_ref[gl], preferred_element_type=_F32)
        s = jnp.dot(u, w_ref[gl], preferred_element_type=_F32)
        s_ref[...] = jnp.swapaxes(s.reshape(bsz, nchunk, 2 * lanes), 0, 1).reshape(rows, 2 * lanes)

        a_re = a_ref[gl, 0:1, :]
        a_im = a_ref[gl, 1:2, :]
        if has_init:
            init = (h0_ref[gl, :, 0:lanes], h0_ref[gl, :, lanes:2 * lanes])
        else:
            init = (jnp.zeros((bsz, lanes), _F32), jnp.zeros((bsz, lanes), _F32))

        def step(k, carry):
            h_re, h_im = carry
            rowf = pl.ds(pl.multiple_of(k * bsz, bsz), bsz)
            rowb = pl.ds(pl.multiple_of((nchunk - 1 - k) * bsz, bsz), bsz)
            pf_ref[rowf, 0:lanes] = h_re
            pf_ref[rowf, lanes:2 * lanes] = h_im
            pb_ref[rowb, 0:lanes] = h_re
            pb_ref[rowb, lanes:2 * lanes] = h_im
            s_re = jnp.where(fwd, s_ref[rowf, 0:lanes], s_ref[rowb, 0:lanes])
            s_im = jnp.where(fwd, s_ref[rowf, lanes:2 * lanes], s_ref[rowb, lanes:2 * lanes])
            return (a_re * h_re - a_im * h_im + s_re, a_re * h_im + a_im * h_re + s_im)

        fin = lax.fori_loop(0, nchunk, step, init)
        if not has_init:
            fin_ref[gl, :, 0:lanes] = fin[0]
            fin_ref[gl, :, lanes:2 * lanes] = fin[1]
        carried = (jnp.dot(pf_ref[...].astype(_BF16), vf_ref[gl], preferred_element_type=_F32)
                   + jnp.dot(pb_ref[...].astype(_BF16), vb_ref[gl], preferred_element_type=_F32))
        carried = jnp.swapaxes(carried.reshape(nchunk, bsz, 2 * lanes), 0, 1).reshape(rows, 2 * lanes)
        y_ref[gl] = y_ref[gl] + carried


def _s5_scan(u, bsz, mats, init):
    g, rows, cw = u.shape
    nchunk = rows // bsz
    gb = _S5_GB
    tok_spec = pl.BlockSpec((gb, rows, cw), lambda i: (i, 0, 0))
    mat_spec = pl.BlockSpec((gb, cw, cw), lambda i: (i, 0, 0))
    st_spec = pl.BlockSpec((gb, bsz, cw), lambda i: (i, 0, 0))
    ins = [u] + list(mats)
    specs = [tok_spec] + [mat_spec] * 4 + [pl.BlockSpec((gb, 2, _S5_LANES), lambda i: (i, 0, 0))]
    y_sds = jax.ShapeDtypeStruct(u.shape, _F32)
    if init is not None:
        ins.append(init)
        specs.append(st_spec)
        out_specs, out_shape = tok_spec, y_sds
    else:
        out_specs = [tok_spec, st_spec]
        out_shape = [y_sds, jax.ShapeDtypeStruct((g, bsz, cw), _F32)]
    out = pl.pallas_call(
        functools.partial(_s5_scan_kernel, nchunk, bsz, init is not None),
        grid=(g // gb,),
        in_specs=specs, out_specs=out_specs, out_shape=out_shape,
        scratch_shapes=[pltpu.VMEM((rows, cw), _F32)] * 3,
        compiler_params=_params(("arbitrary",)),
        name=f"s5_{nchunk * _S5_CHUNK}",
    )(*ins)
    if init is not None:
        return out, None
    return out[0], out[1]


def _trunk(x, pos, row_of_b, shared_mod, init_state, mod5, filt, s5_mats, wts):
    bsz, seq, d = x.shape
    fold = _POST_TILE // seq if (shared_mod and seq < _POST_TILE) else 1
    rows = lambda a: a.reshape(bsz // fold, seq * fold, a.shape[-1])
    seqs = lambda a: a.reshape(bsz, seq, a.shape[-1])
    z = _pre_hyena(rows(x), pos, mod5, row_of_b, 0, wts["norm1_g"], wts["hy_in_w"], wts["hy_in_b"])
    v = _hyena_conv(seqs(z), wts["hy_conv_w"], wts["hy_conv_b"], wts["hy_fbias"], *filt)
    x1, u = _post_mixer("hyena", "next", rows(x), pos, rows(v), mod5, row_of_b, 0, wts["hy_out_w"],
                        wts["hy_out_b"], wts["norm1_g"], None, wts["norm2_g"], wts["ffn_w13"], wts["ffn_w2"],
                        None)
    ys, fin = _s5_scan(u, bsz, s5_mats, init_state)
    (out,) = _post_mixer("s5", "final", x1, None, ys, mod5, row_of_b, 1, wts["s5_glu_w"], wts["s5_glu_b"],
                         wts["norm1_g"], wts["s5_D"], wts["norm2_g"], wts["ffn_w13"], wts["ffn_w2"],
                         wts["final_g"])
    return seqs(out), fin


def kernel(x_prompt, x_sample, state_s5, c, c_ctx, norm1_g, norm2_g, final_g, ada_w, ada_b, ffn_w13, ffn_w2, hy_in_w, hy_in_b, hy_conv_w, hy_conv_b, hy_pe_w1, hy_pe_b1, hy_pe_w2, hy_pe_b2, hy_pe_w3, hy_freq, hy_fbias, hy_out_w, hy_out_b, s5_A_re, s5_A_im, s5_log_dt, s5_B_re, s5_B_im, s5_C_re, s5_C_im, s5_D, s5_glu_w, s5_glu_b):
    depth, d = norm1_g.shape
    assert depth == 2 and hy_in_w.shape[0] == 1 and s5_glu_w.shape[0] == 1
    dec_b, dec_seq, _ = x_sample.shape
    g = d // _S5_H
    p = _S5_P

    nrow = -(-(1 + dec_b) // 8) * 8
    cond = jnp.concatenate([c_ctx[None], c, jnp.zeros((nrow - 1 - dec_b, d), _F32)], axis=0)
    mod5 = _modulation(cond, ada_w, ada_b)

    row3 = lambda a: a.reshape(a.shape[0], 1, a.shape[-1])
    wts = dict(
        norm1_g=row3(norm1_g), norm2_g=row3(norm2_g), final_g=final_g[None],
        hy_in_w=hy_in_w.astype(_BF16), hy_in_b=row3(hy_in_b),
        hy_conv_w=hy_conv_w[0], hy_conv_b=hy_conv_b, hy_fbias=hy_fbias[0],
        hy_out_w=hy_out_w.astype(_BF16), hy_out_b=row3(hy_out_b),
        ffn_w13=ffn_w13.astype(_BF16), ffn_w2=ffn_w2.astype(_BF16),
        s5_glu_w=s5_glu_w.astype(_BF16), s5_glu_b=row3(s5_glu_b), s5_D=row3(s5_D),
    )
    dirs_on_lanes = lambda a: a.transpose(1, 0, 2).reshape(g, 2 * p)
    ldt = jnp.broadcast_to(s5_log_dt[0][:, :, None], (2, g, p))
    prm = jnp.stack([dirs_on_lanes(s5_A_re[0]), dirs_on_lanes(s5_A_im[0]), dirs_on_lanes(ldt)], axis=1)
    bt = lambda a: a.transpose(1, 3, 0, 2).reshape(g, _S5_H, 2 * p)
    ct = lambda a: a.transpose(1, 2, 0, 3).reshape(g, _S5_H, 2 * p)
    s5_mats = _s5_mats(prm, bt(s5_B_re[0]), bt(s5_B_im[0]), ct(s5_C_re[0]), ct(s5_C_im[0]))

    w1p = jnp.pad(hy_pe_w1[0], ((0, _HY_FO - _HY_EMB), (0, 0)))
    filt_args = (w1p, hy_pe_b1, hy_pe_w2[0], hy_pe_b2, hy_freq, hy_pe_w3[0])
    filt_ctx = _hyena_filters(x_prompt.shape[1], d, *filt_args)
    filt_lat = _hyena_filters(dec_seq, d, *filt_args)

    y_prompt, fin = _trunk(x_prompt, None, lambda b: 0, True, None, mod5, filt_ctx, s5_mats, wts)
    new_state = fin.reshape(g, -1, 2, 2, p).transpose(1, 3, 2, 0, 4)[:, None]

    init = state_s5[:, 0].transpose(3, 0, 2, 1, 4).reshape(g, dec_b, 4 * p)
    pos = jnp.asarray(_pos_embed(dec_seq // _GRID_W, d))
    y_sample, _ = _trunk(x_sample, pos, lambda b: b + 1, False, init, mod5, filt_lat, s5_mats, wts)
    return (y_prompt, y_sample, new_state)
```

```python
import functools
import math

import numpy as np
import jax
import jax.numpy as jnp
from jax import lax
from jax.experimental import pallas as pl
from jax.experimental.pallas import tpu as pltpu

_F32 = jnp.float32
_BF16 = jnp.bfloat16

_EPS = 1e-6
_GRID_W = 64
_POS_BASE = 10000.0
_HY_ORDER = 2
_HY_EMB = 33
_HY_FO = 64
_HY_TARGET = 1e-2
_HY_FAST = 0.3
_HY_SLOW = 1.5
_S5_H = 16
_S5_P = 64
_S5_CHUNK = 16
_CHUNK_W = _S5_CHUNK * _S5_H

_VMEM_LIMIT = 56 * 1024 * 1024
_TOKEN_TILE = 512
_POST_TILE = 512
_POST_SUB = 128
_CONV_DBLK = 512
_CONV_SEQS = 2
_CONV_ROWS = 1024
_FILT_DBLK = 512
_MOD_PARTS = 2


def _params(sem):
    return pltpu.CompilerParams(dimension_semantics=sem, vmem_limit_bytes=_VMEM_LIMIT)


def _dot_bf16(a, b):
    return jnp.dot(a.astype(_BF16), b.astype(_BF16), preferred_element_type=_F32)


def _split_bf16(x):
    hi = x.astype(_BF16)
    return hi, (x - hi.astype(_F32)).astype(_BF16)


def _dot_3pass(a, b):
    ah, al = _split_bf16(a)
    bh, bl = _split_bf16(b)
    return (jnp.dot(ah, bh, preferred_element_type=_F32) + jnp.dot(al, bh, preferred_element_type=_F32)
            + jnp.dot(ah, bl, preferred_element_type=_F32))


def _dot_nt(a, b, precision=None):
    return lax.dot_general(a, b, (((1,), (1,)), ((), ())), precision=precision,
                           preferred_element_type=_F32)


def _rmsnorm(x, g):
    ms = jnp.mean(x * x, axis=-1, keepdims=True)
    return x * lax.rsqrt(ms + _EPS) * g


def _silu(x):
    return x * jax.nn.sigmoid(x)


def _gelu_tanh(x):
    c = math.sqrt(2.0 / math.pi)
    return 0.5 * x * (1.0 + jnp.tanh(c * (x + 0.044715 * (x * x * x))))


def _const_spec(shape):
    nd = len(shape)
    return pl.BlockSpec(shape, lambda *_: (0,) * nd, pipeline_mode=pl.Buffered(1))


def _layer_spec(shape, layer):
    nd = len(shape)
    return pl.BlockSpec((None,) + tuple(shape), lambda *_: (layer,) + (0,) * nd,
                        pipeline_mode=pl.Buffered(1))


def _mod_spec(d, layer, part, row_of_b):
    return pl.BlockSpec((None, None, None, 1, d), lambda t, b: (layer, part, row_of_b(b), 0, 0))


def _mod_kernel(cond_ref, w_ref, b_ref, o_ref):
    d = o_ref.shape[-1]
    r = _dot_3pass(_silu(cond_ref[...]), w_ref[...])
    for p in range(o_ref.shape[0]):
        o_ref[p] = r[:, p * d:(p + 1) * d] + b_ref[p]


def _modulation(cond, ada_w, ada_b):
    depth, d, d6 = ada_w.shape
    parts = d6 // d
    rows = cond.shape[0]
    pb = _MOD_PARTS
    out = pl.pallas_call(
        _mod_kernel,
        grid=(depth, parts // pb),
        in_specs=[
            pl.BlockSpec((rows, d), lambda l, p: (0, 0)),
            pl.BlockSpec((None, d, pb * d), lambda l, p: (l, 0, p)),
            pl.BlockSpec((None, pb, 1, d), lambda l, p: (l, p, 0, 0)),
        ],
        out_specs=pl.BlockSpec((None, pb, rows, d), lambda l, p: (l, p, 0, 0)),
        out_shape=jax.ShapeDtypeStruct((depth, parts, rows, d), _F32),
        compiler_params=_params(("arbitrary", "arbitrary")),
        name="mod",
    )(cond, ada_w, ada_b.reshape(depth, parts, 1, d))
    return out.reshape(depth, parts, rows, 1, d)


@functools.lru_cache(maxsize=None)
def _dft_consts(seq):
    f = np.arange(seq)[:, None]
    t = np.arange(seq)[None, :]
    ang = np.pi * ((f * t) % (2 * seq)) / seq
    cos = np.cos(ang)
    sin = np.sin(ang)
    alt = np.where(np.arange(seq) % 2 == 0, 1.0, -1.0)
    fwd_top = cos
    fwd_bot = -sin
    fwd_bot[0] = alt
    wgt = np.full((seq, 1), 2.0)
    wgt[0] = 1.0
    inv_top = (cos * wgt).T / (2 * seq)
    inv_bot = (-2.0 * sin).T / (2 * seq)
    inv_bot[:, 0] = alt / (2 * seq)
    fwd = np.concatenate([fwd_top, fwd_bot], axis=0).astype(np.float32)
    inv = np.concatenate([inv_top, inv_bot], axis=1).astype(np.float32)
    return fwd, inv, alt.astype(np.float32)[:, None]


@functools.lru_cache(maxsize=None)
def _filter_consts(seq, d):
    t = np.linspace(0.0, 1.0, seq)[:, None]
    w = 2.0 * np.pi * np.arange(seq)[:, None] / seq
    nb = (_HY_EMB - 1) // 2
    bands = np.linspace(1e-4, nb - 1, nb)[None, :]
    z = np.concatenate([t, np.cos(bands * w), -np.sin(bands * w)], axis=-1)
    zpad = np.zeros((seq, _HY_FO))
    zpad[:, :_HY_EMB] = z
    max_decay = math.log(_HY_TARGET) / _HY_FAST
    min_decay = math.log(_HY_TARGET) / _HY_SLOW
    deltas = np.abs(np.linspace(min_decay, max_decay, d))[None, :]
    return zpad.astype(np.float32), deltas.astype(np.float32)


@functools.lru_cache(maxsize=None)
def _pos_embed(rows, d):
    quarter = d // 4
    omega = 1.0 / (_POS_BASE ** (np.arange(quarter, dtype=np.float64) / quarter))

    def axis_embed(n):
        ang = np.arange(n, dtype=np.float64)[:, None] * omega[None]
        return np.concatenate([np.sin(ang), np.cos(ang)], axis=-1)

    er = np.broadcast_to(axis_embed(rows)[:, None], (rows, _GRID_W, d // 2))
    ec = np.broadcast_to(axis_embed(_GRID_W)[None], (rows, _GRID_W, d // 2))
    return np.concatenate([er, ec], axis=-1).reshape(rows * _GRID_W, d).astype(np.float32)


@functools.lru_cache(maxsize=None)
def _s5_masks():
    step = np.repeat(np.arange(_S5_CHUNK), _S5_H)
    causal = (step[None, :] >= step[:, None]).astype(np.float32)
    anti = (step[:, None] >= step[None, :]).astype(np.float32)
    return causal, anti


def _filter_kernel(z_ref, w1_ref, b1_ref, w2_ref, b2_ref, fr_ref, w3f_ref, w3b_ref, dl_ref,
                   at_ref, ab_ref, alt_ref, krt_ref, krb_ref, ki_ref, h_ref):
    @pl.when((pl.program_id(0) == 0) & (pl.program_id(1) == 0))
    def _():
        fr = fr_ref[...]
        h1 = jnp.sin(fr * (_dot_3pass(z_ref[...], w1_ref[...]) + b1_ref[...]))
        h_ref[...] = jnp.sin(fr * (_dot_3pass(h1, w2_ref[...]) + b2_ref[...]))

    h = h_ref[...]
    decay = jnp.exp(-z_ref[:, 0:1] * dl_ref[...])
    hf = _dot_3pass(h, w3f_ref[...]) * decay
    hb = _dot_3pass(h, w3b_ref[...]) * decay
    norm = (jnp.sum(jnp.abs(hf), axis=0, keepdims=True)
            + jnp.sum(jnp.abs(hb), axis=0, keepdims=True) + _EPS)
    hf = hf / norm
    hb = hb / norm
    first = lax.broadcasted_iota(jnp.int32, hf.shape, 0) == 0
    hb = jnp.where(first, 0.0, hb)
    ksum = hf + hb
    kdiff = hf - hb
    kre = jnp.dot(at_ref[...], ksum.astype(_BF16), preferred_element_type=_F32)
    kim = jnp.dot(ab_ref[...], kdiff.astype(_BF16), preferred_element_type=_F32)
    nyq = jnp.sum(alt_ref[...] * ksum, axis=0, keepdims=True)
    krt_ref[...] = kre
    krb_ref[...] = jnp.where(first, nyq, kre)
    ki_ref[...] = jnp.where(first, 0.0, kim)


def _hyena_filters(seq, d, w1p, b1, w2, b2, freq, w3):
    zpad, deltas = _filter_consts(seq, d)
    fwd, _, alt = _dft_consts(seq)
    fwd = jnp.asarray(fwd).astype(_BF16)
    top = pl.BlockSpec((seq, seq), lambda o, j: (0, 0), pipeline_mode=pl.Buffered(1))
    bot = pl.BlockSpec((seq, seq), lambda o, j: (1, 0), pipeline_mode=pl.Buffered(1))
    nb = d // _FILT_DBLK
    fo = _HY_FO
    out_sds = jax.ShapeDtypeStruct((_HY_ORDER, seq, d), _F32)
    out_spec = pl.BlockSpec((None, seq, _FILT_DBLK), lambda o, j: (o, 0, j))
    return pl.pallas_call(
        _filter_kernel,
        grid=(_HY_ORDER, nb),
        in_specs=[
            _const_spec((seq, fo)), _const_spec((fo, fo)), _const_spec((1, fo)),
            _const_spec((fo, fo)), _const_spec((1, fo)), _const_spec((1, fo)),
            pl.BlockSpec((fo, _FILT_DBLK), lambda o, j: (0, (2 * o) * nb + j)),
            pl.BlockSpec((fo, _FILT_DBLK), lambda o, j: (0, (2 * o + 1) * nb + j)),
            pl.BlockSpec((1, _FILT_DBLK), lambda o, j: (0, j)),
            top, bot,
            _const_spec((seq, 1)),
        ],
        out_specs=[out_spec, out_spec, out_spec],
        out_shape=[out_sds, out_sds, out_sds],
        scratch_shapes=[pltpu.VMEM((seq, fo), _F32)],
        compiler_params=_params(("arbitrary", "arbitrary")),
        name=f"filt{seq}",
    )(jnp.asarray(zpad), w1p, b1, w2, b2, freq, w3, w3, jnp.asarray(deltas),
      fwd, fwd, jnp.asarray(alt))


def _pre_kernel(has_pos, *refs):
    if has_pos:
        x_ref, pos_ref, sh_ref, sc_ref, g_ref, w_ref, b_ref, z_ref = refs
        x = x_ref[...] + pos_ref[...]
    else:
        x_ref, sh_ref, sc_ref, g_ref, w_ref, b_ref, z_ref = refs
        x = x_ref[...]
    h = _rmsnorm(x, g_ref[...]) * (1.0 + sc_ref[...]) + sh_ref[...]
    z_ref[...] = (_dot_bf16(h, w_ref[...]) + b_ref[...]).astype(z_ref.dtype)


def _pre_hyena(x, pos, mod5, row_of_b, layer, norm_g, w_in, b_in):
    bsz, seq, d = x.shape
    n = w_in.shape[-1]
    tm = _TOKEN_TILE
    tok = pl.BlockSpec((None, tm, d), lambda t, b: (b, t, 0))
    ins, specs = [x], [tok]
    if pos is not None:
        ins.append(pos)
        specs.append(pl.BlockSpec((tm, d), lambda t, b: (t, 0)))
    ins += [mod5, mod5, norm_g, w_in, b_in]
    specs += [_mod_spec(d, layer, 0, row_of_b), _mod_spec(d, layer, 1, row_of_b),
              _layer_spec((1, d), layer), _layer_spec((d, n), 0), _layer_spec((1, n), 0)]
    return pl.pallas_call(
        functools.partial(_pre_kernel, pos is not None),
        grid=(seq // tm, bsz),
        in_specs=specs,
        out_specs=pl.BlockSpec((None, tm, n), lambda t, b: (b, t, 0)),
        out_shape=jax.ShapeDtypeStruct((bsz, seq, n), _BF16),
        compiler_params=_params(("arbitrary", "arbitrary")),
        name=f"pre{seq}",
    )(*ins)


def _conv_kernel(zv_ref, z1_ref, z2_ref, wv_ref, w1_ref, w2_ref, bv_ref, b1_ref, b2_ref, fb_ref,
                 krt_ref, krb_ref, ki_ref, ft_ref, fb2_ref, it_ref, ib_ref, o_ref):
    nseq, seq, dblk = zv_ref.shape
    row = lax.broadcasted_iota(jnp.int32, (seq, dblk), 0)
    first = row == 0
    last = row == seq - 1

    def short_conv(z_ref, w_ref, b_ref, i):
        z = z_ref[i].astype(_F32)
        prev = jnp.where(first, 0.0, pltpu.roll(z, 1, 0))
        nxt = jnp.where(last, 0.0, pltpu.roll(z, seq - 1, 0))
        w = w_ref[...]
        return prev * w[0:1] + z * w[1:2] + nxt * w[2:3] + b_ref[...]

    gate_refs = ((z1_ref, w1_ref, b1_ref), (z2_ref, w2_ref, b2_ref))

    def stages(i):
        st = {}

        def start():
            st["v"] = short_conv(zv_ref, wv_ref, bv_ref, i)

        def forward():
            vb = st["v"].astype(_BF16)
            st["p"] = jnp.dot(ft_ref[...], vb, preferred_element_type=_F32)
            st["q"] = jnp.dot(fb2_ref[...], vb, preferred_element_type=_F32)

        def spectrum(o):
            p, q, ki = st.pop("p"), st.pop("q"), ki_ref[o]
            st["yt"] = (p * krt_ref[o] - q * ki).astype(_BF16)
            st["yb"] = (p * ki + q * krb_ref[o]).astype(_BF16)
            st["g"] = short_conv(*gate_refs[o], i)

        def inverse():
            st["y"] = (jnp.dot(it_ref[...], st.pop("yt"), preferred_element_type=_F32)
                       + jnp.dot(ib_ref[...], st.pop("yb"), preferred_element_type=_F32))

        def update(o):
            st["v"] = (st.pop("y") + st["v"] * fb_ref[o:o + 1, :]) * st.pop("g")
            if o == _HY_ORDER - 1:
                o_ref[i] = st.pop("v").astype(o_ref.dtype)

        pipe = [start]
        for o in range(_HY_ORDER):
            pipe += [forward, functools.partial(spectrum, o), inverse, functools.partial(update, o)]
        return pipe

    pipes = [stages(i) for i in range(nseq)]
    depth = len(pipes[0])
    for t in range(depth + nseq - 1):
        for i, pipe in enumerate(pipes):
            if 0 <= t - i < depth:
                pipe[t - i]()


def _hyena_conv(z, conv_w, conv_b, fbias, krt, krb, ki):
    bsz, seq, d3 = z.shape
    d = d3 // 3
    dblk = _CONV_DBLK
    nb = d // dblk
    fwd, inv, _ = _dft_consts(seq)
    fwd = jnp.asarray(fwd).astype(_BF16)
    inv = jnp.asarray(inv).astype(_BF16)

    def mspec(r, c):
        return pl.BlockSpec((seq, seq), lambda j, b: (r, c), pipeline_mode=pl.Buffered(1))

    nseq = min(bsz, max(_CONV_SEQS, _CONV_ROWS // seq))
    assert bsz % nseq == 0 and d % dblk == 0

    def zspec(k):
        return pl.BlockSpec((nseq, seq, dblk), lambda j, b: (b, 0, k * nb + j))

    def wspec(rows, k):
        return pl.BlockSpec((rows, dblk), lambda j, b: (0, k * nb + j))

    kspec = pl.BlockSpec((_HY_ORDER, seq, dblk), lambda j, b: (0, 0, j), pipeline_mode=pl.Buffered(1))
    return pl.pallas_call(
        _conv_kernel,
        grid=(nb, bsz // nseq),
        in_specs=[zspec(0), zspec(1), zspec(2),
                  wspec(3, 0), wspec(3, 1), wspec(3, 2),
                  wspec(1, 0), wspec(1, 1), wspec(1, 2),
                  wspec(_HY_ORDER, 0),
                  kspec, kspec, kspec,
                  mspec(0, 0), mspec(1, 0), mspec(0, 0), mspec(0, 1)],
        out_specs=pl.BlockSpec((nseq, seq, dblk), lambda j, b: (b, 0, j)),
        out_shape=jax.ShapeDtypeStruct((bsz, seq, d), _BF16),
        compiler_params=_params(("arbitrary", "arbitrary")),
        name=f"conv{seq}",
    )(z, z, z, conv_w, conv_w, conv_w, conv_b, conv_b, conv_b, fbias, krt, krb, ki,
      fwd, fwd, inv, inv)


def _segment_transpose(xs):
    n = len(xs)
    seg = lax.broadcasted_iota(jnp.int32, xs[0].shape, 1) // _S5_H
    diags = []
    for k in range(n):
        z = xs[-k % n]
        for m in range(1, n):
            z = jnp.where(seg == m, xs[(m - k) % n], z)
        diags.append(pltpu.roll(z, ((n - k) % n) * _S5_H, 1) if k else z)
    ys = []
    for b in range(n):
        y = diags[b]
        for a in range(1, n):
            y = jnp.where(seg == a, diags[(b - a) % n], y)
        ys.append(y)
    return ys


def _to_group_major(x, o_ref, crows):
    sm = jnp.swapaxes(x, 0, 1)
    for j in range(x.shape[-1] // 128):
        for half in range(2):
            per_group = _segment_transpose([sm[half * 8 + s8, :, j * 128:(j + 1) * 128] for s8 in range(8)])
            for gl in range(8):
                o_ref[j * 8 + gl, crows, half * 128:(half + 1) * 128] = per_group[gl]


def _from_group_major(m_ref, crows, d):
    per_step = [[None] * (d // 128) for _ in range(_S5_CHUNK)]
    for j in range(d // 128):
        for half in range(2):
            cols = slice(half * 128, (half + 1) * 128)
            steps = _segment_transpose([m_ref[j * 8 + gl, crows, cols] for gl in range(8)])
            for t8 in range(8):
                per_step[half * 8 + t8][j] = steps[t8]
    sm = jnp.stack([jnp.concatenate(tiles, axis=1) for tiles in per_step])
    return jnp.swapaxes(sm, 0, 1)


def _post_kernel(kind, tail, names, *refs):
    r = dict(zip(names, refs))
    tm, d = r["x"].shape
    steps = _S5_CHUNK
    nsub = max(tm // _POST_SUB, 1)
    sub = tm // nsub

    def stages(i):
        rows = slice(i * sub, (i + 1) * sub)
        crows = slice(i * sub // steps, (i + 1) * sub // steps)
        st = {}

        def head():
            st["x"] = r["x"][rows, :]
            if "pos" in r:
                st["x"] = st["x"] + r["pos"][rows, :]
            if kind == "hyena":
                st["mix"] = r["mix"][rows, :]
            else:
                u = _rmsnorm(st["x"], r["n1"][...]) * (1.0 + r["sc1"][...]) + r["sh1"][...]
                mix = _from_group_major(r["mix"], crows, d).reshape(sub, d)
                st["mix"] = _gelu_tanh(mix + u * r["skip"][...]).astype(_BF16)

        def mixer_out():
            st["m"] = _dot_bf16(st.pop("mix"), r["w_out"][...]) + r["b_out"][...]

        def mid():
            m = st.pop("m")
            if kind == "s5":
                half = m.shape[-1] // 2
                m = m[:, :half] * jax.nn.sigmoid(m[:, half:])
            st["x"] = st["x"] + r["g1"][...] * m
            h = _rmsnorm(st["x"], r["n2"][...]) * (1.0 + r["sc2"][...]) + r["sh2"][...]
            st["h"] = h.astype(_BF16)

        def ffn_in():
            st["ab"] = _dot_bf16(st.pop("h"), r["w13"][...])

        def act():
            ab = st.pop("ab")
            ff = ab.shape[-1] // 2
            st["act"] = (_silu(ab[:, :ff]) * ab[:, ff:]).astype(_BF16)

        def ffn_out():
            st["f"] = _dot_bf16(st.pop("act"), r["w2"][...])

        def finish():
            x = st.pop("x") + r["g2"][...] * st.pop("f")
            if tail == "next":
                r["o_x"][rows, :] = x
                un = _rmsnorm(x, r["nn"][...]) * (1.0 + r["scn"][...]) + r["shn"][...]
                _to_group_major(un.reshape(-1, steps, d), r["o_u"], crows)
            else:
                r["o_x"][rows, :] = _rmsnorm(x, r["nf"][...])

        return [head, mixer_out, mid, ffn_in, act, ffn_out, finish]

    pipes = [stages(i) for i in range(nsub)]
    depth = len(pipes[0])
    for t in range(depth + nsub - 1):
        for i, pipe in enumerate(pipes):
            if 0 <= t - i < depth:
                pipe[t - i]()


def _post_mixer(kind, tail, x, pos, mix, mod5, row_of_b, layer, w_out, b_out, norm1_g, skip,
                norm2_g, w13, w2, final_g):
    bsz, seq, d = x.shape
    tm = min(_POST_TILE, seq)
    tok = pl.BlockSpec((None, tm, d), lambda t, b: (b, t, 0))
    nt = seq // tm
    steps = _S5_CHUNK
    groups = d // _S5_H
    stepmajor = pl.BlockSpec((groups, tm // steps, _CHUNK_W), lambda t, b: (0, b * nt + t, 0))
    stepmajor_sds = jax.ShapeDtypeStruct((groups, bsz * seq // steps, _CHUNK_W), _F32)
    n_out = w_out.shape[-1]
    f2 = w13.shape[-1]

    def mod(part, lyr=layer):
        return (mod5, _mod_spec(d, lyr, part, row_of_b))

    items = [("x", x, tok), ("mix", mix, stepmajor if kind == "s5" else tok),
             ("w_out", w_out, _layer_spec((d, n_out), 0)), ("b_out", b_out, _layer_spec((1, n_out), 0)),
             ("g1",) + mod(2), ("sh2",) + mod(3), ("sc2",) + mod(4), ("g2",) + mod(5),
             ("n2", norm2_g, _layer_spec((1, d), layer)),
             ("w13", w13, _layer_spec((d, f2), layer)), ("w2", w2, _layer_spec((f2 // 2, d), layer))]
    if kind == "s5":
        items += [("n1", norm1_g, _layer_spec((1, d), layer)), ("sh1",) + mod(0), ("sc1",) + mod(1),
                  ("skip", skip, _layer_spec((1, d), 0))]
    if pos is not None:
        items.append(("pos", pos, pl.BlockSpec((tm, d), lambda t, b: (t, 0))))
    out_specs, out_shape = [tok], [jax.ShapeDtypeStruct(x.shape, _F32)]
    names_out = ["o_x"]
    if tail == "next":
        items += [("nn", norm1_g, _layer_spec((1, d), layer + 1)),
                  ("shn",) + mod(0, layer + 1), ("scn",) + mod(1, layer + 1)]
        out_specs.append(stepmajor)
        out_shape.append(stepmajor_sds)
        names_out.append("o_u")
    else:
        items.append(("nf", final_g, _const_spec((1, d))))
    names = tuple(i[0] for i in items) + tuple(names_out)
    out = pl.pallas_call(
        functools.partial(_post_kernel, kind, tail, names),
        grid=(seq // tm, bsz),
        in_specs=[i[2] for i in items], out_specs=out_specs, out_shape=out_shape,
        compiler_params=_params(("arbitrary", "arbitrary")),
        name=f"post_{kind}{seq}",
    )(*[i[1] for i in items])
    return out


_S5_GB = 8
_S5_LANES = 2 * _S5_P
_S5_MATS_GB = 8


def _cmul(ar, ai, br, bi):
    return ar * br - ai * bi, ar * bi + ai * br


def _s5_mats_kernel(prm_ref, btr_ref, bti_ref, cr_ref, ci_ref, causal_ref, anti_ref,
                    m_ref, w_ref, vf_ref, vb_ref, a_ref):
    c, h, lanes = _S5_CHUNK, _S5_H, _S5_LANES
    a_re = prm_ref[0:1, :]
    a_im = prm_ref[1:2, :]
    dt = jnp.exp(prm_ref[2:3, :])
    sr = dt * a_re
    ph = dt * a_im
    mag = jnp.exp(sr)
    a_one = (mag * jnp.cos(ph), mag * jnp.sin(ph))
    nr = a_one[0] - 1.0
    ni = a_one[1]
    den = a_re * a_re + a_im * a_im
    co_re = (nr * a_re + ni * a_im) / den
    co_im = (ni * a_re - nr * a_im) / den
    bt = _cmul(co_re, co_im, btr_ref[...], bti_ref[...])
    ct = (cr_ref[...], ci_ref[...])

    def powers(k):
        kf = k.astype(_F32)
        e = jnp.exp(kf * sr)
        return e * jnp.cos(kf * ph), e * jnp.sin(kf * ph)

    def per_step(x):
        return jnp.broadcast_to(x[:, None, :], (c, h, lanes)).reshape(c * h, lanes)

    def per_chan(x):
        return jnp.broadcast_to(x[None, :, :], (c, h, lanes)).reshape(c * h, lanes)

    step = lax.broadcasted_iota(jnp.int32, (c, lanes), 0)
    fwd = lax.broadcasted_iota(jnp.int32, (c, lanes), 1) < _S5_P
    tb = tuple(per_chan(x) for x in bt)
    tc = tuple(per_chan(x) for x in ct)

    asc = powers(step)
    desc = powers(-step)
    a_last = (asc[0][c - 1:c, :], asc[1][c - 1:c, :])
    a_chunk = _cmul(*a_last, *a_one)

    def table(base, p_fwd, p_bwd):
        pw = tuple(jnp.where(fwd, f, b) for f, b in zip(p_fwd, p_bwd))
        return _cmul(*base, *(per_step(x) for x in pw))

    lmat = table(tb, desc, asc)
    rmat = table(tc, asc, desc)
    wmat = table(tb, _cmul(*a_last, *desc), asc)
    vmat = table(tc, _cmul(*a_one, *asc), _cmul(*a_chunk, *desc))

    def nt3(x, y):
        xh, xl = _split_bf16(x)
        yh, yl = _split_bf16(y)
        return _dot_nt(xh, yh) + _dot_nt(xl, yh) + _dot_nt(xh, yl)

    fwd_rows = lax.broadcasted_iota(jnp.int32, (c * h, lanes), 1) < _S5_P

    def kernel_half(keep):
        return (nt3(jnp.where(keep, lmat[0], 0.0), rmat[0])
                - nt3(jnp.where(keep, lmat[1], 0.0), rmat[1]))

    m = causal_ref[...] * kernel_half(fwd_rows) + anti_ref[...] * kernel_half(jnp.logical_not(fwd_rows))
    m_ref[...] = m.astype(m_ref.dtype)
    w_ref[...] = jnp.concatenate([wmat[0], wmat[1]], axis=1).astype(w_ref.dtype)
    vcat = jnp.concatenate([vmat[0], -vmat[1]], axis=1)
    keep_f = (lax.broadcasted_iota(jnp.int32, vcat.shape, 1) % lanes) < _S5_P
    vf_ref[...] = jnp.where(keep_f, vcat, 0.0).T.astype(vf_ref.dtype)
    vb_ref[...] = jnp.where(keep_f, 0.0, vcat).T.astype(vb_ref.dtype)
    a_ref[0:1, :] = a_chunk[0]
    a_ref[1:2, :] = a_chunk[1]


def _s5_mats_block_kernel(prm_ref, btr_ref, bti_ref, cr_ref, ci_ref, causal_ref, anti_ref,
                          m_ref, w_ref, vf_ref, vb_ref, a_ref):
    for j in range(prm_ref.shape[0]):
        _s5_mats_kernel(prm_ref.at[j], btr_ref.at[j], bti_ref.at[j], cr_ref.at[j], ci_ref.at[j],
                        causal_ref, anti_ref,
                        m_ref.at[j], w_ref.at[j], vf_ref.at[j], vb_ref.at[j], a_ref.at[j])


def _s5_mats(prm, bt_re, bt_im, c_re, c_im):
    g = prm.shape[0]
    cw, lanes = _CHUNK_W, _S5_LANES
    causal, anti = (jnp.asarray(a) for a in _s5_masks())
    gm = _S5_MATS_GB
    gspec = pl.BlockSpec((gm, _S5_H, lanes), lambda i: (i, 0, 0))
    mat_spec = pl.BlockSpec((gm, cw, cw), lambda i: (i, 0, 0))
    mat_sds = jax.ShapeDtypeStruct((g, cw, cw), _BF16)
    return pl.pallas_call(
        _s5_mats_block_kernel,
        grid=(g // gm,),
        in_specs=[pl.BlockSpec((gm, 3, lanes), lambda i: (i, 0, 0)), gspec, gspec, gspec, gspec,
                  _const_spec((cw, cw)), _const_spec((cw, cw))],
        out_specs=[mat_spec] * 4 + [pl.BlockSpec((gm, 2, lanes), lambda i: (i, 0, 0))],
        out_shape=[mat_sds] * 4 + [jax.ShapeDtypeStruct((g, 2, lanes), _F32)],
        compiler_params=_params(("arbitrary",)),
        name="s5_mats",
    )(prm, bt_re, bt_im, c_re, c_im, causal, anti)


def _s5_scan_kernel(nchunk, bsz, has_init, *refs):
    if has_init:
        u_ref, m_ref, w_ref, vf_ref, vb_ref, a_ref, h0_ref, y_ref = refs[:8]
    else:
        u_ref, m_ref, w_ref, vf_ref, vb_ref, a_ref, y_ref, fin_ref = refs[:8]
    s_ref, pf_ref, pb_ref = refs[8:]
    lanes, gb = _S5_LANES, _S5_GB
    rows = nchunk * bsz
    fwd = lax.broadcasted_iota(jnp.int32, (bsz, lanes), 1) < _S5_P

    for gl in range(gb):
        u = u_ref[gl].astype(_BF16)
        y_ref[gl] = jnp.dot(u, m_ref[gl], preferred_element_type=_F32)
        s = jnp.dot(u, w_ref[gl], preferred_element_type=_F32)
        s_ref[...] = jnp.swapaxes(s.reshape(bsz, nchunk, 2 * lanes), 0, 1).reshape(rows, 2 * lanes)

        a_re = a_ref[gl, 0:1, :]
        a_im = a_ref[gl, 1:2, :]
        if has_init:
            init = (h0_ref[gl, :, 0:lanes], h0_ref[gl, :, lanes:2 * lanes])
        else:
            init = (jnp.zeros((bsz, lanes), _F32), jnp.zeros((bsz, lanes), _F32))

        def step(k, carry):
            h_re, h_im = carry
            rowf = pl.ds(pl.multiple_of(k * bsz, bsz), bsz)
            rowb = pl.ds(pl.multiple_of((nchunk - 1 - k) * bsz, bsz), bsz)
            pf_ref[rowf, 0:lanes] = h_re
            pf_ref[rowf, lanes:2 * lanes] = h_im
            pb_ref[rowb, 0:lanes] = h_re
            pb_ref[rowb, lanes:2 * lanes] = h_im
            s_re = jnp.where(fwd, s_ref[rowf, 0:lanes], s_ref[rowb, 0:lanes])
            s_im = jnp.where(fwd, s_ref[rowf, lanes:2 * lanes], s_ref[rowb, lanes:2 * lanes])
            return (a_re * h_re - a_im * h_im + s_re, a_re * h_im + a_im * h_re + s_im)

        fin = lax.fori_loop(0, nchunk, step, init, unroll=2)
        if not has_init:
            fin_ref[gl, :, 0:lanes] = fin[0]
            fin_ref[gl, :, lanes:2 * lanes] = fin[1]
        carried = (jnp.dot(pf_ref[...].astype(_BF16), vf_ref[gl], preferred_element_type=_F32)
                   + jnp.dot(pb_ref[...].astype(_BF16), vb_ref[gl], preferred_element_type=_F32))
        carried = jnp.swapaxes(carried.reshape(nchunk, bsz, 2 * lanes), 0, 1).reshape(rows, 2 * lanes)
        y_ref[gl] = y_ref[gl] + carried


def _s5_scan(u, bsz, mats, init):
    g, rows, cw = u.shape
    nchunk = rows // bsz
    gb = _S5_GB
    tok_spec = pl.BlockSpec((gb, rows, cw), lambda i: (i, 0, 0))
    mat_spec = pl.BlockSpec((gb, cw, cw), lambda i: (i, 0, 0))
    st_spec = pl.BlockSpec((gb, bsz, cw), lambda i: (i, 0, 0))
    ins = [u] + list(mats)
    specs = [tok_spec] + [mat_spec] * 4 + [pl.BlockSpec((gb, 2, _S5_LANES), lambda i: (i, 0, 0))]
    y_sds = jax.ShapeDtypeStruct(u.shape, _F32)
    if init is not None:
        ins.append(init)
        specs.append(st_spec)
        out_specs, out_shape = tok_spec, y_sds
    else:
        out_specs = [tok_spec, st_spec]
        out_shape = [y_sds, jax.ShapeDtypeStruct((g, bsz, cw), _F32)]
    out = pl.pallas_call(
        functools.partial(_s5_scan_kernel, nchunk, bsz, init is not None),
        grid=(g // gb,),
        in_specs=specs, out_specs=out_specs, out_shape=out_shape,
        scratch_shapes=[pltpu.VMEM((rows, cw), _F32)] * 3,
        compiler_params=_params(("arbitrary",)),
        name=f"s5_{nchunk * _S5_CHUNK}",
    )(*ins)
    if init is not None:
        return out, None
    return out[0], out[1]


def _trunk(x, pos, row_of_b, shared_mod, init_state, mod5, filt, s5_mats, wts):
    bsz, seq, d = x.shape
    fold = _POST_TILE // seq if (shared_mod and seq < _POST_TILE) else 1
    rows = lambda a: a.reshape(bsz // fold, seq * fold, a.shape[-1])
    seqs = lambda a: a.reshape(bsz, seq, a.shape[-1])
    z = _pre_hyena(rows(x), pos, mod5, row_of_b, 0, wts["norm1_g"], wts["hy_in_w"], wts["hy_in_b"])
    v = _hyena_conv(seqs(z), wts["hy_conv_w"], wts["hy_conv_b"], wts["hy_fbias"], *filt)
    x1, u = _post_mixer("hyena", "next", rows(x), pos, rows(v), mod5, row_of_b, 0, wts["hy_out_w"],
                        wts["hy_out_b"], wts["norm1_g"], None, wts["norm2_g"], wts["ffn_w13"], wts["ffn_w2"],
                        None)
    ys, fin = _s5_scan(u, bsz, s5_mats, init_state)
    (out,) = _post_mixer("s5", "final", x1, None, ys, mod5, row_of_b, 1, wts["s5_glu_w"], wts["s5_glu_b"],
                         wts["norm1_g"], wts["s5_D"], wts["norm2_g"], wts["ffn_w13"], wts["ffn_w2"],
                         wts["final_g"])
    return seqs(out), fin


def kernel(x_prompt, x_sample, state_s5, c, c_ctx, norm1_g, norm2_g, final_g, ada_w, ada_b, ffn_w13, ffn_w2, hy_in_w, hy_in_b, hy_conv_w, hy_conv_b, hy_pe_w1, hy_pe_b1, hy_pe_w2, hy_pe_b2, hy_pe_w3, hy_freq, hy_fbias, hy_out_w, hy_out_b, s5_A_re, s5_A_im, s5_log_dt, s5_B_re, s5_B_im, s5_C_re, s5_C_im, s5_D, s5_glu_w, s5_glu_b):
    depth, d = norm1_g.shape
    assert depth == 2 and hy_in_w.shape[0] == 1 and s5_glu_w.shape[0] == 1
    dec_b, dec_seq, _ = x_sample.shape
    g = d // _S5_H
    p = _S5_P

    nrow = -(-(1 + dec_b) // 8) * 8
    cond = jnp.concatenate([c_ctx[None], c, jnp.zeros((nrow - 1 - dec_b, d), _F32)], axis=0)
    mod5 = _modulation(cond, ada_w, ada_b)

    row3 = lambda a: a.reshape(a.shape[0], 1, a.shape[-1])
    wts = dict(
        norm1_g=row3(norm1_g), norm2_g=row3(norm2_g), final_g=final_g[None],
        hy_in_w=hy_in_w.astype(_BF16), hy_in_b=row3(hy_in_b),
        hy_conv_w=hy_conv_w[0], hy_conv_b=hy_conv_b, hy_fbias=hy_fbias[0],
        hy_out_w=hy_out_w.astype(_BF16), hy_out_b=row3(hy_out_b),
        ffn_w13=ffn_w13.astype(_BF16), ffn_w2=ffn_w2.astype(_BF16),
        s5_glu_w=s5_glu_w.astype(_BF16), s5_glu_b=row3(s5_glu_b), s5_D=row3(s5_D),
    )
    dirs_on_lanes = lambda a: a.transpose(1, 0, 2).reshape(g, 2 * p)
    ldt = jnp.broadcast_to(s5_log_dt[0][:, :, None], (2, g, p))
    prm = jnp.stack([dirs_on_lanes(s5_A_re[0]), dirs_on_lanes(s5_A_im[0]), dirs_on_lanes(ldt)], axis=1)
    bt = lambda a: a.transpose(1, 3, 0, 2).reshape(g, _S5_H, 2 * p)
    ct = lambda a: a.transpose(1, 2, 0, 3).reshape(g, _S5_H, 2 * p)
    s5_mats = _s5_mats(prm, bt(s5_B_re[0]), bt(s5_B_im[0]), ct(s5_C_re[0]), ct(s5_C_im[0]))

    w1p = jnp.pad(hy_pe_w1[0], ((0, _HY_FO - _HY_EMB), (0, 0)))
    filt_args = (w1p, hy_pe_b1, hy_pe_w2[0], hy_pe_b2, hy_freq, hy_pe_w3[0])
    filt_ctx = _hyena_filters(x_prompt.shape[1], d, *filt_args)
    filt_lat = _hyena_filters(dec_seq, d, *filt_args)

    y_prompt, fin = _trunk(x_prompt, None, lambda b: 0, True, None, mod5, filt_ctx, s5_mats, wts)
    new_state = fin.reshape(g, -1, 2, 2, p).transpose(1, 3, 2, 0, 4)[:, None]

    init = state_s5[:, 0].transpose(3, 0, 2, 1, 4).reshape(g, dec_b, 4 * p)
    pos = jnp.asarray(_pos_embed(dec_seq // _GRID_W, d))
    y_sample, _ = _trunk(x_sample, pos, lambda b: b + 1, False, init, mod5, filt_lat, s5_mats, wts)
    return (y_prompt, y_sample, new_state)
```

```python
import functools
import math

import numpy as np
import jax
import jax.numpy as jnp
from jax import lax
from jax.experimental import pallas as pl
from jax.experimental.pallas import tpu as pltpu

_F32 = jnp.float32
_BF16 = jnp.bfloat16

_EPS = 1e-6
_GRID_W = 64
_POS_BASE = 10000.0
_HY_ORDER = 2
_HY_EMB = 33
_HY_FO = 64
_HY_TARGET = 1e-2
_HY_FAST = 0.3
_HY_SLOW = 1.5
_S5_H = 16
_S5_P = 64
_S5_CHUNK = 16
_CHUNK_W = _S5_CHUNK * _S5_H

_VMEM_LIMIT = 56 * 1024 * 1024
_TOKEN_TILE = 512
_POST_TILE = 512
_POST_SUB = 128
_CONV_DBLK = 512
_CONV_SEQS = 2
_FILT_DBLK = 512
_MOD_PARTS = 2


def _params(sem):
    return pltpu.CompilerParams(dimension_semantics=sem, vmem_limit_bytes=_VMEM_LIMIT)


def _dot_bf16(a, b):
    return jnp.dot(a.astype(_BF16), b.astype(_BF16), preferred_element_type=_F32)


def _split_bf16(x):
    hi = x.astype(_BF16)
    return hi, (x - hi.astype(_F32)).astype(_BF16)


def _dot_3pass(a, b):
    ah, al = _split_bf16(a)
    bh, bl = _split_bf16(b)
    return (jnp.dot(ah, bh, preferred_element_type=_F32) + jnp.dot(al, bh, preferred_element_type=_F32)
            + jnp.dot(ah, bl, preferred_element_type=_F32))


def _dot_nt(a, b, precision=None):
    return lax.dot_general(a, b, (((1,), (1,)), ((), ())), precision=precision,
                           preferred_element_type=_F32)


def _rmsnorm(x, g):
    ms = jnp.mean(x * x, axis=-1, keepdims=True)
    return x * lax.rsqrt(ms + _EPS) * g


def _silu(x):
    return x * jax.nn.sigmoid(x)


def _gelu_tanh(x):
    c = math.sqrt(2.0 / math.pi)
    return 0.5 * x * (1.0 + jnp.tanh(c * (x + 0.044715 * (x * x * x))))


def _const_spec(shape):
    nd = len(shape)
    return pl.BlockSpec(shape, lambda *_: (0,) * nd, pipeline_mode=pl.Buffered(1))


def _layer_spec(shape, layer):
    nd = len(shape)
    return pl.BlockSpec((None,) + tuple(shape), lambda *_: (layer,) + (0,) * nd,
                        pipeline_mode=pl.Buffered(1))


def _mod_spec(d, layer, part, row_of_b):
    return pl.BlockSpec((None, None, None, 1, d), lambda t, b: (layer, part, row_of_b(b), 0, 0))


def _mod_kernel(cond_ref, w_ref, b_ref, o_ref):
    d = o_ref.shape[-1]
    r = _dot_3pass(_silu(cond_ref[...]), w_ref[...])
    for p in range(o_ref.shape[0]):
        o_ref[p] = r[:, p * d:(p + 1) * d] + b_ref[p]


def _modulation(cond, ada_w, ada_b):
    depth, d, d6 = ada_w.shape
    parts = d6 // d
    rows = cond.shape[0]
    pb = _MOD_PARTS
    out = pl.pallas_call(
        _mod_kernel,
        grid=(depth, parts // pb),
        in_specs=[
            pl.BlockSpec((rows, d), lambda l, p: (0, 0)),
            pl.BlockSpec((None, d, pb * d), lambda l, p: (l, 0, p)),
            pl.BlockSpec((None, pb, 1, d), lambda l, p: (l, p, 0, 0)),
        ],
        out_specs=pl.BlockSpec((None, pb, rows, d), lambda l, p: (l, p, 0, 0)),
        out_shape=jax.ShapeDtypeStruct((depth, parts, rows, d), _F32),
        compiler_params=_params(("arbitrary", "arbitrary")),
        name="mod",
    )(cond, ada_w, ada_b.reshape(depth, parts, 1, d))
    return out.reshape(depth, parts, rows, 1, d)


@functools.lru_cache(maxsize=None)
def _dft_consts(seq):
    f = np.arange(seq)[:, None]
    t = np.arange(seq)[None, :]
    ang = np.pi * ((f * t) % (2 * seq)) / seq
    cos = np.cos(ang)
    sin = np.sin(ang)
    alt = np.where(np.arange(seq) % 2 == 0, 1.0, -1.0)
    fwd_top = cos
    fwd_bot = -sin
    fwd_bot[0] = alt
    wgt = np.full((seq, 1), 2.0)
    wgt[0] = 1.0
    inv_top = (cos * wgt).T / (2 * seq)
    inv_bot = (-2.0 * sin).T / (2 * seq)
    inv_bot[:, 0] = alt / (2 * seq)
    fwd = np.concatenate([fwd_top, fwd_bot], axis=0).astype(np.float32)
    inv = np.concatenate([inv_top, inv_bot], axis=1).astype(np.float32)
    return fwd, inv, alt.astype(np.float32)[:, None]


@functools.lru_cache(maxsize=None)
def _filter_consts(seq, d):
    t = np.linspace(0.0, 1.0, seq)[:, None]
    w = 2.0 * np.pi * np.arange(seq)[:, None] / seq
    nb = (_HY_EMB - 1) // 2
    bands = np.linspace(1e-4, nb - 1, nb)[None, :]
    z = np.concatenate([t, np.cos(bands * w), -np.sin(bands * w)], axis=-1)
    zpad = np.zeros((seq, _HY_FO))
    zpad[:, :_HY_EMB] = z
    max_decay = math.log(_HY_TARGET) / _HY_FAST
    min_decay = math.log(_HY_TARGET) / _HY_SLOW
    deltas = np.abs(np.linspace(min_decay, max_decay, d))[None, :]
    return zpad.astype(np.float32), deltas.astype(np.float32)


@functools.lru_cache(maxsize=None)
def _pos_embed(rows, d):
    quarter = d // 4
    omega = 1.0 / (_POS_BASE ** (np.arange(quarter, dtype=np.float64) / quarter))

    def axis_embed(n):
        ang = np.arange(n, dtype=np.float64)[:, None] * omega[None]
        return np.concatenate([np.sin(ang), np.cos(ang)], axis=-1)

    er = np.broadcast_to(axis_embed(rows)[:, None], (rows, _GRID_W, d // 2))
    ec = np.broadcast_to(axis_embed(_GRID_W)[None], (rows, _GRID_W, d // 2))
    return np.concatenate([er, ec], axis=-1).reshape(rows * _GRID_W, d).astype(np.float32)


@functools.lru_cache(maxsize=None)
def _s5_masks():
    step = np.repeat(np.arange(_S5_CHUNK), _S5_H)
    causal = (step[None, :] >= step[:, None]).astype(np.float32)
    anti = (step[:, None] >= step[None, :]).astype(np.float32)
    return causal, anti


def _filter_kernel(z_ref, w1_ref, b1_ref, w2_ref, b2_ref, fr_ref, w3f_ref, w3b_ref, dl_ref,
                   at_ref, ab_ref, alt_ref, krt_ref, krb_ref, ki_ref, h_ref):
    @pl.when((pl.program_id(0) == 0) & (pl.program_id(1) == 0))
    def _():
        fr = fr_ref[...]
        h1 = jnp.sin(fr * (_dot_3pass(z_ref[...], w1_ref[...]) + b1_ref[...]))
        h_ref[...] = jnp.sin(fr * (_dot_3pass(h1, w2_ref[...]) + b2_ref[...]))

    h = h_ref[...]
    decay = jnp.exp(-z_ref[:, 0:1] * dl_ref[...])
    hf = _dot_3pass(h, w3f_ref[...]) * decay
    hb = _dot_3pass(h, w3b_ref[...]) * decay
    norm = (jnp.sum(jnp.abs(hf), axis=0, keepdims=True)
            + jnp.sum(jnp.abs(hb), axis=0, keepdims=True) + _EPS)
    hf = hf / norm
    hb = hb / norm
    first = lax.broadcasted_iota(jnp.int32, hf.shape, 0) == 0
    hb = jnp.where(first, 0.0, hb)
    ksum = hf + hb
    kdiff = hf - hb
    kre = jnp.dot(at_ref[...], ksum.astype(_BF16), preferred_element_type=_F32)
    kim = jnp.dot(ab_ref[...], kdiff.astype(_BF16), preferred_element_type=_F32)
    nyq = jnp.sum(alt_ref[...] * ksum, axis=0, keepdims=True)
    krt_ref[...] = kre
    krb_ref[...] = jnp.where(first, nyq, kre)
    ki_ref[...] = jnp.where(first, 0.0, kim)


def _hyena_filters(seq, d, w1p, b1, w2, b2, freq, w3):
    zpad, deltas = _filter_consts(seq, d)
    fwd, _, alt = _dft_consts(seq)
    fwd = jnp.asarray(fwd).astype(_BF16)
    top = pl.BlockSpec((seq, seq), lambda o, j: (0, 0), pipeline_mode=pl.Buffered(1))
    bot = pl.BlockSpec((seq, seq), lambda o, j: (1, 0), pipeline_mode=pl.Buffered(1))
    nb = d // _FILT_DBLK
    fo = _HY_FO
    out_sds = jax.ShapeDtypeStruct((_HY_ORDER, seq, d), _F32)
    out_spec = pl.BlockSpec((None, seq, _FILT_DBLK), lambda o, j: (o, 0, j))
    return pl.pallas_call(
        _filter_kernel,
        grid=(_HY_ORDER, nb),
        in_specs=[
            _const_spec((seq, fo)), _const_spec((fo, fo)), _const_spec((1, fo)),
            _const_spec((fo, fo)), _const_spec((1, fo)), _const_spec((1, fo)),
            pl.BlockSpec((fo, _FILT_DBLK), lambda o, j: (0, (2 * o) * nb + j)),
            pl.BlockSpec((fo, _FILT_DBLK), lambda o, j: (0, (2 * o + 1) * nb + j)),
            pl.BlockSpec((1, _FILT_DBLK), lambda o, j: (0, j)),
            top, bot,
            _const_spec((seq, 1)),
        ],
        out_specs=[out_spec, out_spec, out_spec],
        out_shape=[out_sds, out_sds, out_sds],
        scratch_shapes=[pltpu.VMEM((seq, fo), _F32)],
        compiler_params=_params(("arbitrary", "arbitrary")),
        name=f"filt{seq}",
    )(jnp.asarray(zpad), w1p, b1, w2, b2, freq, w3, w3, jnp.asarray(deltas),
      fwd, fwd, jnp.asarray(alt))


def _pre_kernel(has_pos, *refs):
    if has_pos:
        x_ref, pos_ref, sh_ref, sc_ref, g_ref, w_ref, b_ref, z_ref = refs
        x = x_ref[...] + pos_ref[...]
    else:
        x_ref, sh_ref, sc_ref, g_ref, w_ref, b_ref, z_ref = refs
        x = x_ref[...]
    h = _rmsnorm(x, g_ref[...]) * (1.0 + sc_ref[...]) + sh_ref[...]
    z_ref[...] = (_dot_bf16(h, w_ref[...]) + b_ref[...]).astype(z_ref.dtype)


def _pre_hyena(x, pos, mod5, row_of_b, layer, norm_g, w_in, b_in):
    bsz, seq, d = x.shape
    n = w_in.shape[-1]
    tm = _TOKEN_TILE
    tok = pl.BlockSpec((None, tm, d), lambda t, b: (b, t, 0))
    ins, specs = [x], [tok]
    if pos is not None:
        ins.append(pos)
        specs.append(pl.BlockSpec((tm, d), lambda t, b: (t, 0)))
    ins += [mod5, mod5, norm_g, w_in, b_in]
    specs += [_mod_spec(d, layer, 0, row_of_b), _mod_spec(d, layer, 1, row_of_b),
              _layer_spec((1, d), layer), _layer_spec((d, n), 0), _layer_spec((1, n), 0)]
    return pl.pallas_call(
        functools.partial(_pre_kernel, pos is not None),
        grid=(seq // tm, bsz),
        in_specs=specs,
        out_specs=pl.BlockSpec((None, tm, n), lambda t, b: (b, t, 0)),
        out_shape=jax.ShapeDtypeStruct((bsz, seq, n), _BF16),
        compiler_params=_params(("arbitrary", "arbitrary")),
        name=f"pre{seq}",
    )(*ins)


def _conv_kernel(zv_ref, z1_ref, z2_ref, wv_ref, w1_ref, w2_ref, bv_ref, b1_ref, b2_ref, fb_ref,
                 krt_ref, krb_ref, ki_ref, ft_ref, fb2_ref, it_ref, ib_ref, o_ref):
    nseq, seq, dblk = zv_ref.shape
    row = lax.broadcasted_iota(jnp.int32, (seq, dblk), 0)
    first = row == 0
    last = row == seq - 1

    def short_conv(z_ref, w_ref, b_ref, i):
        z = z_ref[i].astype(_F32)
        prev = jnp.where(first, 0.0, pltpu.roll(z, 1, 0))
        nxt = jnp.where(last, 0.0, pltpu.roll(z, seq - 1, 0))
        w = w_ref[...]
        return prev * w[0:1] + z * w[1:2] + nxt * w[2:3] + b_ref[...]

    gate_refs = ((z1_ref, w1_ref, b1_ref), (z2_ref, w2_ref, b2_ref))

    def stages(i):
        st = {}

        def start():
            st["v"] = short_conv(zv_ref, wv_ref, bv_ref, i)

        def forward():
            vb = st["v"].astype(_BF16)
            st["p"] = jnp.dot(ft_ref[...], vb, preferred_element_type=_F32)
            st["q"] = jnp.dot(fb2_ref[...], vb, preferred_element_type=_F32)

        def spectrum(o):
            p, q, ki = st.pop("p"), st.pop("q"), ki_ref[o]
            st["yt"] = (p * krt_ref[o] - q * ki).astype(_BF16)
            st["yb"] = (p * ki + q * krb_ref[o]).astype(_BF16)
            st["g"] = short_conv(*gate_refs[o], i)

        def inverse():
            st["y"] = (jnp.dot(it_ref[...], st.pop("yt"), preferred_element_type=_F32)
                       + jnp.dot(ib_ref[...], st.pop("yb"), preferred_element_type=_F32))

        def update(o):
            st["v"] = (st.pop("y") + st["v"] * fb_ref[o:o + 1, :]) * st.pop("g")
            if o == _HY_ORDER - 1:
                o_ref[i] = st.pop("v").astype(o_ref.dtype)

        pipe = [start]
        for o in range(_HY_ORDER):
            pipe += [forward, functools.partial(spectrum, o), inverse, functools.partial(update, o)]
        return pipe

    pipes = [stages(i) for i in range(nseq)]
    depth = len(pipes[0])
    for t in range(depth + nseq - 1):
        for i, pipe in enumerate(pipes):
            if 0 <= t - i < depth:
                pipe[t - i]()


def _hyena_conv(z, conv_w, conv_b, fbias, krt, krb, ki):
    bsz, seq, d3 = z.shape
    d = d3 // 3
    dblk = _CONV_DBLK
    nb = d // dblk
    fwd, inv, _ = _dft_consts(seq)
    fwd = jnp.asarray(fwd).astype(_BF16)
    inv = jnp.asarray(inv).astype(_BF16)

    def mspec(r, c):
        return pl.BlockSpec((seq, seq), lambda j, b: (r, c), pipeline_mode=pl.Buffered(1))

    nseq = _CONV_SEQS
    assert bsz % nseq == 0 and d % dblk == 0

    def zspec(k):
        return pl.BlockSpec((nseq, seq, dblk), lambda j, b: (b, 0, k * nb + j))

    def wspec(rows, k):
        return pl.BlockSpec((rows, dblk), lambda j, b: (0, k * nb + j))

    kspec = pl.BlockSpec((_HY_ORDER, seq, dblk), lambda j, b: (0, 0, j), pipeline_mode=pl.Buffered(1))
    return pl.pallas_call(
        _conv_kernel,
        grid=(nb, bsz // nseq),
        in_specs=[zspec(0), zspec(1), zspec(2),
                  wspec(3, 0), wspec(3, 1), wspec(3, 2),
                  wspec(1, 0), wspec(1, 1), wspec(1, 2),
                  wspec(_HY_ORDER, 0),
                  kspec, kspec, kspec,
                  mspec(0, 0), mspec(1, 0), mspec(0, 0), mspec(0, 1)],
        out_specs=pl.BlockSpec((nseq, seq, dblk), lambda j, b: (b, 0, j)),
        out_shape=jax.ShapeDtypeStruct((bsz, seq, d), _BF16),
        compiler_params=_params(("arbitrary", "arbitrary")),
        name=f"conv{seq}",
    )(z, z, z, conv_w, conv_w, conv_w, conv_b, conv_b, conv_b, fbias, krt, krb, ki,
      fwd, fwd, inv, inv)


def _segment_transpose(xs):
    n = len(xs)
    seg = lax.broadcasted_iota(jnp.int32, xs[0].shape, 1) // _S5_H
    diags = []
    for k in range(n):
        z = xs[-k % n]
        for m in range(1, n):
            z = jnp.where(seg == m, xs[(m - k) % n], z)
        diags.append(pltpu.roll(z, ((n - k) % n) * _S5_H, 1) if k else z)
    ys = []
    for b in range(n):
        y = diags[b]
        for a in range(1, n):
            y = jnp.where(seg == a, diags[(b - a) % n], y)
        ys.append(y)
    return ys


def _to_group_major(x, o_ref, crows):
    sm = jnp.swapaxes(x, 0, 1)
    for j in range(x.shape[-1] // 128):
        for half in range(2):
            per_group = _segment_transpose([sm[half * 8 + s8, :, j * 128:(j + 1) * 128] for s8 in range(8)])
            for gl in range(8):
                o_ref[j * 8 + gl, crows, half * 128:(half + 1) * 128] = per_group[gl]


def _from_group_major(m_ref, crows, d):
    per_step = [[None] * (d // 128) for _ in range(_S5_CHUNK)]
    for j in range(d // 128):
        for half in range(2):
            cols = slice(half * 128, (half + 1) * 128)
            steps = _segment_transpose([m_ref[j * 8 + gl, crows, cols] for gl in range(8)])
            for t8 in range(8):
                per_step[half * 8 + t8][j] = steps[t8]
    sm = jnp.stack([jnp.concatenate(tiles, axis=1) for tiles in per_step])
    return jnp.swapaxes(sm, 0, 1)


def _post_kernel(kind, tail, names, *refs):
    r = dict(zip(names, refs))
    tm, d = r["x"].shape
    steps = _S5_CHUNK
    nsub = max(tm // _POST_SUB, 1)
    sub = tm // nsub

    def stages(i):
        rows = slice(i * sub, (i + 1) * sub)
        crows = slice(i * sub // steps, (i + 1) * sub // steps)
        st = {}

        def head():
            st["x"] = r["x"][rows, :]
            if "pos" in r:
                st["x"] = st["x"] + r["pos"][rows, :]
            if kind == "hyena":
                st["mix"] = r["mix"][rows, :]
            else:
                u = _rmsnorm(st["x"], r["n1"][...]) * (1.0 + r["sc1"][...]) + r["sh1"][...]
                mix = _from_group_major(r["mix"], crows, d).reshape(sub, d)
                st["mix"] = _gelu_tanh(mix + u * r["skip"][...]).astype(_BF16)

        def mixer_out():
            st["m"] = _dot_bf16(st.pop("mix"), r["w_out"][...]) + r["b_out"][...]

        def mid():
            m = st.pop("m")
            if kind == "s5":
                half = m.shape[-1] // 2
                m = m[:, :half] * jax.nn.sigmoid(m[:, half:])
            st["x"] = st["x"] + r["g1"][...] * m
            h = _rmsnorm(st["x"], r["n2"][...]) * (1.0 + r["sc2"][...]) + r["sh2"][...]
            st["h"] = h.astype(_BF16)

        def ffn_in():
            st["ab"] = _dot_bf16(st.pop("h"), r["w13"][...])

        def act():
            ab = st.pop("ab")
            ff = ab.shape[-1] // 2
            st["act"] = (_silu(ab[:, :ff]) * ab[:, ff:]).astype(_BF16)

        def ffn_out():
            st["f"] = _dot_bf16(st.pop("act"), r["w2"][...])

        def finish():
            x = st.pop("x") + r["g2"][...] * st.pop("f")
            if tail == "next":
                r["o_x"][rows, :] = x
                un = _rmsnorm(x, r["nn"][...]) * (1.0 + r["scn"][...]) + r["shn"][...]
                _to_group_major(un.reshape(-1, steps, d), r["o_u"], crows)
            else:
                r["o_x"][rows, :] = _rmsnorm(x, r["nf"][...])

        return [head, mixer_out, mid, ffn_in, act, ffn_out, finish]

    pipes = [stages(i) for i in range(nsub)]
    depth = len(pipes[0])
    for t in range(depth + nsub - 1):
        for i, pipe in enumerate(pipes):
            if 0 <= t - i < depth:
                pipe[t - i]()


def _post_mixer(kind, tail, x, pos, mix, mod5, row_of_b, layer, w_out, b_out, norm1_g, skip,
                norm2_g, w13, w2, final_g):
    bsz, seq, d = x.shape
    tm = min(_POST_TILE, seq)
    tok = pl.BlockSpec((None, tm, d), lambda t, b: (b, t, 0))
    nt = seq // tm
    steps = _S5_CHUNK
    groups = d // _S5_H
    stepmajor = pl.BlockSpec((groups, tm // steps, _CHUNK_W), lambda t, b: (0, b * nt + t, 0))
    stepmajor_sds = jax.ShapeDtypeStruct((groups, bsz * seq // steps, _CHUNK_W), _F32)
    n_out = w_out.shape[-1]
    f2 = w13.shape[-1]

    def mod(part, lyr=layer):
        return (mod5, _mod_spec(d, lyr, part, row_of_b))

    items = [("x", x, tok), ("mix", mix, stepmajor if kind == "s5" else tok),
             ("w_out", w_out, _layer_spec((d, n_out), 0)), ("b_out", b_out, _layer_spec((1, n_out), 0)),
             ("g1",) + mod(2), ("sh2",) + mod(3), ("sc2",) + mod(4), ("g2",) + mod(5),
             ("n2", norm2_g, _layer_spec((1, d), layer)),
             ("w13", w13, _layer_spec((d, f2), layer)), ("w2", w2, _layer_spec((f2 // 2, d), layer))]
    if kind == "s5":
        items += [("n1", norm1_g, _layer_spec((1, d), layer)), ("sh1",) + mod(0), ("sc1",) + mod(1),
                  ("skip", skip, _layer_spec((1, d), 0))]
    if pos is not None:
        items.append(("pos", pos, pl.BlockSpec((tm, d), lambda t, b: (t, 0))))
    out_specs, out_shape = [tok], [jax.ShapeDtypeStruct(x.shape, _F32)]
    names_out = ["o_x"]
    if tail == "next":
        items += [("nn", norm1_g, _layer_spec((1, d), layer + 1)),
                  ("shn",) + mod(0, layer + 1), ("scn",) + mod(1, layer + 1)]
        out_specs.append(stepmajor)
        out_shape.append(stepmajor_sds)
        names_out.append("o_u")
    else:
        items.append(("nf", final_g, _const_spec((1, d))))
    names = tuple(i[0] for i in items) + tuple(names_out)
    out = pl.pallas_call(
        functools.partial(_post_kernel, kind, tail, names),
        grid=(seq // tm, bsz),
        in_specs=[i[2] for i in items], out_specs=out_specs, out_shape=out_shape,
        compiler_params=_params(("arbitrary", "arbitrary")),
        name=f"post_{kind}{seq}",
    )(*[i[1] for i in items])
    return out


_S5_GB = 8
_S5_LANES = 2 * _S5_P
_S5_MATS_GB = 8
_S5_SCAN_UNROLL = 4


def _cmul(ar, ai, br, bi):
    return ar * br - ai * bi, ar * bi + ai * br


def _s5_mats_kernel(prm_ref, btr_ref, bti_ref, cr_ref, ci_ref, causal_ref, anti_ref,
                    m_ref, w_ref, vf_ref, vb_ref, a_ref):
    c, h, lanes = _S5_CHUNK, _S5_H, _S5_LANES
    a_re = prm_ref[0:1, :]
    a_im = prm_ref[1:2, :]
    dt = jnp.exp(prm_ref[2:3, :])
    sr = dt * a_re
    ph = dt * a_im
    mag = jnp.exp(sr)
    a_one = (mag * jnp.cos(ph), mag * jnp.sin(ph))
    nr = a_one[0] - 1.0
    ni = a_one[1]
    den = a_re * a_re + a_im * a_im
    co_re = (nr * a_re + ni * a_im) / den
    co_im = (ni * a_re - nr * a_im) / den
    bt = _cmul(co_re, co_im, btr_ref[...], bti_ref[...])
    ct = (cr_ref[...], ci_ref[...])

    def powers(k):
        kf = k.astype(_F32)
        e = jnp.exp(kf * sr)
        return e * jnp.cos(kf * ph), e * jnp.sin(kf * ph)

    def per_step(x):
        return jnp.broadcast_to(x[:, None, :], (c, h, lanes)).reshape(c * h, lanes)

    def per_chan(x):
        return jnp.broadcast_to(x[None, :, :], (c, h, lanes)).reshape(c * h, lanes)

    step = lax.broadcasted_iota(jnp.int32, (c, lanes), 0)
    fwd = lax.broadcasted_iota(jnp.int32, (c, lanes), 1) < _S5_P
    tb = tuple(per_chan(x) for x in bt)
    tc = tuple(per_chan(x) for x in ct)

    asc = powers(step)
    desc = powers(-step)
    a_last = (asc[0][c - 1:c, :], asc[1][c - 1:c, :])
    a_chunk = _cmul(*a_last, *a_one)

    def table(base, p_fwd, p_bwd):
        pw = tuple(jnp.where(fwd, f, b) for f, b in zip(p_fwd, p_bwd))
        return _cmul(*base, *(per_step(x) for x in pw))

    lmat = table(tb, desc, asc)
    rmat = table(tc, asc, desc)
    wmat = table(tb, _cmul(*a_last, *desc), asc)
    vmat = table(tc, _cmul(*a_one, *asc), _cmul(*a_chunk, *desc))

    def nt3(x, y):
        xh, xl = _split_bf16(x)
        yh, yl = _split_bf16(y)
        return _dot_nt(xh, yh) + _dot_nt(xl, yh) + _dot_nt(xh, yl)

    fwd_rows = lax.broadcasted_iota(jnp.int32, (c * h, lanes), 1) < _S5_P

    def kernel_half(keep):
        return (nt3(jnp.where(keep, lmat[0], 0.0), rmat[0])
                - nt3(jnp.where(keep, lmat[1], 0.0), rmat[1]))

    m = causal_ref[...] * kernel_half(fwd_rows) + anti_ref[...] * kernel_half(jnp.logical_not(fwd_rows))
    m_ref[...] = m.astype(m_ref.dtype)
    w_ref[...] = jnp.concatenate([wmat[0], wmat[1]], axis=1).astype(w_ref.dtype)
    vcat = jnp.concatenate([vmat[0], -vmat[1]], axis=1)
    keep_f = (lax.broadcasted_iota(jnp.int32, vcat.shape, 1) % lanes) < _S5_P
    vf_ref[...] = jnp.where(keep_f, vcat, 0.0).T.astype(vf_ref.dtype)
    vb_ref[...] = jnp.where(keep_f, 0.0, vcat).T.astype(vb_ref.dtype)
    a_ref[0:1, :] = a_chunk[0]
    a_ref[1:2, :] = a_chunk[1]


def _s5_mats_block_kernel(prm_ref, btr_ref, bti_ref, cr_ref, ci_ref, causal_ref, anti_ref,
                          m_ref, w_ref, vf_ref, vb_ref, a_ref):
    for j in range(prm_ref.shape[0]):
        _s5_mats_kernel(prm_ref.at[j], btr_ref.at[j], bti_ref.at[j], cr_ref.at[j], ci_ref.at[j],
                        causal_ref, anti_ref,
                        m_ref.at[j], w_ref.at[j], vf_ref.at[j], vb_ref.at[j], a_ref.at[j])


def _s5_mats(prm, bt_re, bt_im, c_re, c_im):
    g = prm.shape[0]
    cw, lanes = _CHUNK_W, _S5_LANES
    causal, anti = (jnp.asarray(a) for a in _s5_masks())
    gm = _S5_MATS_GB
    gspec = pl.BlockSpec((gm, _S5_H, lanes), lambda i: (i, 0, 0))
    mat_spec = pl.BlockSpec((gm, cw, cw), lambda i: (i, 0, 0))
    mat_sds = jax.ShapeDtypeStruct((g, cw, cw), _BF16)
    return pl.pallas_call(
        _s5_mats_block_kernel,
        grid=(g // gm,),
        in_specs=[pl.BlockSpec((gm, 3, lanes), lambda i: (i, 0, 0)), gspec, gspec, gspec, gspec,
                  _const_spec((cw, cw)), _const_spec((cw, cw))],
        out_specs=[mat_spec] * 4 + [pl.BlockSpec((gm, 2, lanes), lambda i: (i, 0, 0))],
        out_shape=[mat_sds] * 4 + [jax.ShapeDtypeStruct((g, 2, lanes), _F32)],
        compiler_params=_params(("arbitrary",)),
        name="s5_mats",
    )(prm, bt_re, bt_im, c_re, c_im, causal, anti)


def _s5_scan_kernel(nchunk, bsz, has_init, *refs):
    if has_init:
        u_ref, m_ref, w_ref, vf_ref, vb_ref, a_ref, h0_ref, y_ref = refs[:8]
    else:
        u_ref, m_ref, w_ref, vf_ref, vb_ref, a_ref, y_ref, fin_ref = refs[:8]
    s_ref, pf_ref, pb_ref = refs[8:]
    lanes, gb = _S5_LANES, _S5_GB
    rows = nchunk * bsz
    fwd = lax.broadcasted_iota(jnp.int32, (bsz, lanes), 1) < _S5_P

    for gl in range(gb):
        u = u_ref[gl].astype(_BF16)
        y_ref[gl] = jnp.dot(u, m_ref[gl], preferred_element_type=_F32)
        s = jnp.dot(u, w_ref[gl], preferred_element_type=_F32)
        s_ref[...] = jnp.swapaxes(s.reshape(bsz, nchunk, 2 * lanes), 0, 1).reshape(rows, 2 * lanes)

        a_re = a_ref[gl, 0:1, :]
        a_im = a_ref[gl, 1:2, :]
        if has_init:
            init = (h0_ref[gl, :, 0:lanes], h0_ref[gl, :, lanes:2 * lanes])
        else:
            init = (jnp.zeros((bsz, lanes), _F32), jnp.zeros((bsz, lanes), _F32))

        def step(k, carry):
            h_re, h_im = carry
            rowf = pl.ds(pl.multiple_of(k * bsz, bsz), bsz)
            rowb = pl.ds(pl.multiple_of((nchunk - 1 - k) * bsz, bsz), bsz)
            pf_ref[rowf, 0:lanes] = h_re
            pf_ref[rowf, lanes:2 * lanes] = h_im
            pb_ref[rowb, 0:lanes] = h_re
            pb_ref[rowb, lanes:2 * lanes] = h_im
            s_re = jnp.where(fwd, s_ref[rowf, 0:lanes], s_ref[rowb, 0:lanes])
            s_im = jnp.where(fwd, s_ref[rowf, lanes:2 * lanes], s_ref[rowb, lanes:2 * lanes])
            return (a_re * h_re - a_im * h_im + s_re, a_re * h_im + a_im * h_re + s_im)

        fin = lax.fori_loop(0, nchunk, step, init, unroll=_S5_SCAN_UNROLL)
        if not has_init:
            fin_ref[gl, :, 0:lanes] = fin[0]
            fin_ref[gl, :, lanes:2 * lanes] = fin[1]
        carried = (jnp.dot(pf_ref[...].astype(_BF16), vf_ref[gl], preferred_element_type=_F32)
                   + jnp.dot(pb_ref[...].astype(_BF16), vb_ref[gl], preferred_element_type=_F32))
        carried = jnp.swapaxes(carried.reshape(nchunk, bsz, 2 * lanes), 0, 1).reshape(rows, 2 * lanes)
        y_ref[gl] = y_ref[gl] + carried


def _s5_scan(u, bsz, mats, init):
    g, rows, cw = u.shape
    nchunk = rows // bsz
    gb = _S5_GB
    tok_spec = pl.BlockSpec((gb, rows, cw), lambda i: (i, 0, 0))
    mat_spec = pl.BlockSpec((gb, cw, cw), lambda i: (i, 0, 0))
    st_spec = pl.BlockSpec((gb, bsz, cw), lambda i: (i, 0, 0))
    ins = [u] + list(mats)
    specs = [tok_spec] + [mat_spec] * 4 + [pl.BlockSpec((gb, 2, _S5_LANES), lambda i: (i, 0, 0))]
    y_sds = jax.ShapeDtypeStruct(u.shape, _F32)
    if init is not None:
        ins.append(init)
        specs.append(st_spec)
        out_specs, out_shape = tok_spec, y_sds
    else:
        out_specs = [tok_spec, st_spec]
        out_shape = [y_sds, jax.ShapeDtypeStruct((g, bsz, cw), _F32)]
    out = pl.pallas_call(
        functools.partial(_s5_scan_kernel, nchunk, bsz, init is not None),
        grid=(g // gb,),
        in_specs=specs, out_specs=out_specs, out_shape=out_shape,
        scratch_shapes=[pltpu.VMEM((rows, cw), _F32)] * 3,
        compiler_params=_params(("arbitrary",)),
        name=f"s5_{nchunk * _S5_CHUNK}",
    )(*ins)
    if init is not None:
        return out, None
    return out[0], out[1]


def _trunk(x, pos, row_of_b, shared_mod, init_state, mod5, filt, s5_mats, wts):
    bsz, seq, d = x.shape
    fold = _POST_TILE // seq if (shared_mod and seq < _POST_TILE) else 1
    rows = lambda a: a.reshape(bsz // fold, seq * fold, a.shape[-1])
    seqs = lambda a: a.reshape(bsz, seq, a.shape[-1])
    z = _pre_hyena(rows(x), pos, mod5, row_of_b, 0, wts["norm1_g"], wts["hy_in_w"], wts["hy_in_b"])
    v = _hyena_conv(seqs(z), wts["hy_conv_w"], wts["hy_conv_b"], wts["hy_fbias"], *filt)
    x1, u = _post_mixer("hyena", "next", rows(x), pos, rows(v), mod5, row_of_b, 0, wts["hy_out_w"],
                        wts["hy_out_b"], wts["norm1_g"], None, wts["norm2_g"], wts["ffn_w13"], wts["ffn_w2"],
                        None)
    ys, fin = _s5_scan(u, bsz, s5_mats, init_state)
    (out,) = _post_mixer("s5", "final", x1, None, ys, mod5, row_of_b, 1, wts["s5_glu_w"], wts["s5_glu_b"],
                         wts["norm1_g"], wts["s5_D"], wts["norm2_g"], wts["ffn_w13"], wts["ffn_w2"],
                         wts["final_g"])
    return seqs(out), fin


def kernel(x_prompt, x_sample, state_s5, c, c_ctx, norm1_g, norm2_g, final_g, ada_w, ada_b, ffn_w13, ffn_w2, hy_in_w, hy_in_b, hy_conv_w, hy_conv_b, hy_pe_w1, hy_pe_b1, hy_pe_w2, hy_pe_b2, hy_pe_w3, hy_freq, hy_fbias, hy_out_w, hy_out_b, s5_A_re, s5_A_im, s5_log_dt, s5_B_re, s5_B_im, s5_C_re, s5_C_im, s5_D, s5_glu_w, s5_glu_b):
    depth, d = norm1_g.shape
    assert depth == 2 and hy_in_w.shape[0] == 1 and s5_glu_w.shape[0] == 1
    dec_b, dec_seq, _ = x_sample.shape
    g = d // _S5_H
    p = _S5_P

    nrow = -(-(1 + dec_b) // 8) * 8
    cond = jnp.concatenate([c_ctx[None], c, jnp.zeros((nrow - 1 - dec_b, d), _F32)], axis=0)
    mod5 = _modulation(cond, ada_w, ada_b)

    row3 = lambda a: a.reshape(a.shape[0], 1, a.shape[-1])
    wts = dict(
        norm1_g=row3(norm1_g), norm2_g=row3(norm2_g), final_g=final_g[None],
        hy_in_w=hy_in_w.astype(_BF16), hy_in_b=row3(hy_in_b),
        hy_conv_w=hy_conv_w[0], hy_conv_b=hy_conv_b, hy_fbias=hy_fbias[0],
        hy_out_w=hy_out_w.astype(_BF16), hy_out_b=row3(hy_out_b),
        ffn_w13=ffn_w13.astype(_BF16), ffn_w2=ffn_w2.astype(_BF16),
        s5_glu_w=s5_glu_w.astype(_BF16), s5_glu_b=row3(s5_glu_b), s5_D=row3(s5_D),
    )
    dirs_on_lanes = lambda a: a.transpose(1, 0, 2).reshape(g, 2 * p)
    ldt = jnp.broadcast_to(s5_log_dt[0][:, :, None], (2, g, p))
    prm = jnp.stack([dirs_on_lanes(s5_A_re[0]), dirs_on_lanes(s5_A_im[0]), dirs_on_lanes(ldt)], axis=1)
    bt = lambda a: a.transpose(1, 3, 0, 2).reshape(g, _S5_H, 2 * p)
    ct = lambda a: a.transpose(1, 2, 0, 3).reshape(g, _S5_H, 2 * p)
    s5_mats = _s5_mats(prm, bt(s5_B_re[0]), bt(s5_B_im[0]), ct(s5_C_re[0]), ct(s5_C_im[0]))

    w1p = jnp.pad(hy_pe_w1[0], ((0, _HY_FO - _HY_EMB), (0, 0)))
    filt_args = (w1p, hy_pe_b1, hy_pe_w2[0], hy_pe_b2, hy_freq, hy_pe_w3[0])
    filt_ctx = _hyena_filters(x_prompt.shape[1], d, *filt_args)
    filt_lat = _hyena_filters(dec_seq, d, *filt_args)

    y_prompt, fin = _trunk(x_prompt, None, lambda b: 0, True, None, mod5, filt_ctx, s5_mats, wts)
    new_state = fin.reshape(g, -1, 2, 2, p).transpose(1, 3, 2, 0, 4)[:, None]

    init = state_s5[:, 0].transpose(3, 0, 2, 1, 4).reshape(g, dec_b, 4 * p)
    pos = jnp.asarray(_pos_embed(dec_seq // _GRID_W, d))
    y_sample, _ = _trunk(x_sample, pos, lambda b: b + 1, False, init, mod5, filt_lat, s5_mats, wts)
    return (y_prompt, y_sample, new_state)
```

```python
import functools
import math

import numpy as np
import jax
import jax.numpy as jnp
from jax import lax
from jax.experimental import pallas as pl
from jax.experimental.pallas import tpu as pltpu

_F32 = jnp.float32
_BF16 = jnp.bfloat16

_EPS = 1e-6
_GRID_W = 64
_POS_BASE = 10000.0
_HY_ORDER = 2
_HY_EMB = 33
_HY_FO = 64
_HY_TARGET = 1e-2
_HY_FAST = 0.3
_HY_SLOW = 1.5
_S5_H = 16
_S5_P = 64
_S5_CHUNK = 16
_CHUNK_W = _S5_CHUNK * _S5_H

_VMEM_LIMIT = 56 * 1024 * 1024
_TOKEN_TILE = 1024
_POST_TILE = 512
_POST_SUB = 128
_CONV_DBLK = 512
_CONV_SEQS = 2
_FILT_DBLK = 512
_MOD_PARTS = 2


def _params(sem):
    return pltpu.CompilerParams(dimension_semantics=sem, vmem_limit_bytes=_VMEM_LIMIT)


def _dot_bf16(a, b):
    return jnp.dot(a.astype(_BF16), b.astype(_BF16), preferred_element_type=_F32)


def _split_bf16(x):
    hi = x.astype(_BF16)
    return hi, (x - hi.astype(_F32)).astype(_BF16)


def _dot_3pass(a, b):
    ah, al = _split_bf16(a)
    bh, bl = _split_bf16(b)
    return (jnp.dot(ah, bh, preferred_element_type=_F32) + jnp.dot(al, bh, preferred_element_type=_F32)
            + jnp.dot(ah, bl, preferred_element_type=_F32))


def _dot_nt(a, b, precision=None):
    return lax.dot_general(a, b, (((1,), (1,)), ((), ())), precision=precision,
                           preferred_element_type=_F32)


def _rmsnorm(x, g):
    ms = jnp.mean(x * x, axis=-1, keepdims=True)
    return x * lax.rsqrt(ms + _EPS) * g


def _silu(x):
    return x * jax.nn.sigmoid(x)


def _gelu_tanh(x):
    c = math.sqrt(2.0 / math.pi)
    return 0.5 * x * (1.0 + jnp.tanh(c * (x + 0.044715 * (x * x * x))))


def _const_spec(shape):
    nd = len(shape)
    return pl.BlockSpec(shape, lambda *_: (0,) * nd, pipeline_mode=pl.Buffered(1))


def _layer_spec(shape, layer):
    nd = len(shape)
    return pl.BlockSpec((None,) + tuple(shape), lambda *_: (layer,) + (0,) * nd,
                        pipeline_mode=pl.Buffered(1))


def _mod_spec(d, layer, part, row_of_b):
    return pl.BlockSpec((None, None, None, 1, d), lambda t, b: (layer, part, row_of_b(b), 0, 0))


def _mod_kernel(cond_ref, w_ref, b_ref, o_ref):
    d = o_ref.shape[-1]
    r = _dot_3pass(_silu(cond_ref[...]), w_ref[...])
    for p in range(o_ref.shape[0]):
        o_ref[p] = r[:, p * d:(p + 1) * d] + b_ref[p]


def _modulation(cond, ada_w, ada_b):
    depth, d, d6 = ada_w.shape
    parts = d6 // d
    rows = cond.shape[0]
    pb = _MOD_PARTS
    out = pl.pallas_call(
        _mod_kernel,
        grid=(depth, parts // pb),
        in_specs=[
            pl.BlockSpec((rows, d), lambda l, p: (0, 0)),
            pl.BlockSpec((None, d, pb * d), lambda l, p: (l, 0, p)),
            pl.BlockSpec((None, pb, 1, d), lambda l, p: (l, p, 0, 0)),
        ],
        out_specs=pl.BlockSpec((None, pb, rows, d), lambda l, p: (l, p, 0, 0)),
        out_shape=jax.ShapeDtypeStruct((depth, parts, rows, d), _F32),
        compiler_params=_params(("arbitrary", "arbitrary")),
        name="mod",
    )(cond, ada_w, ada_b.reshape(depth, parts, 1, d))
    return out.reshape(depth, parts, rows, 1, d)


@functools.lru_cache(maxsize=None)
def _dft_consts(seq):
    f = np.arange(seq)[:, None]
    t = np.arange(seq)[None, :]
    ang = np.pi * ((f * t) % (2 * seq)) / seq
    cos = np.cos(ang)
    sin = np.sin(ang)
    alt = np.where(np.arange(seq) % 2 == 0, 1.0, -1.0)
    fwd_top = cos
    fwd_bot = -sin
    fwd_bot[0] = alt
    wgt = np.full((seq, 1), 2.0)
    wgt[0] = 1.0
    inv_top = (cos * wgt).T / (2 * seq)
    inv_bot = (-2.0 * sin).T / (2 * seq)
    inv_bot[:, 0] = alt / (2 * seq)
    fwd = np.concatenate([fwd_top, fwd_bot], axis=0).astype(np.float32)
    inv = np.concatenate([inv_top, inv_bot], axis=1).astype(np.float32)
    return fwd, inv, alt.astype(np.float32)[:, None]


@functools.lru_cache(maxsize=None)
def _filter_consts(seq, d):
    t = np.linspace(0.0, 1.0, seq)[:, None]
    w = 2.0 * np.pi * np.arange(seq)[:, None] / seq
    nb = (_HY_EMB - 1) // 2
    bands = np.linspace(1e-4, nb - 1, nb)[None, :]
    z = np.concatenate([t, np.cos(bands * w), -np.sin(bands * w)], axis=-1)
    zpad = np.zeros((seq, _HY_FO))
    zpad[:, :_HY_EMB] = z
    max_decay = math.log(_HY_TARGET) / _HY_FAST
    min_decay = math.log(_HY_TARGET) / _HY_SLOW
    deltas = np.abs(np.linspace(min_decay, max_decay, d))[None, :]
    return zpad.astype(np.float32), deltas.astype(np.float32)


@functools.lru_cache(maxsize=None)
def _pos_embed(rows, d):
    quarter = d // 4
    omega = 1.0 / (_POS_BASE ** (np.arange(quarter, dtype=np.float64) / quarter))

    def axis_embed(n):
        ang = np.arange(n, dtype=np.float64)[:, None] * omega[None]
        return np.concatenate([np.sin(ang), np.cos(ang)], axis=-1)

    er = np.broadcast_to(axis_embed(rows)[:, None], (rows, _GRID_W, d // 2))
    ec = np.broadcast_to(axis_embed(_GRID_W)[None], (rows, _GRID_W, d // 2))
    return np.concatenate([er, ec], axis=-1).reshape(rows * _GRID_W, d).astype(np.float32)


@functools.lru_cache(maxsize=None)
def _s5_masks():
    step = np.repeat(np.arange(_S5_CHUNK), _S5_H)
    causal = (step[None, :] >= step[:, None]).astype(np.float32)
    anti = (step[:, None] >= step[None, :]).astype(np.float32)
    return causal, anti


def _filter_kernel(z_ref, w1_ref, b1_ref, w2_ref, b2_ref, fr_ref, w3f_ref, w3b_ref, dl_ref,
                   at_ref, ab_ref, alt_ref, krt_ref, krb_ref, ki_ref, h_ref):
    @pl.when((pl.program_id(0) == 0) & (pl.program_id(1) == 0))
    def _():
        fr = fr_ref[...]
        h1 = jnp.sin(fr * (_dot_3pass(z_ref[...], w1_ref[...]) + b1_ref[...]))
        h_ref[...] = jnp.sin(fr * (_dot_3pass(h1, w2_ref[...]) + b2_ref[...]))

    h = h_ref[...]
    decay = jnp.exp(-z_ref[:, 0:1] * dl_ref[...])
    hf = _dot_3pass(h, w3f_ref[...]) * decay
    hb = _dot_3pass(h, w3b_ref[...]) * decay
    norm = (jnp.sum(jnp.abs(hf), axis=0, keepdims=True)
            + jnp.sum(jnp.abs(hb), axis=0, keepdims=True) + _EPS)
    hf = hf / norm
    hb = hb / norm
    first = lax.broadcasted_iota(jnp.int32, hf.shape, 0) == 0
    hb = jnp.where(first, 0.0, hb)
    ksum = hf + hb
    kdiff = hf - hb
    kre = jnp.dot(at_ref[...], ksum.astype(_BF16), preferred_element_type=_F32)
    kim = jnp.dot(ab_ref[...], kdiff.astype(_BF16), preferred_element_type=_F32)
    nyq = jnp.sum(alt_ref[...] * ksum, axis=0, keepdims=True)
    krt_ref[...] = kre
    krb_ref[...] = jnp.where(first, nyq, kre)
    ki_ref[...] = jnp.where(first, 0.0, kim)


def _hyena_filters(seq, d, w1p, b1, w2, b2, freq, w3):
    zpad, deltas = _filter_consts(seq, d)
    fwd, _, alt = _dft_consts(seq)
    fwd = jnp.asarray(fwd).astype(_BF16)
    top = pl.BlockSpec((seq, seq), lambda o, j: (0, 0), pipeline_mode=pl.Buffered(1))
    bot = pl.BlockSpec((seq, seq), lambda o, j: (1, 0), pipeline_mode=pl.Buffered(1))
    nb = d // _FILT_DBLK
    fo = _HY_FO
    out_sds = jax.ShapeDtypeStruct((_HY_ORDER, seq, d), _F32)
    out_spec = pl.BlockSpec((None, seq, _FILT_DBLK), lambda o, j: (o, 0, j))
    return pl.pallas_call(
        _filter_kernel,
        grid=(_HY_ORDER, nb),
        in_specs=[
            _const_spec((seq, fo)), _const_spec((fo, fo)), _const_spec((1, fo)),
            _const_spec((fo, fo)), _const_spec((1, fo)), _const_spec((1, fo)),
            pl.BlockSpec((fo, _FILT_DBLK), lambda o, j: (0, (2 * o) * nb + j)),
            pl.BlockSpec((fo, _FILT_DBLK), lambda o, j: (0, (2 * o + 1) * nb + j)),
            pl.BlockSpec((1, _FILT_DBLK), lambda o, j: (0, j)),
            top, bot,
            _const_spec((seq, 1)),
        ],
        out_specs=[out_spec, out_spec, out_spec],
        out_shape=[out_sds, out_sds, out_sds],
        scratch_shapes=[pltpu.VMEM((seq, fo), _F32)],
        compiler_params=_params(("arbitrary", "arbitrary")),
        name=f"filt{seq}",
    )(jnp.asarray(zpad), w1p, b1, w2, b2, freq, w3, w3, jnp.asarray(deltas),
      fwd, fwd, jnp.asarray(alt))


def _pre_kernel(has_pos, *refs):
    if has_pos:
        x_ref, pos_ref, sh_ref, sc_ref, g_ref, w_ref, b_ref, z_ref = refs
        x = x_ref[...] + pos_ref[...]
    else:
        x_ref, sh_ref, sc_ref, g_ref, w_ref, b_ref, z_ref = refs
        x = x_ref[...]
    h = _rmsnorm(x, g_ref[...]) * (1.0 + sc_ref[...]) + sh_ref[...]
    z_ref[...] = (_dot_bf16(h, w_ref[...]) + b_ref[...]).astype(z_ref.dtype)


def _pre_hyena(x, pos, mod5, row_of_b, layer, norm_g, w_in, b_in):
    bsz, seq, d = x.shape
    n = w_in.shape[-1]
    tm = min(_TOKEN_TILE, seq)
    tok = pl.BlockSpec((None, tm, d), lambda t, b: (b, t, 0))
    ins, specs = [x], [tok]
    if pos is not None:
        ins.append(pos)
        specs.append(pl.BlockSpec((tm, d), lambda t, b: (t, 0)))
    ins += [mod5, mod5, norm_g, w_in, b_in]
    specs += [_mod_spec(d, layer, 0, row_of_b), _mod_spec(d, layer, 1, row_of_b),
              _layer_spec((1, d), layer), _layer_spec((d, n), 0), _layer_spec((1, n), 0)]
    return pl.pallas_call(
        functools.partial(_pre_kernel, pos is not None),
        grid=(seq // tm, bsz),
        in_specs=specs,
        out_specs=pl.BlockSpec((None, tm, n), lambda t, b: (b, t, 0)),
        out_shape=jax.ShapeDtypeStruct((bsz, seq, n), _BF16),
        compiler_params=_params(("arbitrary", "arbitrary")),
        name=f"pre{seq}",
    )(*ins)


def _conv_kernel(zv_ref, z1_ref, z2_ref, wv_ref, w1_ref, w2_ref, bv_ref, b1_ref, b2_ref, fb_ref,
                 krt_ref, krb_ref, ki_ref, ft_ref, fb2_ref, it_ref, ib_ref, o_ref):
    nseq, seq, dblk = zv_ref.shape
    row = lax.broadcasted_iota(jnp.int32, (seq, dblk), 0)
    first = row == 0
    last = row == seq - 1

    def short_conv(z_ref, w_ref, b_ref, i):
        z = z_ref[i].astype(_F32)
        prev = jnp.where(first, 0.0, pltpu.roll(z, 1, 0))
        nxt = jnp.where(last, 0.0, pltpu.roll(z, seq - 1, 0))
        w = w_ref[...]
        return prev * w[0:1] + z * w[1:2] + nxt * w[2:3] + b_ref[...]

    gate_refs = ((z1_ref, w1_ref, b1_ref), (z2_ref, w2_ref, b2_ref))

    def stages(i):
        st = {}

        def start():
            st["v"] = short_conv(zv_ref, wv_ref, bv_ref, i)

        def forward():
            vb = st["v"].astype(_BF16)
            st["p"] = jnp.dot(ft_ref[...], vb, preferred_element_type=_F32)
            st["q"] = jnp.dot(fb2_ref[...], vb, preferred_element_type=_F32)

        def spectrum(o):
            p, q, ki = st.pop("p"), st.pop("q"), ki_ref[o]
            st["yt"] = (p * krt_ref[o] - q * ki).astype(_BF16)
            st["yb"] = (p * ki + q * krb_ref[o]).astype(_BF16)
            st["g"] = short_conv(*gate_refs[o], i)

        def inverse():
            st["y"] = (jnp.dot(it_ref[...], st.pop("yt"), preferred_element_type=_F32)
                       + jnp.dot(ib_ref[...], st.pop("yb"), preferred_element_type=_F32))

        def update(o):
            st["v"] = (st.pop("y") + st["v"] * fb_ref[o:o + 1, :]) * st.pop("g")
            if o == _HY_ORDER - 1:
                o_ref[i] = st.pop("v").astype(o_ref.dtype)

        pipe = [start]
        for o in range(_HY_ORDER):
            pipe += [forward, functools.partial(spectrum, o), inverse, functools.partial(update, o)]
        return pipe

    pipes = [stages(i) for i in range(nseq)]
    depth = len(pipes[0])
    for t in range(depth + nseq - 1):
        for i, pipe in enumerate(pipes):
            if 0 <= t - i < depth:
                pipe[t - i]()


def _hyena_conv(z, conv_w, conv_b, fbias, krt, krb, ki):
    bsz, seq, d3 = z.shape
    d = d3 // 3
    dblk = _CONV_DBLK
    nb = d // dblk
    fwd, inv, _ = _dft_consts(seq)
    fwd = jnp.asarray(fwd).astype(_BF16)
    inv = jnp.asarray(inv).astype(_BF16)

    def mspec(r, c):
        return pl.BlockSpec((seq, seq), lambda j, b: (r, c), pipeline_mode=pl.Buffered(1))

    nseq = _CONV_SEQS
    assert bsz % nseq == 0 and d % dblk == 0

    def zspec(k):
        return pl.BlockSpec((nseq, seq, dblk), lambda j, b: (b, 0, k * nb + j))

    def wspec(rows, k):
        return pl.BlockSpec((rows, dblk), lambda j, b: (0, k * nb + j))

    kspec = pl.BlockSpec((_HY_ORDER, seq, dblk), lambda j, b: (0, 0, j), pipeline_mode=pl.Buffered(1))
    return pl.pallas_call(
        _conv_kernel,
        grid=(nb, bsz // nseq),
        in_specs=[zspec(0), zspec(1), zspec(2),
                  wspec(3, 0), wspec(3, 1), wspec(3, 2),
                  wspec(1, 0), wspec(1, 1), wspec(1, 2),
                  wspec(_HY_ORDER, 0),
                  kspec, kspec, kspec,
                  mspec(0, 0), mspec(1, 0), mspec(0, 0), mspec(0, 1)],
        out_specs=pl.BlockSpec((nseq, seq, dblk), lambda j, b: (b, 0, j)),
        out_shape=jax.ShapeDtypeStruct((bsz, seq, d), _BF16),
        compiler_params=_params(("arbitrary", "arbitrary")),
        name=f"conv{seq}",
    )(z, z, z, conv_w, conv_w, conv_w, conv_b, conv_b, conv_b, fbias, krt, krb, ki,
      fwd, fwd, inv, inv)


def _segment_transpose(xs):
    n = len(xs)
    seg = lax.broadcasted_iota(jnp.int32, xs[0].shape, 1) // _S5_H
    diags = []
    for k in range(n):
        z = xs[-k % n]
        for m in range(1, n):
            z = jnp.where(seg == m, xs[(m - k) % n], z)
        diags.append(pltpu.roll(z, ((n - k) % n) * _S5_H, 1) if k else z)
    ys = []
    for b in range(n):
        y = diags[b]
        for a in range(1, n):
            y = jnp.where(seg == a, diags[(b - a) % n], y)
        ys.append(y)
    return ys


def _to_group_major(x, o_ref, crows):
    sm = jnp.swapaxes(x, 0, 1)
    for j in range(x.shape[-1] // 128):
        for half in range(2):
            per_group = _segment_transpose([sm[half * 8 + s8, :, j * 128:(j + 1) * 128] for s8 in range(8)])
            for gl in range(8):
                o_ref[j * 8 + gl, crows, half * 128:(half + 1) * 128] = per_group[gl]


def _from_group_major(m_ref, crows, d):
    per_step = [[None] * (d // 128) for _ in range(_S5_CHUNK)]
    for j in range(d // 128):
        for half in range(2):
            cols = slice(half * 128, (half + 1) * 128)
            steps = _segment_transpose([m_ref[j * 8 + gl, crows, cols] for gl in range(8)])
            for t8 in range(8):
                per_step[half * 8 + t8][j] = steps[t8]
    sm = jnp.stack([jnp.concatenate(tiles, axis=1) for tiles in per_step])
    return jnp.swapaxes(sm, 0, 1)


def _post_kernel(kind, tail, names, *refs):
    r = dict(zip(names, refs))
    tm, d = r["x"].shape
    steps = _S5_CHUNK
    nsub = max(tm // _POST_SUB, 1)
    sub = tm // nsub

    def stages(i):
        rows = slice(i * sub, (i + 1) * sub)
        crows = slice(i * sub // steps, (i + 1) * sub // steps)
        st = {}

        def head():
            st["x"] = r["x"][rows, :]
            if "pos" in r:
                st["x"] = st["x"] + r["pos"][rows, :]
            if kind == "hyena":
                st["mix"] = r["mix"][rows, :]
            else:
                u = _rmsnorm(st["x"], r["n1"][...]) * (1.0 + r["sc1"][...]) + r["sh1"][...]
                mix = _from_group_major(r["mix"], crows, d).reshape(sub, d)
                st["mix"] = _gelu_tanh(mix + u * r["skip"][...]).astype(_BF16)

        def mixer_out():
            st["m"] = _dot_bf16(st.pop("mix"), r["w_out"][...]) + r["b_out"][...]

        def mid():
            m = st.pop("m")
            if kind == "s5":
                half = m.shape[-1] // 2
                m = m[:, :half] * jax.nn.sigmoid(m[:, half:])
            st["x"] = st["x"] + r["g1"][...] * m
            h = _rmsnorm(st["x"], r["n2"][...]) * (1.0 + r["sc2"][...]) + r["sh2"][...]
            st["h"] = h.astype(_BF16)

        def ffn_in():
            st["ab"] = _dot_bf16(st.pop("h"), r["w13"][...])

        def act():
            ab = st.pop("ab")
            ff = ab.shape[-1] // 2
            st["act"] = (_silu(ab[:, :ff]) * ab[:, ff:]).astype(_BF16)

        def ffn_out():
            st["f"] = _dot_bf16(st.pop("act"), r["w2"][...])

        def finish():
            x = st.pop("x") + r["g2"][...] * st.pop("f")
            if tail == "next":
                r["o_x"][rows, :] = x
                un = _rmsnorm(x, r["nn"][...]) * (1.0 + r["scn"][...]) + r["shn"][...]
                _to_group_major(un.reshape(-1, steps, d), r["o_u"], crows)
            else:
                r["o_x"][rows, :] = _rmsnorm(x, r["nf"][...])

        return [head, mixer_out, mid, ffn_in, act, ffn_out, finish]

    pipes = [stages(i) for i in range(nsub)]
    depth = len(pipes[0])
    for t in range(depth + nsub - 1):
        for i, pipe in enumerate(pipes):
            if 0 <= t - i < depth:
                pipe[t - i]()


def _post_mixer(kind, tail, x, pos, mix, mod5, row_of_b, layer, w_out, b_out, norm1_g, skip,
                norm2_g, w13, w2, final_g):
    bsz, seq, d = x.shape
    tm = min(_POST_TILE, seq)
    tok = pl.BlockSpec((None, tm, d), lambda t, b: (b, t, 0))
    nt = seq // tm
    steps = _S5_CHUNK
    groups = d // _S5_H
    stepmajor = pl.BlockSpec((groups, tm // steps, _CHUNK_W), lambda t, b: (0, b * nt + t, 0))
    stepmajor_sds = jax.ShapeDtypeStruct((groups, bsz * seq // steps, _CHUNK_W), _F32)
    n_out = w_out.shape[-1]
    f2 = w13.shape[-1]

    def mod(part, lyr=layer):
        return (mod5, _mod_spec(d, lyr, part, row_of_b))

    items = [("x", x, tok), ("mix", mix, stepmajor if kind == "s5" else tok),
             ("w_out", w_out, _layer_spec((d, n_out), 0)), ("b_out", b_out, _layer_spec((1, n_out), 0)),
             ("g1",) + mod(2), ("sh2",) + mod(3), ("sc2",) + mod(4), ("g2",) + mod(5),
             ("n2", norm2_g, _layer_spec((1, d), layer)),
             ("w13", w13, _layer_spec((d, f2), layer)), ("w2", w2, _layer_spec((f2 // 2, d), layer))]
    if kind == "s5":
        items += [("n1", norm1_g, _layer_spec((1, d), layer)), ("sh1",) + mod(0), ("sc1",) + mod(1),
                  ("skip", skip, _layer_spec((1, d), 0))]
    if pos is not None:
        items.append(("pos", pos, pl.BlockSpec((tm, d), lambda t, b: (t, 0))))
    out_specs, out_shape = [tok], [jax.ShapeDtypeStruct(x.shape, _F32)]
    names_out = ["o_x"]
    if tail == "next":
        items += [("nn", norm1_g, _layer_spec((1, d), layer + 1)),
                  ("shn",) + mod(0, layer + 1), ("scn",) + mod(1, layer + 1)]
        out_specs.append(stepmajor)
        out_shape.append(stepmajor_sds)
        names_out.append("o_u")
    else:
        items.append(("nf", final_g, _const_spec((1, d))))
    names = tuple(i[0] for i in items) + tuple(names_out)
    out = pl.pallas_call(
        functools.partial(_post_kernel, kind, tail, names),
        grid=(seq // tm, bsz),
        in_specs=[i[2] for i in items], out_specs=out_specs, out_shape=out_shape,
        compiler_params=_params(("arbitrary", "arbitrary")),
        name=f"post_{kind}{seq}",
    )(*[i[1] for i in items])
    return out


_S5_GB = 8
_S5_LANES = 2 * _S5_P
_S5_MATS_GB = 8
_S5_SCAN_UNROLL = 8


def _cmul(ar, ai, br, bi):
    return ar * br - ai * bi, ar * bi + ai * br


def _s5_mats_kernel(prm_ref, btr_ref, bti_ref, cr_ref, ci_ref, causal_ref, anti_ref,
                    m_ref, w_ref, vf_ref, vb_ref, a_ref):
    c, h, lanes = _S5_CHUNK, _S5_H, _S5_LANES
    a_re = prm_ref[0:1, :]
    a_im = prm_ref[1:2, :]
    dt = jnp.exp(prm_ref[2:3, :])
    sr = dt * a_re
    ph = dt * a_im
    mag = jnp.exp(sr)
    a_one = (mag * jnp.cos(ph), mag * jnp.sin(ph))
    nr = a_one[0] - 1.0
    ni = a_one[1]
    den = a_re * a_re + a_im * a_im
    co_re = (nr * a_re + ni * a_im) / den
    co_im = (ni * a_re - nr * a_im) / den
    bt = _cmul(co_re, co_im, btr_ref[...], bti_ref[...])
    ct = (cr_ref[...], ci_ref[...])

    def powers(k):
        kf = k.astype(_F32)
        e = jnp.exp(kf * sr)
        return e * jnp.cos(kf * ph), e * jnp.sin(kf * ph)

    def per_step(x):
        return jnp.broadcast_to(x[:, None, :], (c, h, lanes)).reshape(c * h, lanes)

    def per_chan(x):
        return jnp.broadcast_to(x[None, :, :], (c, h, lanes)).reshape(c * h, lanes)

    step = lax.broadcasted_iota(jnp.int32, (c, lanes), 0)
    fwd = lax.broadcasted_iota(jnp.int32, (c, lanes), 1) < _S5_P
    tb = tuple(per_chan(x) for x in bt)
    tc = tuple(per_chan(x) for x in ct)

    asc = powers(step)
    desc = powers(-step)
    a_last = (asc[0][c - 1:c, :], asc[1][c - 1:c, :])
    a_chunk = _cmul(*a_last, *a_one)

    def table(base, p_fwd, p_bwd):
        pw = tuple(jnp.where(fwd, f, b) for f, b in zip(p_fwd, p_bwd))
        return _cmul(*base, *(per_step(x) for x in pw))

    lmat = table(tb, desc, asc)
    rmat = table(tc, asc, desc)
    wmat = table(tb, _cmul(*a_last, *desc), asc)
    vmat = table(tc, _cmul(*a_one, *asc), _cmul(*a_chunk, *desc))

    def nt3(x, y):
        xh, xl = _split_bf16(x)
        yh, yl = _split_bf16(y)
        return _dot_nt(xh, yh) + _dot_nt(xl, yh) + _dot_nt(xh, yl)

    fwd_rows = lax.broadcasted_iota(jnp.int32, (c * h, lanes), 1) < _S5_P

    def kernel_half(keep):
        return (nt3(jnp.where(keep, lmat[0], 0.0), rmat[0])
                - nt3(jnp.where(keep, lmat[1], 0.0), rmat[1]))

    m = causal_ref[...] * kernel_half(fwd_rows) + anti_ref[...] * kernel_half(jnp.logical_not(fwd_rows))
    m_ref[...] = m.astype(m_ref.dtype)
    w_ref[...] = jnp.concatenate([wmat[0], wmat[1]], axis=1).astype(w_ref.dtype)
    vcat = jnp.concatenate([vmat[0], -vmat[1]], axis=1)
    keep_f = (lax.broadcasted_iota(jnp.int32, vcat.shape, 1) % lanes) < _S5_P
    vf_ref[...] = jnp.where(keep_f, vcat, 0.0).T.astype(vf_ref.dtype)
    vb_ref[...] = jnp.where(keep_f, 0.0, vcat).T.astype(vb_ref.dtype)
    a_ref[0:1, :] = a_chunk[0]
    a_ref[1:2, :] = a_chunk[1]


def _s5_mats_block_kernel(prm_ref, btr_ref, bti_ref, cr_ref, ci_ref, causal_ref, anti_ref,
                          m_ref, w_ref, vf_ref, vb_ref, a_ref):
    for j in range(prm_ref.shape[0]):
        _s5_mats_kernel(prm_ref.at[j], btr_ref.at[j], bti_ref.at[j], cr_ref.at[j], ci_ref.at[j],
                        causal_ref, anti_ref,
                        m_ref.at[j], w_ref.at[j], vf_ref.at[j], vb_ref.at[j], a_ref.at[j])


def _s5_mats(prm, bt_re, bt_im, c_re, c_im):
    g = prm.shape[0]
    cw, lanes = _CHUNK_W, _S5_LANES
    causal, anti = (jnp.asarray(a) for a in _s5_masks())
    gm = _S5_MATS_GB
    gspec = pl.BlockSpec((gm, _S5_H, lanes), lambda i: (i, 0, 0))
    mat_spec = pl.BlockSpec((gm, cw, cw), lambda i: (i, 0, 0))
    mat_sds = jax.ShapeDtypeStruct((g, cw, cw), _BF16)
    return pl.pallas_call(
        _s5_mats_block_kernel,
        grid=(g // gm,),
        in_specs=[pl.BlockSpec((gm, 3, lanes), lambda i: (i, 0, 0)), gspec, gspec, gspec, gspec,
                  _const_spec((cw, cw)), _const_spec((cw, cw))],
        out_specs=[mat_spec] * 4 + [pl.BlockSpec((gm, 2, lanes), lambda i: (i, 0, 0))],
        out_shape=[mat_sds] * 4 + [jax.ShapeDtypeStruct((g, 2, lanes), _F32)],
        compiler_params=_params(("arbitrary",)),
        name="s5_mats",
    )(prm, bt_re, bt_im, c_re, c_im, causal, anti)


def _s5_scan_kernel(nchunk, bsz, has_init, *refs):
    if has_init:
        u_ref, m_ref, w_ref, vf_ref, vb_ref, a_ref, h0_ref, y_ref = refs[:8]
    else:
        u_ref, m_ref, w_ref, vf_ref, vb_ref, a_ref, y_ref, fin_ref = refs[:8]
    s_ref, pf_ref, pb_ref = refs[8:]
    lanes, gb = _S5_LANES, _S5_GB
    rows = nchunk * bsz
    fwd = lax.broadcasted_iota(jnp.int32, (bsz, lanes), 1) < _S5_P

    for gl in range(gb):
        u = u_ref[gl].astype(_BF16)
        y_ref[gl] = jnp.dot(u, m_ref[gl], preferred_element_type=_F32)
        s = jnp.dot(u, w_ref[gl], preferred_element_type=_F32)
        s_ref[...] = jnp.swapaxes(s.reshape(bsz, nchunk, 2 * lanes), 0, 1).reshape(rows, 2 * lanes)

        a_re = a_ref[gl, 0:1, :]
        a_im = a_ref[gl, 1:2, :]
        if has_init:
            init = (h0_ref[gl, :, 0:lanes], h0_ref[gl, :, lanes:2 * lanes])
        else:
            init = (jnp.zeros((bsz, lanes), _F32), jnp.zeros((bsz, lanes), _F32))

        def step(k, carry):
            h_re, h_im = carry
            rowf = pl.ds(pl.multiple_of(k * bsz, bsz), bsz)
            rowb = pl.ds(pl.multiple_of((nchunk - 1 - k) * bsz, bsz), bsz)
            pf_ref[rowf, 0:lanes] = h_re
            pf_ref[rowf, lanes:2 * lanes] = h_im
            pb_ref[rowb, 0:lanes] = h_re
            pb_ref[rowb, lanes:2 * lanes] = h_im
            s_re = jnp.where(fwd, s_ref[rowf, 0:lanes], s_ref[rowb, 0:lanes])
            s_im = jnp.where(fwd, s_ref[rowf, lanes:2 * lanes], s_ref[rowb, lanes:2 * lanes])
            return (a_re * h_re - a_im * h_im + s_re, a_re * h_im + a_im * h_re + s_im)

        fin = lax.fori_loop(0, nchunk, step, init, unroll=_S5_SCAN_UNROLL)
        if not has_init:
            fin_ref[gl, :, 0:lanes] = fin[0]
            fin_ref[gl, :, lanes:2 * lanes] = fin[1]
        carried = (jnp.dot(pf_ref[...].astype(_BF16), vf_ref[gl], preferred_element_type=_F32)
                   + jnp.dot(pb_ref[...].astype(_BF16), vb_ref[gl], preferred_element_type=_F32))
        carried = jnp.swapaxes(carried.reshape(nchunk, bsz, 2 * lanes), 0, 1).reshape(rows, 2 * lanes)
        y_ref[gl] = y_ref[gl] + carried


def _s5_scan(u, bsz, mats, init):
    g, rows, cw = u.shape
    nchunk = rows // bsz
    gb = _S5_GB
    tok_spec = pl.BlockSpec((gb, rows, cw), lambda i: (i, 0, 0))
    mat_spec = pl.BlockSpec((gb, cw, cw), lambda i: (i, 0, 0))
    st_spec = pl.BlockSpec((gb, bsz, cw), lambda i: (i, 0, 0))
    ins = [u] + list(mats)
    specs = [tok_spec] + [mat_spec] * 4 + [pl.BlockSpec((gb, 2, _S5_LANES), lambda i: (i, 0, 0))]
    y_sds = jax.ShapeDtypeStruct(u.shape, _F32)
    if init is not None:
        ins.append(init)
        specs.append(st_spec)
        out_specs, out_shape = tok_spec, y_sds
    else:
        out_specs = [tok_spec, st_spec]
        out_shape = [y_sds, jax.ShapeDtypeStruct((g, bsz, cw), _F32)]
    out = pl.pallas_call(
        functools.partial(_s5_scan_kernel, nchunk, bsz, init is not None),
        grid=(g // gb,),
        in_specs=specs, out_specs=out_specs, out_shape=out_shape,
        scratch_shapes=[pltpu.VMEM((rows, cw), _F32)] * 3,
        compiler_params=_params(("arbitrary",)),
        name=f"s5_{nchunk * _S5_CHUNK}",
    )(*ins)
    if init is not None:
        return out, None
    return out[0], out[1]


def _trunk(x, pos, row_of_b, shared_mod, init_state, mod5, filt, s5_mats, wts):
    bsz, seq, d = x.shape
    fold = _POST_TILE // seq if (shared_mod and seq < _POST_TILE) else 1
    rows = lambda a: a.reshape(bsz // fold, seq * fold, a.shape[-1])
    seqs = lambda a: a.reshape(bsz, seq, a.shape[-1])
    z = _pre_hyena(rows(x), pos, mod5, row_of_b, 0, wts["norm1_g"], wts["hy_in_w"], wts["hy_in_b"])
    v = _hyena_conv(seqs(z), wts["hy_conv_w"], wts["hy_conv_b"], wts["hy_fbias"], *filt)
    x1, u = _post_mixer("hyena", "next", rows(x), pos, rows(v), mod5, row_of_b, 0, wts["hy_out_w"],
                        wts["hy_out_b"], wts["norm1_g"], None, wts["norm2_g"], wts["ffn_w13"], wts["ffn_w2"],
                        None)
    ys, fin = _s5_scan(u, bsz, s5_mats, init_state)
    (out,) = _post_mixer("s5", "final", x1, None, ys, mod5, row_of_b, 1, wts["s5_glu_w"], wts["s5_glu_b"],
                         wts["norm1_g"], wts["s5_D"], wts["norm2_g"], wts["ffn_w13"], wts["ffn_w2"],
                         wts["final_g"])
    return seqs(out), fin


def kernel(x_prompt, x_sample, state_s5, c, c_ctx, norm1_g, norm2_g, final_g, ada_w, ada_b, ffn_w13, ffn_w2, hy_in_w, hy_in_b, hy_conv_w, hy_conv_b, hy_pe_w1, hy_pe_b1, hy_pe_w2, hy_pe_b2, hy_pe_w3, hy_freq, hy_fbias, hy_out_w, hy_out_b, s5_A_re, s5_A_im, s5_log_dt, s5_B_re, s5_B_im, s5_C_re, s5_C_im, s5_D, s5_glu_w, s5_glu_b):
    depth, d = norm1_g.shape
    assert depth == 2 and hy_in_w.shape[0] == 1 and s5_glu_w.shape[0] == 1
    dec_b, dec_seq, _ = x_sample.shape
    g = d // _S5_H
    p = _S5_P

    nrow = -(-(1 + dec_b) // 8) * 8
    cond = jnp.concatenate([c_ctx[None], c, jnp.zeros((nrow - 1 - dec_b, d), _F32)], axis=0)
    mod5 = _modulation(cond, ada_w, ada_b)

    row3 = lambda a: a.reshape(a.shape[0], 1, a.shape[-1])
    wts = dict(
        norm1_g=row3(norm1_g), norm2_g=row3(norm2_g), final_g=final_g[None],
        hy_in_w=hy_in_w.astype(_BF16), hy_in_b=row3(hy_in_b),
        hy_conv_w=hy_conv_w[0], hy_conv_b=hy_conv_b, hy_fbias=hy_fbias[0],
        hy_out_w=hy_out_w.astype(_BF16), hy_out_b=row3(hy_out_b),
        ffn_w13=ffn_w13.astype(_BF16), ffn_w2=ffn_w2.astype(_BF16),
        s5_glu_w=s5_glu_w.astype(_BF16), s5_glu_b=row3(s5_glu_b), s5_D=row3(s5_D),
    )
    dirs_on_lanes = lambda a: a.transpose(1, 0, 2).reshape(g, 2 * p)
    ldt = jnp.broadcast_to(s5_log_dt[0][:, :, None], (2, g, p))
    prm = jnp.stack([dirs_on_lanes(s5_A_re[0]), dirs_on_lanes(s5_A_im[0]), dirs_on_lanes(ldt)], axis=1)
    bt = lambda a: a.transpose(1, 3, 0, 2).reshape(g, _S5_H, 2 * p)
    ct = lambda a: a.transpose(1, 2, 0, 3).reshape(g, _S5_H, 2 * p)
    s5_mats = _s5_mats(prm, bt(s5_B_re[0]), bt(s5_B_im[0]), ct(s5_C_re[0]), ct(s5_C_im[0]))

    w1p = jnp.pad(hy_pe_w1[0], ((0, _HY_FO - _HY_EMB), (0, 0)))
    filt_args = (w1p, hy_pe_b1, hy_pe_w2[0], hy_pe_b2, hy_freq, hy_pe_w3[0])
    filt_ctx = _hyena_filters(x_prompt.shape[1], d, *filt_args)
    filt_lat = _hyena_filters(dec_seq, d, *filt_args)

    y_prompt, fin = _trunk(x_prompt, None, lambda b: 0, True, None, mod5, filt_ctx, s5_mats, wts)
    new_state = fin.reshape(g, -1, 2, 2, p).transpose(1, 3, 2, 0, 4)[:, None]

    init = state_s5[:, 0].transpose(3, 0, 2, 1, 4).reshape(g, dec_b, 4 * p)
    pos = jnp.asarray(_pos_embed(dec_seq // _GRID_W, d))
    y_sample, _ = _trunk(x_sample, pos, lambda b: b + 1, False, init, mod5, filt_lat, s5_mats, wts)
    return (y_prompt, y_sample, new_state)
```
